```python
import jax, jax.numpy as jnp
from jax import lax
import numpy as np

D_MODEL = 1024
BATCH = 8
SEQ = 2048
DEPTH = 2
DEC_BATCH = 128
DEC_SEQ = 1
PAST_LEN = 16384
PAGE_SIZE = 128

MIX_W = D_MODEL // 2
CHUNK = 128
A_GROUPS = 4
CONV_B = 3
POOL_WINDOWS = (2, 4, 8, 16)
POOL_CTX = 15
CONV_D = 31
N_BRANCH = 4
N_GROUPS_MOE = 4
EXPERTS_PER_GROUP = 8
N_EXPERTS = N_GROUPS_MOE * EXPERTS_PER_GROUP
TOP_K = 2
D_EXPERT = D_MODEL // 4
MOE_BLOCK = 128
EPS = 1e-6
D_IN = 2 * MIX_W + 3 * MIX_W + MIX_W + 2 * MIX_W + N_BRANCH * D_MODEL
SPLITS = (MIX_W, 2 * MIX_W, 3 * MIX_W, 4 * MIX_W, 5 * MIX_W, 6 * MIX_W, 7 * MIX_W, 8 * MIX_W)

kernel_name = 'hybrid_gmlp_conv_pool_conformer_hmoe_step'


def _rmsnorm(x, g):
    xf = x.astype(jnp.float32)
    y = xf * lax.rsqrt(jnp.mean(xf * xf, axis=-1, keepdims=True) + EPS)
    return (y * g.astype(jnp.float32)).astype(x.dtype)


def _layernorm(x, g, b):
    xf = x.astype(jnp.float32)
    mu = jnp.mean(xf, axis=-1, keepdims=True)
    xc = xf - mu
    y = xc * lax.rsqrt(jnp.mean(xc * xc, axis=-1, keepdims=True) + EPS)
    return (y * g.astype(jnp.float32) + b.astype(jnp.float32)).astype(x.dtype)


def _dwconv(x_ext, w):
    c = x_ext.shape[-1]
    return lax.conv_general_dilated(x_ext, w[:, None, :].astype(x_ext.dtype), window_strides=(1,), padding='VALID',
                                    dimension_numbers=('NWC', 'WIO', 'NWC'), feature_group_count=c)


def _spatial_gate(u, v, w_s, b_s, clen):
    n, l, c = v.shape
    nc = l // clen
    cg = c // A_GROUPS
    w = (w_s[:, :clen, :clen] * jnp.tril(jnp.ones((clen, clen), w_s.dtype))).astype(v.dtype)
    vg = v.reshape(n, nc, clen, A_GROUPS, cg)
    mixed = jnp.einsum('gts,bnsgc->bntgc', w, vg) + jnp.swapaxes(b_s[:, :clen], 0, 1)[:, :, None].astype(v.dtype)
    return (u * mixed.reshape(n, l, c)).astype(v.dtype)


def _pool_mix(p_ext, pos0, w_pg, scale):
    n, le, c = p_ext.shape
    l = le - POOL_CTX
    ef = p_ext.astype(jnp.float32)
    cs0 = jnp.concatenate([jnp.zeros((n, 1, c), jnp.float32), jnp.cumsum(ef, axis=1)], axis=1)
    pos = pos0 + jnp.arange(l)
    cg = c // len(POOL_WINDOWS)
    means = []
    for gi, w in enumerate(POOL_WINDOWS):
        lo, hi = gi * cg, (gi + 1) * cg
        s = cs0[:, POOL_CTX + 1:, lo:hi] - cs0[:, POOL_CTX + 1 - w:POOL_CTX + 1 - w + l, lo:hi]
        cnt = jnp.minimum(pos + 1, w).astype(jnp.float32)
        means.append(s / cnt[None, :, None])
    pooled = jnp.concatenate(means, axis=-1) - ef[:, POOL_CTX:]
    mixed = jnp.einsum('blgc,gcd->blgd', pooled.reshape(n, l, len(POOL_WINDOWS), cg), w_pg.astype(jnp.float32))
    return (mixed.reshape(n, l, c) * scale.astype(jnp.float32)).astype(p_ext.dtype)


def _mixer(h, st_b, st_pool, st_d, pos0, clen, pr, li):
    n, l, _ = h.shape
    proj = h @ pr['w_in'][li]
    u, v, gb, gc, xin, pin, ga, gg, gates = jnp.split(proj, SPLITS, axis=-1)
    u = jax.nn.gelu(u)
    v = _layernorm(jax.nn.gelu(v), pr['a_ln_g'][li], pr['a_ln_b'][li])
    ya = _spatial_gate(u, v, pr['a_ws'][li], pr['a_bs'][li], clen)
    cx = gc * xin
    cx_ext = jnp.concatenate([st_b.astype(cx.dtype), cx], axis=1)
    yb = gb * _dwconv(cx_ext, pr['conv_b_w'][li])
    p_ext = jnp.concatenate([st_pool.astype(pin.dtype), pin], axis=1)
    yc = _pool_mix(p_ext, pos0, pr['pool_w'][li], pr['pool_scale'][li])
    gl = ga * jax.nn.sigmoid(gg)
    gl_ext = jnp.concatenate([st_d.astype(gl.dtype), gl], axis=1)
    dc = _dwconv(gl_ext, pr['conv_d_w'][li]) + pr['conv_d_b'][li]
    yd = jax.nn.silu(_layernorm(dc, pr['d_ln_g'][li], pr['d_ln_b'][li]))
    ys = jnp.stack([ya, yb, yc, yd], axis=2)
    br = jnp.einsum('blic,icd->blid', ys, pr['w_branch'][li])
    m = jnp.sum(jax.nn.sigmoid(gates).reshape(n, l, N_BRANCH, D_MODEL) * br, axis=2)
    out = m @ pr['w_out'][li]
    return out, (cx_ext[:, -(CONV_B - 1):], p_ext[:, -POOL_CTX:], gl_ext[:, -(CONV_D - 1):], v)


def _hier_moe(h, pr, li):
    t = h.shape[0]
    hf = h.astype(jnp.float32)
    g_logits = hf @ pr['w_rg'][li].astype(jnp.float32) + pr['b_rg'][li].astype(jnp.float32)
    g_prob = jax.nn.softmax(g_logits, axis=-1)
    g_sel = jnp.argmax(g_logits, axis=-1)
    p_g = jnp.take_along_axis(g_prob, g_sel[:, None], axis=1)[:, 0]
    e_logits = (hf @ pr['w_re'][li].astype(jnp.float32) + pr['b_re'][li].astype(jnp.float32)).reshape(t, N_GROUPS_MOE, EXPERTS_PER_GROUP)
    e_sel_logits = jnp.take_along_axis(e_logits, g_sel[:, None, None], axis=1)[:, 0]
    top_p, top_i = lax.top_k(jax.nn.softmax(e_sel_logits, axis=-1), TOP_K)
    weights = p_g[:, None] * top_p / jnp.sum(top_p, axis=-1, keepdims=True)
    expert = g_sel[:, None].astype(jnp.int32) * EXPERTS_PER_GROUP + top_i.astype(jnp.int32)
    flat_e = expert.reshape(-1)
    flat_w = weights.reshape(-1)
    flat_tok = jnp.arange(t * TOP_K, dtype=jnp.int32) // TOP_K
    order = jnp.argsort(flat_e)
    se = flat_e[order]
    counts = jax.ops.segment_sum(jnp.ones_like(flat_e), flat_e, num_segments=N_EXPERTS)
    starts = jnp.cumsum(counts) - counts
    padded = (counts + MOE_BLOCK - 1) // MOE_BLOCK * MOE_BLOCK
    ends = jnp.cumsum(padded)
    pstarts = ends - padded
    dest = pstarts[se] + jnp.arange(t * TOP_K, dtype=jnp.int32) - starts[se]
    n_blocks = -(-(t * TOP_K) // MOE_BLOCK) + N_EXPERTS
    cap = n_blocks * MOE_BLOCK
    slot_tok = jnp.full((cap,), t, jnp.int32).at[dest].set(flat_tok[order])
    slot_w = jnp.zeros((cap,), jnp.float32).at[dest].set(flat_w[order])
    block_e = jnp.clip(jnp.searchsorted(ends, jnp.arange(n_blocks) * MOE_BLOCK, side='right'), 0, N_EXPERTS - 1)
    h_pad = jnp.concatenate([h, jnp.zeros((1, D_MODEL), h.dtype)], axis=0)
    xs = h_pad[slot_tok].reshape(n_blocks, MOE_BLOCK, D_MODEL)
    w_eg, w_eu, w_ed = pr['w_eg'][li], pr['w_eu'][li], pr['w_ed'][li]

    def expert_block(args):
        xb, e = args
        return (jax.nn.silu(xb @ w_eg[e]) * (xb @ w_eu[e])) @ w_ed[e]

    ys = lax.map(expert_block, (xs, block_e)).reshape(cap, D_MODEL)
    y = jax.ops.segment_sum(ys * slot_w[:, None].astype(ys.dtype), slot_tok, num_segments=t + 1)[:t]
    return y.astype(h.dtype)


def _trunk(x, c, st_b, st_pool, st_d, pos0, clen, pr):
    nb, npool, nd, nv = [], [], [], []
    for li in range(DEPTH):
        mod = (c @ pr['w_mod'][li] + pr['b_mod'][li])[:, None, :]
        sh1, sc1, g1, sh2, sc2, g2 = jnp.split(mod, 6, axis=-1)
        h = _rmsnorm(x, pr['norm1_g'][li]) * (1 + sc1) + sh1
        mo, (sb, sp, sd, v) = _mixer(h, st_b[li], st_pool[li], st_d[li], pos0, clen, pr, li)
        x = x + g1 * mo
        h2 = _rmsnorm(x, pr['norm2_g'][li]) * (1 + sc2) + sh2
        n, l, _ = x.shape
        x = x + g2 * _hier_moe(h2.reshape(n * l, D_MODEL), pr, li).reshape(n, l, D_MODEL)
        nb.append(sb)
        npool.append(sp)
        nd.append(sd)
        nv.append(v)
    y = _rmsnorm(x, pr['final_g'])
    return y, jnp.stack(nb), jnp.stack(npool), jnp.stack(nd), jnp.stack(nv)


def setup_inputs(seed: int = 0) -> dict:
    key = jax.random.key(seed)
    ks = jax.random.split(key, 40)
    f32 = jnp.float32
    nrm = lambda k, s, sc: jax.random.normal(k, s, f32) * sc
    G = len(POOL_WINDOWS)
    return {
        'x_prompt': nrm(ks[0], (BATCH, SEQ, D_MODEL), 1.0),
        'x_sample': nrm(ks[1], (DEC_BATCH, DEC_SEQ, D_MODEL), 1.0),
        'state_conv_b': nrm(ks[2], (DEPTH, DEC_BATCH, CONV_B - 1, MIX_W), 1.0),
        'state_pool': nrm(ks[3], (DEPTH, DEC_BATCH, POOL_CTX, MIX_W), 1.0),
        'state_conv_d': nrm(ks[4], (DEPTH, DEC_BATCH, CONV_D - 1, MIX_W), 0.5),
        'c_prompt': nrm(ks[5], (BATCH, D_MODEL), 1.0),
        'c_sample': nrm(ks[6], (DEC_BATCH, D_MODEL), 1.0),
        'w_mod': nrm(ks[7], (DEPTH, D_MODEL, 6 * D_MODEL), 0.5 * D_MODEL ** -0.5),
        'b_mod': nrm(ks[8], (DEPTH, 6 * D_MODEL), 0.02),
        'norm1_g': 1.0 + nrm(ks[9], (DEPTH, D_MODEL), 0.1),
        'norm2_g': 1.0 + nrm(ks[10], (DEPTH, D_MODEL), 0.1),
        'w_in': nrm(ks[11], (DEPTH, D_MODEL, D_IN), D_MODEL ** -0.5),
        'a_ln_g': 1.0 + nrm(ks[12], (DEPTH, MIX_W), 0.1),
        'a_ln_b': nrm(ks[13], (DEPTH, MIX_W), 0.02),
        'a_ws': nrm(ks[14], (DEPTH, A_GROUPS, CHUNK, CHUNK), CHUNK ** -0.5),
        'a_bs': 1.0 + nrm(ks[15], (DEPTH, A_GROUPS, CHUNK), 0.1),
        'conv_b_w': nrm(ks[16], (DEPTH, CONV_B, MIX_W), CONV_B ** -0.5),
        'pool_w': nrm(ks[17], (DEPTH, G, MIX_W // G, MIX_W // G), (MIX_W // G) ** -0.5),
        'pool_scale': 1.0 + nrm(ks[18], (DEPTH, MIX_W), 0.1),
        'conv_d_w': nrm(ks[19], (DEPTH, CONV_D, MIX_W), CONV_D ** -0.5),
        'conv_d_b': nrm(ks[20], (DEPTH, MIX_W), 0.02),
        'd_ln_g': 1.0 + nrm(ks[21], (DEPTH, MIX_W), 0.1),
        'd_ln_b': nrm(ks[22], (DEPTH, MIX_W), 0.02),
        'w_branch': nrm(ks[23], (DEPTH, N_BRANCH, MIX_W, D_MODEL), MIX_W ** -0.5),
        'w_out': nrm(ks[24], (DEPTH, D_MODEL, D_MODEL), D_MODEL ** -0.5),
        'w_rg': nrm(ks[25], (DEPTH, D_MODEL, N_GROUPS_MOE), D_MODEL ** -0.5),
        'b_rg': nrm(ks[26], (DEPTH, N_GROUPS_MOE), 0.01),
        'w_re': nrm(ks[27], (DEPTH, D_MODEL, N_EXPERTS), D_MODEL ** -0.5),
        'b_re': nrm(ks[28], (DEPTH, N_EXPERTS), 0.01),
        'w_eg': nrm(ks[29], (DEPTH, N_EXPERTS, D_MODEL, D_EXPERT), D_MODEL ** -0.5),
        'w_eu': nrm(ks[30], (DEPTH, N_EXPERTS, D_MODEL, D_EXPERT), D_MODEL ** -0.5),
        'w_ed': nrm(ks[31], (DEPTH, N_EXPERTS, D_EXPERT, D_MODEL), D_EXPERT ** -0.5),
        'final_g': 1.0 + nrm(ks[32], (D_MODEL,), 0.1),
    }


def reference(x_prompt, x_sample, state_conv_b, state_pool, state_conv_d, c_prompt, c_sample,
              w_mod, b_mod, norm1_g, norm2_g, w_in, a_ln_g, a_ln_b, a_ws, a_bs, conv_b_w,
              pool_w, pool_scale, conv_d_w, conv_d_b, d_ln_g, d_ln_b, w_branch, w_out,
              w_rg, b_rg, w_re, b_re, w_eg, w_eu, w_ed, final_g):
    pr = {'w_mod': w_mod, 'b_mod': b_mod, 'norm1_g': norm1_g, 'norm2_g': norm2_g, 'w_in': w_in,
          'a_ln_g': a_ln_g, 'a_ln_b': a_ln_b, 'a_ws': a_ws, 'a_bs': a_bs, 'conv_b_w': conv_b_w,
          'pool_w': pool_w, 'pool_scale': pool_scale, 'conv_d_w': conv_d_w, 'conv_d_b': conv_d_b,
          'd_ln_g': d_ln_g, 'd_ln_b': d_ln_b, 'w_branch': w_branch, 'w_out': w_out,
          'w_rg': w_rg, 'b_rg': b_rg, 'w_re': w_re, 'b_re': b_re,
          'w_eg': w_eg, 'w_eu': w_eu, 'w_ed': w_ed, 'final_g': final_g}
    dt = x_prompt.dtype
    z_b = jnp.zeros((DEPTH, BATCH, CONV_B - 1, MIX_W), dt)
    z_p = jnp.zeros((DEPTH, BATCH, POOL_CTX, MIX_W), dt)
    z_d = jnp.zeros((DEPTH, BATCH, CONV_D - 1, MIX_W), dt)
    y_prompt, nb_p, np_p, nd_p, _ = _trunk(x_prompt, c_prompt, z_b, z_p, z_d, 0, CHUNK, pr)
    y_sample, nb_s, np_s, nd_s, nv_s = _trunk(x_sample, c_sample, state_conv_b, state_pool, state_conv_d,
                                              PAST_LEN, min(DEC_SEQ, CHUNK), pr)
    return (y_prompt, y_sample, nb_p, np_p, nd_p, nb_s, np_s, nd_s, nv_s)
```

```python
import functools

import jax
import jax.numpy as jnp
from jax import lax
from jax.experimental import pallas as pl
from jax.experimental.pallas import tpu as pltpu

F32 = jnp.float32
BF16 = jnp.bfloat16

D_MODEL = 1024
MIX_W = D_MODEL // 2
CHUNK = 128
A_GROUPS = 4
CONV_B = 3
POOL_WINDOWS = (2, 4, 8, 16)
POOL_CTX = 15
CONV_D = 31
N_BRANCH = 4
N_GROUPS_MOE = 4
EXPERTS_PER_GROUP = 8
N_EXPERTS = N_GROUPS_MOE * EXPERTS_PER_GROUP
D_EXPERT = D_MODEL // 4
EPS = 1e-6
D_IN = 8 * MIX_W + N_BRANCH * D_MODEL
PAST_LEN = 16384

LANES = 128
SUBLANES = 8
ROUTE_LANE0 = N_GROUPS_MOE

MIX_TM = 256
CONV_ROWS = 32
CTX_B = 8
CTX_P = 16
CTX_D = 32
MOE_BM = 128
ROUTE_TR = 512
COMB_TC = 256
MIXER_VMEM_BYTES = 56 * 1024 * 1024


def _rms(x, g):
    return x * lax.rsqrt(jnp.mean(x * x, axis=-1, keepdims=True) + EPS) * g


def _ln(x, g, b):
    mu = jnp.mean(x, axis=-1, keepdims=True)
    xc = x - mu
    return xc * lax.rsqrt(jnp.mean(xc * xc, axis=-1, keepdims=True) + EPS) * g + b


def _bdot(a, b):
    return jnp.dot(a.astype(BF16), b.astype(BF16), preferred_element_type=F32)


def _mod_kernel(c_ref, w_ref, b_ref, o_ref):
    o_ref[0] = _bdot(c_ref[...], w_ref[0]) + b_ref[0]


def _modulation(c_all, w_mod, b_mod):
    depth = w_mod.shape[0]
    rows = c_all.shape[0]
    tn = 1536
    return pl.pallas_call(
        _mod_kernel,
        grid=(depth, 6 * D_MODEL // tn),
        in_specs=[pl.BlockSpec((rows, D_MODEL), lambda l, n: (0, 0)),
                  pl.BlockSpec((1, D_MODEL, tn), lambda l, n: (l, 0, n)),
                  pl.BlockSpec((1, 1, tn), lambda l, n: (l, 0, n))],
        out_specs=pl.BlockSpec((1, rows, tn), lambda l, n: (l, 0, n)),
        out_shape=jax.ShapeDtypeStruct((depth, rows, 6 * D_MODEL), F32),
        name="modulation",
    )(c_all, w_mod, b_mod.reshape(depth, 1, 6 * D_MODEL))


def _project(h, win_ref, proj_ref):
    hb = h.astype(BF16)
    for c in range(0, D_IN, D_MODEL):
        proj_ref[:, c:c + D_MODEL] = jnp.dot(hb, win_ref[:, c:c + D_MODEL], preferred_element_type=F32)


def _merge_tail(x, g1, sh2, sc2, n2g, proj_ref, ys_ref, wbr_ref, wout_ref, wr_ref, br_ref):
    m = None
    for i in range(N_BRANCH):
        lo = 8 * MIX_W + i * D_MODEL
        br = jnp.dot(ys_ref[i], wbr_ref[i], preferred_element_type=F32)
        t = jax.nn.sigmoid(proj_ref[:, lo:lo + D_MODEL]) * br
        m = t if m is None else m + t
    out = jnp.dot(m.astype(BF16), wout_ref[...], preferred_element_type=F32)
    x1 = x + g1 * out
    h2 = _rms(x1, n2g) * (1.0 + sc2) + sh2
    logits = jnp.dot(h2, wr_ref[...], preferred_element_type=F32, precision=lax.Precision.HIGHEST) + br_ref[...]
    return x1, h2, logits


def _prompt_mixer_kernel(x_ref, mod_ref, n1g_ref, n2g_ref, win_ref, alng_ref, alnb_ref, aws_ref, absf_ref,
                         cbw_ref, poolw_ref, pscale_ref, cdw_ref, cdb_ref, dlng_ref, dlnb_ref,
                         wbr_ref, wout_ref, wr_ref, br_ref,
                         x1_ref, h2_ref, lg_ref, nb_ref, np_ref, nd_ref,
                         proj_ref, ys_ref, cx_ext, p_ext, gl_ext):
    tm = x_ref.shape[1]
    j = pl.program_id(1)

    @pl.when(j == 0)
    def _():
        cx_ext[0:CTX_B, :] = jnp.zeros((CTX_B, MIX_W), F32)
        p_ext[0:CTX_P, :] = jnp.zeros((CTX_P, MIX_W), F32)
        gl_ext[0:CTX_D, :] = jnp.zeros((CTX_D, MIX_W), F32)

    x = x_ref[0]
    mod = mod_ref[0]
    sh1, sc1, g1, sh2, sc2 = (mod[:, i * D_MODEL:(i + 1) * D_MODEL] for i in range(5))
    h = _rms(x, n1g_ref[...]) * (1.0 + sc1) + sh1
    _project(h, win_ref, proj_ref)

    u = jax.nn.gelu(proj_ref[:, 0:MIX_W])
    v = _ln(jax.nn.gelu(proj_ref[:, MIX_W:2 * MIX_W]), alng_ref[...], alnb_ref[...])
    vb = v.astype(BF16)
    rr = lax.broadcasted_iota(jnp.int32, (CHUNK, CHUNK), 0)
    cc = lax.broadcasted_iota(jnp.int32, (CHUNK, CHUNK), 1)
    wtril = [jnp.where(rr >= cc, aws_ref[g], 0.0).astype(BF16) for g in range(A_GROUPS)]
    cg = MIX_W // A_GROUPS
    for c0 in range(0, tm, CHUNK):
        mixed = jnp.concatenate(
            [jnp.dot(wtril[g], vb[c0:c0 + CHUNK, g * cg:(g + 1) * cg], preferred_element_type=F32)
             for g in range(A_GROUPS)], axis=1) + absf_ref[...]
        ys_ref[0, c0:c0 + CHUNK, :] = (u[c0:c0 + CHUNK] * mixed).astype(BF16)

    cx_ext[CTX_B:CTX_B + tm, :] = proj_ref[:, 3 * MIX_W:4 * MIX_W] * proj_ref[:, 4 * MIX_W:5 * MIX_W]
    for r0 in range(0, tm, CONV_ROWS):
        acc = None
        for k in range(CONV_B):
            t = cbw_ref[k:k + 1, :] * cx_ext[pl.ds(r0 + CTX_B - (CONV_B - 1) + k, CONV_ROWS), :]
            acc = t if acc is None else acc + t
        ys_ref[1, r0:r0 + CONV_ROWS, :] = (proj_ref[r0:r0 + CONV_ROWS, 2 * MIX_W:3 * MIX_W] * acc).astype(BF16)
    nb_ref[0] = cx_ext[pl.ds(CTX_B + tm - (CONV_B - 1), CONV_B - 1), :]
    cx_ext[0:CTX_B, :] = cx_ext[tm:tm + CTX_B, :]

    p_ext[CTX_P:CTX_P + tm, :] = proj_ref[:, 5 * MIX_W:6 * MIX_W]
    pos = j * tm + lax.broadcasted_iota(jnp.int32, (tm, 1), 0)
    pg = MIX_W // len(POOL_WINDOWS)
    mixed = []
    for gi, w in enumerate(POOL_WINDOWS):
        lo = gi * pg
        s = None
        for i in range(w):
            t = p_ext[pl.ds(CTX_P - i, tm), lo:lo + pg]
            s = t if s is None else s + t
        cnt = jnp.minimum(pos + 1, w).astype(F32)
        pooled = s / cnt - p_ext[CTX_P:CTX_P + tm, lo:lo + pg]
        mixed.append(_bdot(pooled, poolw_ref[gi]))
    ys_ref[2] = (jnp.concatenate(mixed, axis=1) * pscale_ref[...]).astype(BF16)
    np_ref[0] = p_ext[pl.ds(CTX_P + tm - POOL_CTX, POOL_CTX), :]
    p_ext[0:CTX_P, :] = p_ext[tm:tm + CTX_P, :]

    gl_ext[CTX_D:CTX_D + tm, :] = proj_ref[:, 6 * MIX_W:7 * MIX_W] * jax.nn.sigmoid(proj_ref[:, 7 * MIX_W:8 * MIX_W])
    for r0 in range(0, tm, CONV_ROWS):
        acc = None
        for k in range(CONV_D):
            t = cdw_ref[k:k + 1, :] * gl_ext[pl.ds(r0 + CTX_D - (CONV_D - 1) + k, CONV_ROWS), :]
            acc = t if acc is None else acc + t
        dc = acc + cdb_ref[...]
        ys_ref[3, r0:r0 + CONV_ROWS, :] = jax.nn.silu(_ln(dc, dlng_ref[...], dlnb_ref[...])).astype(BF16)
    nd_ref[0] = gl_ext[pl.ds(CTX_D + tm - (CONV_D - 1), CONV_D - 1), :]
    gl_ext[0:CTX_D, :] = gl_ext[tm:tm + CTX_D, :]

    x1, h2, logits = _merge_tail(x, g1, sh2, sc2, n2g_ref[...], proj_ref, ys_ref, wbr_ref, wout_ref, wr_ref, br_ref)
    x1_ref[0] = x1
    h2_ref[0] = h2
    lg_ref[0] = logits


def _const_spec(shape):
    nd = len(shape)
    return pl.BlockSpec(shape, lambda *_: (0,) * nd, pipeline_mode=pl.Buffered(1))


def _prompt_mixer(x, mod, lw):
    n, l, _ = x.shape
    tm = MIX_TM
    consts = [lw["n1g"], lw["n2g"], lw["w_in"], lw["a_ln_g"], lw["a_ln_b"], lw["a_ws"], lw["a_bs_full"],
              lw["conv_b_w"], lw["pool_w"], lw["pool_scale"], lw["conv_d_w"], lw["conv_d_b"], lw["d_ln_g"], lw["d_ln_b"],
              lw["w_branch"], lw["w_out"], lw["w_r"], lw["b_r"]]
    tok = lambda last: pl.BlockSpec((1, tm, last), lambda b, j: (b, j, 0))
    st = lambda rows: pl.BlockSpec((1, rows, MIX_W), lambda b, j: (b, 0, 0))
    return pl.pallas_call(
        _prompt_mixer_kernel,
        grid=(n, l // tm),
        in_specs=[tok(D_MODEL), pl.BlockSpec((1, 1, 6 * D_MODEL), lambda b, j: (b, 0, 0))]
                 + [_const_spec(c.shape) for c in consts],
        out_specs=[tok(D_MODEL), tok(D_MODEL), tok(LANES), st(CONV_B - 1), st(POOL_CTX), st(CONV_D - 1)],
        out_shape=[jax.ShapeDtypeStruct((n, l, D_MODEL), F32), jax.ShapeDtypeStruct((n, l, D_MODEL), F32),
                   jax.ShapeDtypeStruct((n, l, LANES), F32),
                   jax.ShapeDtypeStruct((n, CONV_B - 1, MIX_W), F32), jax.ShapeDtypeStruct((n, POOL_CTX, MIX_W), F32),
                   jax.ShapeDtypeStruct((n, CONV_D - 1, MIX_W), F32)],
        scratch_shapes=[pltpu.VMEM((tm, D_IN), F32), pltpu.VMEM((N_BRANCH, tm, MIX_W), BF16),
                        pltpu.VMEM((CTX_B + tm, MIX_W), F32), pltpu.VMEM((CTX_P + tm, MIX_W), F32),
                        pltpu.VMEM((CTX_D + tm, MIX_W), F32)],
        compiler_params=pltpu.CompilerParams(dimension_semantics=("arbitrary", "arbitrary"),
                                             vmem_limit_bytes=MIXER_VMEM_BYTES),
        name="prompt_mixer",
    )(x, mod, *consts)


def _sample_mixer_kernel(x_ref, mod_ref, stb_ref, stp_ref, std_ref,
                         n1g_ref, n2g_ref, win_ref, alng_ref, alnb_ref, aw0_ref, ab0_ref,
                         cbw_ref, poolw_ref, pscale_ref, cdw_ref, cdb_ref, dlng_ref, dlnb_ref,
                         wbr_ref, wout_ref, wr_ref, br_ref,
                         x1_ref, h2_ref, lg_ref, cx_ref, pin_ref, gl_ref, v_ref,
                         proj_ref, ys_ref):
    x = x_ref[...]
    mod = mod_ref[...]
    sh1, sc1, g1, sh2, sc2 = (mod[:, i * D_MODEL:(i + 1) * D_MODEL] for i in range(5))
    h = _rms(x, n1g_ref[...]) * (1.0 + sc1) + sh1
    _project(h, win_ref, proj_ref)

    u = jax.nn.gelu(proj_ref[:, 0:MIX_W])
    v = _ln(jax.nn.gelu(proj_ref[:, MIX_W:2 * MIX_W]), alng_ref[...], alnb_ref[...])
    v_ref[...] = v
    ys_ref[0] = (u * (aw0_ref[...] * v + ab0_ref[...])).astype(BF16)

    cx = proj_ref[:, 3 * MIX_W:4 * MIX_W] * proj_ref[:, 4 * MIX_W:5 * MIX_W]
    cx_ref[...] = cx
    acc = cbw_ref[CONV_B - 1:CONV_B, :] * cx
    for k in range(CONV_B - 1):
        acc = acc + cbw_ref[k:k + 1, :] * stb_ref[:, k * MIX_W:(k + 1) * MIX_W]
    ys_ref[1] = (proj_ref[:, 2 * MIX_W:3 * MIX_W] * acc).astype(BF16)

    pin = proj_ref[:, 5 * MIX_W:6 * MIX_W]
    pin_ref[...] = pin
    pg = MIX_W // len(POOL_WINDOWS)
    mixed = []
    for gi, w in enumerate(POOL_WINDOWS):
        lo = gi * pg
        s = pin[:, lo:lo + pg]
        for i in range(1, w):
            r = POOL_CTX - i
            s = s + stp_ref[:, r * MIX_W + lo:r * MIX_W + lo + pg]
        pooled = s / float(min(PAST_LEN + 1, w)) - pin[:, lo:lo + pg]
        mixed.append(_bdot(pooled, poolw_ref[gi]))
    ys_ref[2] = (jnp.concatenate(mixed, axis=1) * pscale_ref[...]).astype(BF16)

    gl = proj_ref[:, 6 * MIX_W:7 * MIX_W] * jax.nn.sigmoid(proj_ref[:, 7 * MIX_W:8 * MIX_W])
    gl_ref[...] = gl
    acc = cdw_ref[CONV_D - 1:CONV_D, :] * gl
    for k in range(CONV_D - 1):
        acc = acc + cdw_ref[k:k + 1, :] * std_ref[:, k * MIX_W:(k + 1) * MIX_W]
    dc = acc + cdb_ref[...]
    ys_ref[3] = jax.nn.silu(_ln(dc, dlng_ref[...], dlnb_ref[...])).astype(BF16)

    x1, h2, logits = _merge_tail(x, g1, sh2, sc2, n2g_ref[...], proj_ref, ys_ref, wbr_ref, wout_ref, wr_ref, br_ref)
    x1_ref[...] = x1
    h2_ref[...] = h2
    lg_ref[...] = logits


def _sample_mixer(x, mod, st_b, st_p, st_d, lw):
    rows = x.shape[0]
    ins = [x, mod, st_b, st_p, st_d,
           lw["n1g"], lw["n2g"], lw["w_in"], lw["a_ln_g"], lw["a_ln_b"], lw["a_w0"], lw["a_b0"],
           lw["conv_b_w"], lw["pool_w"], lw["pool_scale"], lw["conv_d_w"], lw["conv_d_b"], lw["d_ln_g"], lw["d_ln_b"],
           lw["w_branch"], lw["w_out"], lw["w_r"], lw["b_r"]]
    mk = lambda last: jax.ShapeDtypeStruct((rows, last), F32)
    return pl.pallas_call(
        _sample_mixer_kernel,
        out_shape=[mk(D_MODEL), mk(D_MODEL), mk(LANES), mk(MIX_W), mk(MIX_W), mk(MIX_W), mk(MIX_W)],
        scratch_shapes=[pltpu.VMEM((rows, D_IN), F32), pltpu.VMEM((N_BRANCH, rows, MIX_W), BF16)],
        compiler_params=pltpu.CompilerParams(vmem_limit_bytes=MIXER_VMEM_BYTES),
        name="sample_mixer",
    )(*ins)


def _route_kernel(lg_ref, route_ref, cnt_ref, carry_ref):
    tr = lg_ref.shape[0]
    i = pl.program_id(0)

    @pl.when(i == 0)
    def _():
        carry_ref[...] = jnp.zeros_like(carry_ref)

    lg = lg_ref[...]
    lane = lax.broadcasted_iota(jnp.int32, (tr, LANES), 1)
    lanef = lane.astype(F32)
    big = float(LANES)

    def first_argmax(vals):
        mx = jnp.max(vals, axis=-1, keepdims=True)
        return mx, jnp.min(jnp.where(vals == mx, lanef, big), axis=-1, keepdims=True)

    neg = -jnp.inf
    gl = jnp.where(lane < N_GROUPS_MOE, lg, neg)
    gmax, gsel = first_argmax(gl)
    p_g = 1.0 / jnp.sum(jnp.exp(gl - gmax), axis=-1, keepdims=True)
    e_lo = ROUTE_LANE0 + EXPERTS_PER_GROUP * gsel
    emask = jnp.logical_and(lanef >= e_lo, lanef < e_lo + EXPERTS_PER_GROUP)
    el = jnp.where(emask, lg, neg)
    emax = jnp.max(el, axis=-1, keepdims=True)
    ex = jnp.exp(el - emax)
    prob = jnp.where(emask, ex / jnp.sum(ex, axis=-1, keepdims=True), -1.0)
    p1, i1 = first_argmax(prob)
    p2, i2 = first_argmax(jnp.where(lanef == i1, -1.0, prob))
    w1 = p_g * p1 / (p1 + p2)
    w2 = p_g * p2 / (p1 + p2)

    hot1 = lanef == i1
    hot2 = lanef == i2
    hot = jnp.logical_or(hot1, hot2).astype(F32)
    rr = lax.broadcasted_iota(jnp.int32, (tr, tr), 0)
    cc = lax.broadcasted_iota(jnp.int32, (tr, tr), 1)
    before = jnp.dot((rr > cc).astype(BF16), hot.astype(BF16), preferred_element_type=F32) + carry_ref[...]
    rank1 = jnp.sum(jnp.where(hot1, before, 0.0), axis=-1, keepdims=True)
    rank2 = jnp.sum(jnp.where(hot2, before, 0.0), axis=-1, keepdims=True)
    carry_ref[...] = carry_ref[...] + jnp.sum(hot, axis=0, keepdims=True)
    cnt_ref[...] = carry_ref[...]

    fields = (i1 - ROUTE_LANE0, i2 - ROUTE_LANE0, rank1, rank2, w1, w2)
    out = jnp.zeros((tr, LANES), F32)
    for k, f in enumerate(fields):
        out = jnp.where(lane == k, f, out)
    route_ref[...] = out


def _route(logits):
    t = logits.shape[0]
    tr = min(t, ROUTE_TR)
    return pl.pallas_call(
        _route_kernel,
        grid=(t // tr,),
        in_specs=[pl.BlockSpec((tr, LANES), lambda i: (i, 0))],
        out_specs=[pl.BlockSpec((tr, LANES), lambda i: (i, 0)), pl.BlockSpec((1, LANES), lambda i: (0, 0))],
        out_shape=[jax.ShapeDtypeStruct((t, LANES), F32), jax.ShapeDtypeStruct((1, LANES), F32)],
        scratch_shapes=[pltpu.VMEM((1, LANES), F32)],
        compiler_params=pltpu.CompilerParams(dimension_semantics=("arbitrary",)),
        name="route",
    )(logits)


def _expert_kernel(be_ref, tok_ref, nu_ref, h2_hbm, wg_ref, wu_ref, wd_ref, ys_ref, buf, sem, wgb, wub, wdb):
    bm = buf.shape[1]
    b = pl.program_id(0)
    nu = nu_ref[0]
    slot = b % 2

    def issue(blk, s):
        def body(r, carry):
            tok = tok_ref[blk * bm + r]
            pltpu.make_async_copy(h2_hbm.at[pl.ds(tok, 1)], buf.at[s, pl.ds(r, 1)], sem.at[s]).start()
            return carry
        lax.fori_loop(0, bm, body, 0)

    @pl.when(b == 0)
    def _():
        issue(0, 0)

    @pl.when(b + 1 < nu)
    def _():
        issue(b + 1, 1 - slot)

    @pl.when(b >= nu)
    def _():
        ys_ref[...] = jnp.zeros_like(ys_ref)

    @pl.when(b < nu)
    def _():
        pltpu.make_async_copy(h2_hbm.at[pl.ds(0, bm)], buf.at[slot], sem.at[slot]).wait()

        @pl.when(jnp.logical_or(b == 0, be_ref[b] != be_ref[jnp.maximum(b - 1, 0)]))
        def _():
            wgb[...] = wg_ref[0, 0].astype(BF16)
            wub[...] = wu_ref[0, 0].astype(BF16)
            wdb[...] = wd_ref[0, 0].astype(BF16)

        xb = buf[slot].astype(BF16)
        g = jnp.dot(xb, wgb[...], preferred_element_type=F32)
        u = jnp.dot(xb, wub[...], preferred_element_type=F32)
        ys_ref[...] = jnp.dot((jax.nn.silu(g) * u).astype(BF16), wdb[...], preferred_element_type=F32)


def _experts(h2, slot_tok, block_e, n_used, li, w_eg, w_eu, w_ed):
    n_blocks = block_e.shape[0]
    bm = MOE_BM
    wspec = lambda shape: pl.BlockSpec((1, 1) + shape, lambda b, be, tok, nu: (li, be[b], 0, 0))
    return pl.pallas_call(
        _expert_kernel,
        grid_spec=pltpu.PrefetchScalarGridSpec(
            num_scalar_prefetch=3,
            grid=(n_blocks,),
            in_specs=[pl.BlockSpec(memory_space=pl.ANY), wspec((D_MODEL, D_EXPERT)), wspec((D_MODEL, D_EXPERT)),
                      wspec((D_EXPERT, D_MODEL))],
            out_specs=pl.BlockSpec((bm, D_MODEL), lambda b, be, tok, nu: (b, 0)),
            scratch_shapes=[pltpu.VMEM((2, bm, D_MODEL), F32), pltpu.SemaphoreType.DMA((2,)),
                            pltpu.VMEM((D_MODEL, D_EXPERT), BF16), pltpu.VMEM((D_MODEL, D_EXPERT), BF16),
                            pltpu.VMEM((D_EXPERT, D_MODEL), BF16)]),
        out_shape=jax.ShapeDtypeStruct((n_blocks * bm, D_MODEL), F32),
        compiler_params=pltpu.CompilerParams(dimension_semantics=("arbitrary",)),
        name="experts",
    )(block_e, slot_tok, n_used, h2, w_eg, w_eu, w_ed)


def _combine_kernel(pos_ref, ys_hbm, x1_ref, route_ref, g2_ref, fg_ref, out_ref, buf, sem, *, final):
    tc = x1_ref.shape[0]
    i = pl.program_id(0)
    n = pl.num_programs(0)
    slot = i % 2

    def issue(tile, s):
        def body(r, carry):
            for k in range(2):
                p = pos_ref[(tile * tc + r) * 2 + k]
                pltpu.make_async_copy(ys_hbm.at[pl.ds(p, 1)], buf.at[s, k, pl.ds(r, 1)], sem.at[s]).start()
            return carry
        lax.fori_loop(0, tc, body, 0)

    @pl.when(i == 0)
    def _():
        issue(0, 0)

    @pl.when(i + 1 < n)
    def _():
        issue(i + 1, 1 - slot)

    for k in range(2):
        pltpu.make_async_copy(ys_hbm.at[pl.ds(0, tc)], buf.at[slot, k], sem.at[slot]).wait()
    route = route_ref[...]
    y = route[:, 4:5] * buf[slot, 0] + route[:, 5:6] * buf[slot, 1]
    x2 = x1_ref[...] + g2_ref[...].reshape(-1, D_MODEL) * y
    out_ref[...] = _rms(x2, fg_ref[...]) if final else x2


def _combine(ys, pos, x1, route, g2, fg, final):
    t = x1.shape[0]
    tc = min(t, COMB_TC)
    if g2.ndim == 3:
        seq = t // g2.shape[0]
        g2spec = pl.BlockSpec((1, 1, D_MODEL), lambda i, pos: (i * tc // seq, 0, 0))
    else:
        g2spec = pl.BlockSpec((tc, D_MODEL), lambda i, pos: (i, 0))
    tokspec = lambda last: pl.BlockSpec((tc, last), lambda i, pos: (i, 0))
    return pl.pallas_call(
        functools.partial(_combine_kernel, final=final),
        grid_spec=pltpu.PrefetchScalarGridSpec(
            num_scalar_prefetch=1,
            grid=(t // tc,),
            in_specs=[pl.BlockSpec(memory_space=pl.ANY), tokspec(D_MODEL), tokspec(LANES), g2spec,
                      pl.BlockSpec((1, D_MODEL), lambda i, pos: (0, 0))],
            out_specs=tokspec(D_MODEL),
            scratch_shapes=[pltpu.VMEM((2, 2, tc, D_MODEL), F32), pltpu.SemaphoreType.DMA((2,))]),
        out_shape=jax.ShapeDtypeStruct((t, D_MODEL), F32),
        compiler_params=pltpu.CompilerParams(dimension_semantics=("arbitrary",)),
        name="combine",
    )(pos, ys, x1, route, g2, fg)


def _moe(h2, logits, x1, g2, fg, final, li, w_eg, w_eu, w_ed):
    t = h2.shape[0]
    bm = MOE_BM
    route, counts = _route(logits)
    expert = route[:, 0:2].astype(jnp.int32)
    rank = route[:, 2:4].astype(jnp.int32)
    cnt = counts[0, ROUTE_LANE0:ROUTE_LANE0 + N_EXPERTS].astype(jnp.int32)
    padded = (cnt + bm - 1) // bm * bm
    ends = jnp.cumsum(padded)
    pstarts = ends - padded
    dest = (pstarts[expert] + rank).reshape(-1)
    n_blocks = (t * 2) // bm + N_EXPERTS
    flat_tok = jnp.arange(t * 2, dtype=jnp.int32) // 2
    slot_tok = jnp.zeros((n_blocks * bm,), jnp.int32).at[dest].set(flat_tok)
    block_e = jnp.clip(jnp.searchsorted(ends, jnp.arange(n_blocks, dtype=jnp.int32) * bm, side="right"),
                       0, N_EXPERTS - 1).astype(jnp.int32)
    n_used = (ends[-1:] // bm).astype(jnp.int32)
    ys = _experts(h2, slot_tok, block_e, n_used, li, w_eg, w_eu, w_ed)
    return _combine(ys, dest.astype(jnp.int32), x1, route, g2, fg, final)


def _layer_weights(li, norm1_g, norm2_g, w_in, a_ln_g, a_ln_b, a_ws, a_bs, conv_b_w, pool_w, pool_scale,
                   conv_d_w, conv_d_b, d_ln_g, d_ln_b, w_branch, w_out, w_rg, b_rg, w_re, b_re):
    row = lambda a: a[li].reshape(1, -1)
    cg = MIX_W // A_GROUPS
    pad = LANES - N_GROUPS_MOE - N_EXPERTS
    return {
        "n1g": row(norm1_g), "n2g": row(norm2_g), "w_in": w_in[li].astype(BF16),
        "a_ln_g": row(a_ln_g), "a_ln_b": row(a_ln_b), "a_ws": a_ws[li],
        "a_bs_full": jnp.repeat(a_bs[li].T, cg, axis=1),
        "a_w0": jnp.repeat(a_ws[li, :, 0, 0], cg).reshape(1, MIX_W),
        "a_b0": jnp.repeat(a_bs[li, :, 0], cg).reshape(1, MIX_W),
        "conv_b_w": conv_b_w[li], "pool_w": pool_w[li], "pool_scale": row(pool_scale),
        "conv_d_w": conv_d_w[li], "conv_d_b": row(conv_d_b), "d_ln_g": row(d_ln_g), "d_ln_b": row(d_ln_b),
        "w_branch": w_branch[li].astype(BF16), "w_out": w_out[li].astype(BF16),
        "w_r": jnp.concatenate([w_rg[li], w_re[li], jnp.zeros((D_MODEL, pad), F32)], axis=1),
        "b_r": jnp.concatenate([b_rg[li], b_re[li], jnp.zeros((pad,), F32)]).reshape(1, LANES),
    }


def kernel(x_prompt, x_sample, state_conv_b, state_pool, state_conv_d, c_prompt, c_sample, w_mod, b_mod, norm1_g, norm2_g, w_in, a_ln_g, a_ln_b, a_ws, a_bs, conv_b_w, pool_w, pool_scale, conv_d_w, conv_d_b, d_ln_g, d_ln_b, w_branch, w_out, w_rg, b_rg, w_re, b_re, w_eg, w_eu, w_ed, final_g):
    depth = w_mod.shape[0]
    nb, seq, _ = x_prompt.shape
    ns = x_sample.shape[0]
    mod_all = _modulation(jnp.concatenate([c_prompt, c_sample], axis=0), w_mod, b_mod)
    fg = final_g.reshape(1, D_MODEL)

    xp = x_prompt
    xs = x_sample.reshape(ns, D_MODEL)
    outs = {k: [] for k in ("bp", "pp", "dp", "bs", "ps", "ds", "vs")}
    for li in range(depth):
        lw = _layer_weights(li, norm1_g, norm2_g, w_in, a_ln_g, a_ln_b, a_ws, a_bs, conv_b_w, pool_w, pool_scale,
                            conv_d_w, conv_d_b, d_ln_g, d_ln_b, w_branch, w_out, w_rg, b_rg, w_re, b_re)
        final = li == depth - 1
        mod_p = mod_all[li, :nb].reshape(nb, 1, 6 * D_MODEL)
        mod_s = mod_all[li, nb:]

        x1, h2, lg, sb, sp, sd = _prompt_mixer(xp, mod_p, lw)
        outs["bp"].append(sb)
        outs["pp"].append(sp)
        outs["dp"].append(sd)
        t = nb * seq
        xp = _moe(h2.reshape(t, D_MODEL), lg.reshape(t, LANES), x1.reshape(t, D_MODEL),
                  mod_p[:, :, 5 * D_MODEL:], fg, final, li, w_eg, w_eu, w_ed).reshape(nb, seq, D_MODEL)

        st_b, st_p, st_d = state_conv_b[li], state_pool[li], state_conv_d[li]
        x1, h2, lg, cx, pin, gl, v = _sample_mixer(xs, mod_s, st_b.reshape(ns, -1), st_p.reshape(ns, -1),
                                                   st_d.reshape(ns, -1), lw)
        outs["bs"].append(jnp.concatenate([st_b[:, 1:], cx[:, None]], axis=1))
        outs["ps"].append(jnp.concatenate([st_p[:, 1:], pin[:, None]], axis=1))
        outs["ds"].append(jnp.concatenate([st_d[:, 1:], gl[:, None]], axis=1))
        outs["vs"].append(v[:, None])
        xs = _moe(h2, lg, x1, mod_s[:, 5 * D_MODEL:], fg, final, li, w_eg, w_eu, w_ed)

    stack = lambda k: jnp.stack(outs[k])
    return (xp, xs.reshape(ns, 1, D_MODEL), stack("bp"), stack("pp"), stack("dp"),
            stack("bs"), stack("ps"), stack("ds"), stack("vs"))
```

```python
import functools

import jax
import jax.numpy as jnp
from jax import lax
from jax.experimental import pallas as pl
from jax.experimental.pallas import tpu as pltpu

F32 = jnp.float32
BF16 = jnp.bfloat16

D_MODEL = 1024
MIX_W = D_MODEL // 2
CHUNK = 128
A_GROUPS = 4
CONV_B = 3
POOL_WINDOWS = (2, 4, 8, 16)
POOL_CTX = 15
CONV_D = 31
N_BRANCH = 4
N_GROUPS_MOE = 4
EXPERTS_PER_GROUP = 8
N_EXPERTS = N_GROUPS_MOE * EXPERTS_PER_GROUP
D_EXPERT = D_MODEL // 4
EPS = 1e-6
D_IN = 8 * MIX_W + N_BRANCH * D_MODEL
PAST_LEN = 16384

LANES = 128
SUBLANES = 8
ROUTE_LANE0 = N_GROUPS_MOE

MIX_TM = 256
CONV_ROWS = 32
CTX_B = 8
CTX_P = 16
CTX_D = 32
MOE_BM = 128
ROUTE_TR = 512
COMB_TC = 256
DMA_UNROLL = 8
MIXER_VMEM_BYTES = 56 * 1024 * 1024


def _rms(x, g):
    return x * lax.rsqrt(jnp.mean(x * x, axis=-1, keepdims=True) + EPS) * g


def _ln(x, g, b):
    mu = jnp.mean(x, axis=-1, keepdims=True)
    xc = x - mu
    return xc * lax.rsqrt(jnp.mean(xc * xc, axis=-1, keepdims=True) + EPS) * g + b


def _bdot(a, b):
    return jnp.dot(a.astype(BF16), b.astype(BF16), preferred_element_type=F32)


def _mod_kernel(c_ref, w_ref, b_ref, o_ref):
    o_ref[0] = _bdot(c_ref[...], w_ref[0]) + b_ref[0]


def _modulation(c_all, w_mod, b_mod):
    depth = w_mod.shape[0]
    rows = c_all.shape[0]
    tn = 1536
    return pl.pallas_call(
        _mod_kernel,
        grid=(depth, 6 * D_MODEL // tn),
        in_specs=[pl.BlockSpec((rows, D_MODEL), lambda l, n: (0, 0)),
                  pl.BlockSpec((1, D_MODEL, tn), lambda l, n: (l, 0, n)),
                  pl.BlockSpec((1, 1, tn), lambda l, n: (l, 0, n))],
        out_specs=pl.BlockSpec((1, rows, tn), lambda l, n: (l, 0, n)),
        out_shape=jax.ShapeDtypeStruct((depth, rows, 6 * D_MODEL), F32),
        name="modulation",
    )(c_all, w_mod, b_mod.reshape(depth, 1, 6 * D_MODEL))


def _project(h, win_ref, proj_ref):
    hb = h.astype(BF16)
    for c in range(0, D_IN, D_MODEL):
        proj_ref[:, c:c + D_MODEL] = jnp.dot(hb, win_ref[:, c:c + D_MODEL], preferred_element_type=F32)


def _merge_tail(x, g1, sh2, sc2, n2g, proj_ref, ys_ref, wbr_ref, wout_ref, wr_ref, br_ref):
    m = None
    for i in range(N_BRANCH):
        lo = 8 * MIX_W + i * D_MODEL
        br = jnp.dot(ys_ref[i], wbr_ref[i], preferred_element_type=F32)
        t = jax.nn.sigmoid(proj_ref[:, lo:lo + D_MODEL]) * br
        m = t if m is None else m + t
    out = jnp.dot(m.astype(BF16), wout_ref[...], preferred_element_type=F32)
    x1 = x + g1 * out
    h2 = _rms(x1, n2g) * (1.0 + sc2) + sh2
    logits = _bdot(h2, wr_ref[...]) + br_ref[...]
    return x1, h2, logits


def _prompt_mixer_kernel(x_ref, mod_ref, n1g_ref, n2g_ref, win_ref, alng_ref, alnb_ref, aws_ref, absf_ref,
                         cbw_ref, poolw_ref, pscale_ref, cdw_ref, cdb_ref, dlng_ref, dlnb_ref,
                         wbr_ref, wout_ref, wr_ref, br_ref,
                         x1_ref, h2_ref, lg_ref, nb_ref, np_ref, nd_ref,
                         proj_ref, ys_ref, cx_ext, p_ext, gl_ext):
    tm = x_ref.shape[1]
    j = pl.program_id(1)

    @pl.when(j == 0)
    def _():
        cx_ext[0:CTX_B, :] = jnp.zeros((CTX_B, MIX_W), F32)
        p_ext[0:CTX_P, :] = jnp.zeros((CTX_P, MIX_W), F32)
        gl_ext[0:CTX_D, :] = jnp.zeros((CTX_D, MIX_W), F32)

    x = x_ref[0]
    mod = mod_ref[0]
    sh1, sc1, g1, sh2, sc2 = (mod[:, i * D_MODEL:(i + 1) * D_MODEL] for i in range(5))
    h = _rms(x, n1g_ref[...]) * (1.0 + sc1) + sh1
    _project(h, win_ref, proj_ref)

    u = jax.nn.gelu(proj_ref[:, 0:MIX_W])
    v = _ln(jax.nn.gelu(proj_ref[:, MIX_W:2 * MIX_W]), alng_ref[...], alnb_ref[...])
    vb = v.astype(BF16)
    rr = lax.broadcasted_iota(jnp.int32, (CHUNK, CHUNK), 0)
    cc = lax.broadcasted_iota(jnp.int32, (CHUNK, CHUNK), 1)
    wtril = [jnp.where(rr >= cc, aws_ref[g], 0.0).astype(BF16) for g in range(A_GROUPS)]
    cg = MIX_W // A_GROUPS
    for c0 in range(0, tm, CHUNK):
        mixed = jnp.concatenate(
            [jnp.dot(wtril[g], vb[c0:c0 + CHUNK, g * cg:(g + 1) * cg], preferred_element_type=F32)
             for g in range(A_GROUPS)], axis=1) + absf_ref[...]
        ys_ref[0, c0:c0 + CHUNK, :] = (u[c0:c0 + CHUNK] * mixed).astype(BF16)

    cx_ext[CTX_B:CTX_B + tm, :] = proj_ref[:, 3 * MIX_W:4 * MIX_W] * proj_ref[:, 4 * MIX_W:5 * MIX_W]
    for r0 in range(0, tm, CONV_ROWS):
        acc = None
        for k in range(CONV_B):
            t = cbw_ref[k:k + 1, :] * cx_ext[pl.ds(r0 + CTX_B - (CONV_B - 1) + k, CONV_ROWS), :]
            acc = t if acc is None else acc + t
        ys_ref[1, r0:r0 + CONV_ROWS, :] = (proj_ref[r0:r0 + CONV_ROWS, 2 * MIX_W:3 * MIX_W] * acc).astype(BF16)
    nb_ref[0] = cx_ext[pl.ds(CTX_B + tm - (CONV_B - 1), CONV_B - 1), :]
    cx_ext[0:CTX_B, :] = cx_ext[tm:tm + CTX_B, :]

    p_ext[CTX_P:CTX_P + tm, :] = proj_ref[:, 5 * MIX_W:6 * MIX_W]
    pos = j * tm + lax.broadcasted_iota(jnp.int32, (tm, 1), 0)
    pg = MIX_W // len(POOL_WINDOWS)
    mixed = []
    for gi, w in enumerate(POOL_WINDOWS):
        lo = gi * pg
        s = None
        for i in range(w):
            t = p_ext[pl.ds(CTX_P - i, tm), lo:lo + pg]
            s = t if s is None else s + t
        cnt = jnp.minimum(pos + 1, w).astype(F32)
        pooled = s / cnt - p_ext[CTX_P:CTX_P + tm, lo:lo + pg]
        mixed.append(_bdot(pooled, poolw_ref[gi]))
    ys_ref[2] = (jnp.concatenate(mixed, axis=1) * pscale_ref[...]).astype(BF16)
    np_ref[0] = p_ext[pl.ds(CTX_P + tm - POOL_CTX, POOL_CTX), :]
    p_ext[0:CTX_P, :] = p_ext[tm:tm + CTX_P, :]

    gl_ext[CTX_D:CTX_D + tm, :] = proj_ref[:, 6 * MIX_W:7 * MIX_W] * jax.nn.sigmoid(proj_ref[:, 7 * MIX_W:8 * MIX_W])
    for r0 in range(0, tm, CONV_ROWS):
        acc = None
        for k in range(CONV_D):
            t = cdw_ref[k:k + 1, :] * gl_ext[pl.ds(r0 + CTX_D - (CONV_D - 1) + k, CONV_ROWS), :]
            acc = t if acc is None else acc + t
        dc = acc + cdb_ref[...]
        ys_ref[3, r0:r0 + CONV_ROWS, :] = jax.nn.silu(_ln(dc, dlng_ref[...], dlnb_ref[...])).astype(BF16)
    nd_ref[0] = gl_ext[pl.ds(CTX_D + tm - (CONV_D - 1), CONV_D - 1), :]
    gl_ext[0:CTX_D, :] = gl_ext[tm:tm + CTX_D, :]

    x1, h2, logits = _merge_tail(x, g1, sh2, sc2, n2g_ref[...], proj_ref, ys_ref, wbr_ref, wout_ref, wr_ref, br_ref)
    x1_ref[0] = x1
    h2_ref[0] = h2
    lg_ref[0] = logits


def _const_spec(shape):
    nd = len(shape)
    return pl.BlockSpec(shape, lambda *_: (0,) * nd, pipeline_mode=pl.Buffered(1))


def _prompt_mixer(x, mod, lw):
    n, l, _ = x.shape
    tm = MIX_TM
    consts = [lw["n1g"], lw["n2g"], lw["w_in"], lw["a_ln_g"], lw["a_ln_b"], lw["a_ws"], lw["a_bs_full"],
              lw["conv_b_w"], lw["pool_w"], lw["pool_scale"], lw["conv_d_w"], lw["conv_d_b"], lw["d_ln_g"], lw["d_ln_b"],
              lw["w_branch"], lw["w_out"], lw["w_r"], lw["b_r"]]
    tok = lambda last: pl.BlockSpec((1, tm, last), lambda b, j: (b, j, 0))
    st = lambda rows: pl.BlockSpec((1, rows, MIX_W), lambda b, j: (b, 0, 0))
    return pl.pallas_call(
        _prompt_mixer_kernel,
        grid=(n, l // tm),
        in_specs=[tok(D_MODEL), pl.BlockSpec((1, 1, 6 * D_MODEL), lambda b, j: (b, 0, 0))]
                 + [_const_spec(c.shape) for c in consts],
        out_specs=[tok(D_MODEL), tok(D_MODEL), tok(LANES), st(CONV_B - 1), st(POOL_CTX), st(CONV_D - 1)],
        out_shape=[jax.ShapeDtypeStruct((n, l, D_MODEL), F32), jax.ShapeDtypeStruct((n, l, D_MODEL), F32),
                   jax.ShapeDtypeStruct((n, l, LANES), F32),
                   jax.ShapeDtypeStruct((n, CONV_B - 1, MIX_W), F32), jax.ShapeDtypeStruct((n, POOL_CTX, MIX_W), F32),
                   jax.ShapeDtypeStruct((n, CONV_D - 1, MIX_W), F32)],
        scratch_shapes=[pltpu.VMEM((tm, D_IN), F32), pltpu.VMEM((N_BRANCH, tm, MIX_W), BF16),
                        pltpu.VMEM((CTX_B + tm, MIX_W), F32), pltpu.VMEM((CTX_P + tm, MIX_W), F32),
                        pltpu.VMEM((CTX_D + tm, MIX_W), F32)],
        compiler_params=pltpu.CompilerParams(dimension_semantics=("arbitrary", "arbitrary"),
                                             vmem_limit_bytes=MIXER_VMEM_BYTES),
        name="prompt_mixer",
    )(x, mod, *consts)


def _sample_mixer_kernel(x_ref, mod_ref, stb_ref, stp_ref, std_ref,
                         n1g_ref, n2g_ref, win_ref, alng_ref, alnb_ref, aw0_ref, ab0_ref,
                         cbw_ref, poolw_ref, pscale_ref, cdw_ref, cdb_ref, dlng_ref, dlnb_ref,
                         wbr_ref, wout_ref, wr_ref, br_ref,
                         x1_ref, h2_ref, lg_ref, cx_ref, pin_ref, gl_ref, v_ref,
                         proj_ref, ys_ref):
    x = x_ref[...]
    mod = mod_ref[...]
    sh1, sc1, g1, sh2, sc2 = (mod[:, i * D_MODEL:(i + 1) * D_MODEL] for i in range(5))
    h = _rms(x, n1g_ref[...]) * (1.0 + sc1) + sh1
    _project(h, win_ref, proj_ref)

    u = jax.nn.gelu(proj_ref[:, 0:MIX_W])
    v = _ln(jax.nn.gelu(proj_ref[:, MIX_W:2 * MIX_W]), alng_ref[...], alnb_ref[...])
    v_ref[...] = v
    ys_ref[0] = (u * (aw0_ref[...] * v + ab0_ref[...])).astype(BF16)

    cx = proj_ref[:, 3 * MIX_W:4 * MIX_W] * proj_ref[:, 4 * MIX_W:5 * MIX_W]
    cx_ref[...] = cx
    acc = cbw_ref[CONV_B - 1:CONV_B, :] * cx
    for k in range(CONV_B - 1):
        acc = acc + cbw_ref[k:k + 1, :] * stb_ref[:, k * MIX_W:(k + 1) * MIX_W]
    ys_ref[1] = (proj_ref[:, 2 * MIX_W:3 * MIX_W] * acc).astype(BF16)

    pin = proj_ref[:, 5 * MIX_W:6 * MIX_W]
    pin_ref[...] = pin
    pg = MIX_W // len(POOL_WINDOWS)
    mixed = []
    for gi, w in enumerate(POOL_WINDOWS):
        lo = gi * pg
        s = pin[:, lo:lo + pg]
        for i in range(1, w):
            r = POOL_CTX - i
            s = s + stp_ref[:, r * MIX_W + lo:r * MIX_W + lo + pg]
        pooled = s / float(min(PAST_LEN + 1, w)) - pin[:, lo:lo + pg]
        mixed.append(_bdot(pooled, poolw_ref[gi]))
    ys_ref[2] = (jnp.concatenate(mixed, axis=1) * pscale_ref[...]).astype(BF16)

    gl = proj_ref[:, 6 * MIX_W:7 * MIX_W] * jax.nn.sigmoid(proj_ref[:, 7 * MIX_W:8 * MIX_W])
    gl_ref[...] = gl
    acc = cdw_ref[CONV_D - 1:CONV_D, :] * gl
    for k in range(CONV_D - 1):
        acc = acc + cdw_ref[k:k + 1, :] * std_ref[:, k * MIX_W:(k + 1) * MIX_W]
    dc = acc + cdb_ref[...]
    ys_ref[3] = jax.nn.silu(_ln(dc, dlng_ref[...], dlnb_ref[...])).astype(BF16)

    x1, h2, logits = _merge_tail(x, g1, sh2, sc2, n2g_ref[...], proj_ref, ys_ref, wbr_ref, wout_ref, wr_ref, br_ref)
    x1_ref[...] = x1
    h2_ref[...] = h2
    lg_ref[...] = logits


def _sample_mixer(x, mod, st_b, st_p, st_d, lw):
    rows = x.shape[0]
    ins = [x, mod, st_b, st_p, st_d,
           lw["n1g"], lw["n2g"], lw["w_in"], lw["a_ln_g"], lw["a_ln_b"], lw["a_w0"], lw["a_b0"],
           lw["conv_b_w"], lw["pool_w"], lw["pool_scale"], lw["conv_d_w"], lw["conv_d_b"], lw["d_ln_g"], lw["d_ln_b"],
           lw["w_branch"], lw["w_out"], lw["w_r"], lw["b_r"]]
    mk = lambda last: jax.ShapeDtypeStruct((rows, last), F32)
    return pl.pallas_call(
        _sample_mixer_kernel,
        out_shape=[mk(D_MODEL), mk(D_MODEL), mk(LANES), mk(MIX_W), mk(MIX_W), mk(MIX_W), mk(MIX_W)],
        scratch_shapes=[pltpu.VMEM((rows, D_IN), F32), pltpu.VMEM((N_BRANCH, rows, MIX_W), BF16)],
        compiler_params=pltpu.CompilerParams(vmem_limit_bytes=MIXER_VMEM_BYTES),
        name="sample_mixer",
    )(*ins)


def _route_kernel(lg_ref, route_ref, cnt_ref, carry_ref):
    tr = lg_ref.shape[0]
    i = pl.program_id(0)

    @pl.when(i == 0)
    def _():
        carry_ref[...] = jnp.zeros_like(carry_ref)

    lg = lg_ref[...]
    lane = lax.broadcasted_iota(jnp.int32, (tr, LANES), 1)
    lanef = lane.astype(F32)
    big = float(LANES)

    def first_argmax(vals):
        mx = jnp.max(vals, axis=-1, keepdims=True)
        return mx, jnp.min(jnp.where(vals == mx, lanef, big), axis=-1, keepdims=True)

    neg = -jnp.inf
    gl = jnp.where(lane < N_GROUPS_MOE, lg, neg)
    gmax, gsel = first_argmax(gl)
    p_g = 1.0 / jnp.sum(jnp.exp(gl - gmax), axis=-1, keepdims=True)
    e_lo = ROUTE_LANE0 + EXPERTS_PER_GROUP * gsel
    emask = jnp.logical_and(lanef >= e_lo, lanef < e_lo + EXPERTS_PER_GROUP)
    el = jnp.where(emask, lg, neg)
    emax = jnp.max(el, axis=-1, keepdims=True)
    ex = jnp.exp(el - emax)
    prob = jnp.where(emask, ex / jnp.sum(ex, axis=-1, keepdims=True), -1.0)
    p1, i1 = first_argmax(prob)
    p2, i2 = first_argmax(jnp.where(lanef == i1, -1.0, prob))
    w1 = p_g * p1 / (p1 + p2)
    w2 = p_g * p2 / (p1 + p2)

    hot1 = lanef == i1
    hot2 = lanef == i2
    hot = jnp.logical_or(hot1, hot2).astype(F32)
    rr = lax.broadcasted_iota(jnp.int32, (tr, tr), 0)
    cc = lax.broadcasted_iota(jnp.int32, (tr, tr), 1)
    before = jnp.dot((rr > cc).astype(BF16), hot.astype(BF16), preferred_element_type=F32) + carry_ref[...]
    rank1 = jnp.sum(jnp.where(hot1, before, 0.0), axis=-1, keepdims=True)
    rank2 = jnp.sum(jnp.where(hot2, before, 0.0), axis=-1, keepdims=True)
    carry_ref[...] = carry_ref[...] + jnp.sum(hot, axis=0, keepdims=True)
    cnt_ref[...] = carry_ref[...]

    fields = (i1 - ROUTE_LANE0, i2 - ROUTE_LANE0, rank1, rank2, w1, w2)
    out = jnp.zeros((tr, LANES), F32)
    for k, f in enumerate(fields):
        out = jnp.where(lane == k, f, out)
    route_ref[...] = out


def _route(logits):
    t = logits.shape[0]
    tr = next(c for c in range(ROUTE_TR, 0, -LANES) if t % c == 0)
    return pl.pallas_call(
        _route_kernel,
        grid=(t // tr,),
        in_specs=[pl.BlockSpec((tr, LANES), lambda i: (i, 0))],
        out_specs=[pl.BlockSpec((tr, LANES), lambda i: (i, 0)), pl.BlockSpec((1, LANES), lambda i: (0, 0))],
        out_shape=[jax.ShapeDtypeStruct((t, LANES), F32), jax.ShapeDtypeStruct((1, LANES), F32)],
        scratch_shapes=[pltpu.VMEM((1, LANES), F32)],
        compiler_params=pltpu.CompilerParams(dimension_semantics=("arbitrary",)),
        name="route",
    )(logits)


def _expert_kernel(be_ref, tok_ref, nu_ref, h2_hbm, wg_ref, wu_ref, wd_ref, ys_ref, buf, sem, wgb, wub, wdb):
    bm = buf.shape[1]
    b = pl.program_id(0)
    nu = nu_ref[0]
    slot = b % 2

    def issue(blk, s):
        def body(r, carry):
            tok = tok_ref[blk * bm + r]
            pltpu.make_async_copy(h2_hbm.at[pl.ds(tok, 1)], buf.at[s, pl.ds(r, 1)], sem.at[s]).start()
            return carry
        lax.fori_loop(0, bm, body, 0, unroll=DMA_UNROLL)

    @pl.when(b == 0)
    def _():
        issue(0, 0)

    @pl.when(b + 1 < nu)
    def _():
        issue(b + 1, 1 - slot)

    @pl.when(b >= nu)
    def _():
        ys_ref[...] = jnp.zeros_like(ys_ref)

    @pl.when(b < nu)
    def _():
        pltpu.make_async_copy(h2_hbm.at[pl.ds(0, bm)], buf.at[slot], sem.at[slot]).wait()

        @pl.when(jnp.logical_or(b == 0, be_ref[b] != be_ref[jnp.maximum(b - 1, 0)]))
        def _():
            wgb[...] = wg_ref[0, 0].astype(BF16)
            wub[...] = wu_ref[0, 0].astype(BF16)
            wdb[...] = wd_ref[0, 0].astype(BF16)

        xb = buf[slot].astype(BF16)
        g = jnp.dot(xb, wgb[...], preferred_element_type=F32)
        u = jnp.dot(xb, wub[...], preferred_element_type=F32)
        ys_ref[...] = jnp.dot((jax.nn.silu(g) * u).astype(BF16), wdb[...], preferred_element_type=F32)


def _experts(h2, slot_tok, block_e, n_used, li, w_eg, w_eu, w_ed):
    n_blocks = block_e.shape[0]
    bm = MOE_BM
    wspec = lambda shape: pl.BlockSpec((1, 1) + shape, lambda b, be, tok, nu: (li, be[b], 0, 0))
    return pl.pallas_call(
        _expert_kernel,
        grid_spec=pltpu.PrefetchScalarGridSpec(
            num_scalar_prefetch=3,
            grid=(n_blocks,),
            in_specs=[pl.BlockSpec(memory_space=pl.ANY), wspec((D_MODEL, D_EXPERT)), wspec((D_MODEL, D_EXPERT)),
                      wspec((D_EXPERT, D_MODEL))],
            out_specs=pl.BlockSpec((bm, D_MODEL), lambda b, be, tok, nu: (b, 0)),
            scratch_shapes=[pltpu.VMEM((2, bm, D_MODEL), F32), pltpu.SemaphoreType.DMA((2,)),
                            pltpu.VMEM((D_MODEL, D_EXPERT), BF16), pltpu.VMEM((D_MODEL, D_EXPERT), BF16),
                            pltpu.VMEM((D_EXPERT, D_MODEL), BF16)]),
        out_shape=jax.ShapeDtypeStruct((n_blocks * bm, D_MODEL), F32),
        compiler_params=pltpu.CompilerParams(dimension_semantics=("arbitrary",), disable_bounds_checks=True),
        name="experts",
    )(block_e, slot_tok, n_used, h2, w_eg, w_eu, w_ed)


def _combine_kernel(pos_ref, ys_hbm, x1_ref, route_ref, g2_ref, fg_ref, out_ref, buf, sem, *, final):
    tc = x1_ref.shape[0]
    i = pl.program_id(0)
    n = pl.num_programs(0)
    slot = i % 2

    def issue(tile, s):
        def body(r, carry):
            for k in range(2):
                p = pos_ref[(tile * tc + r) * 2 + k]
                pltpu.make_async_copy(ys_hbm.at[pl.ds(p, 1)], buf.at[s, k, pl.ds(r, 1)], sem.at[s]).start()
            return carry
        lax.fori_loop(0, tc, body, 0, unroll=DMA_UNROLL)

    @pl.when(i == 0)
    def _():
        issue(0, 0)

    @pl.when(i + 1 < n)
    def _():
        issue(i + 1, 1 - slot)

    for k in range(2):
        pltpu.make_async_copy(ys_hbm.at[pl.ds(0, tc)], buf.at[slot, k], sem.at[slot]).wait()
    route = route_ref[...]
    y = route[:, 4:5] * buf[slot, 0] + route[:, 5:6] * buf[slot, 1]
    x2 = x1_ref[...] + g2_ref[...].reshape(-1, D_MODEL) * y
    out_ref[...] = _rms(x2, fg_ref[...]) if final else x2


def _combine(ys, pos, x1, route, g2, fg, final):
    t = x1.shape[0]
    tc = min(t, COMB_TC)
    if g2.ndim == 3:
        seq = t // g2.shape[0]
        g2spec = pl.BlockSpec((1, 1, D_MODEL), lambda i, pos: (i * tc // seq, 0, 0))
    else:
        g2spec = pl.BlockSpec((tc, D_MODEL), lambda i, pos: (i, 0))
    tokspec = lambda last: pl.BlockSpec((tc, last), lambda i, pos: (i, 0))
    return pl.pallas_call(
        functools.partial(_combine_kernel, final=final),
        grid_spec=pltpu.PrefetchScalarGridSpec(
            num_scalar_prefetch=1,
            grid=(t // tc,),
            in_specs=[pl.BlockSpec(memory_space=pl.ANY), tokspec(D_MODEL), tokspec(LANES), g2spec,
                      pl.BlockSpec((1, D_MODEL), lambda i, pos: (0, 0))],
            out_specs=tokspec(D_MODEL),
            scratch_shapes=[pltpu.VMEM((2, 2, tc, D_MODEL), F32), pltpu.SemaphoreType.DMA((2,))]),
        out_shape=jax.ShapeDtypeStruct((t, D_MODEL), F32),
        compiler_params=pltpu.CompilerParams(dimension_semantics=("arbitrary",), disable_bounds_checks=True),
        name="combine",
    )(pos, ys, x1, route, g2, fg)


def _moe_experts(h2, logits, li, w_eg, w_eu, w_ed):
    t = h2.shape[0]
    bm = MOE_BM
    route, counts = _route(logits)
    expert = route[:, 0:2].astype(jnp.int32)
    rank = route[:, 2:4].astype(jnp.int32)
    cnt = counts[0, ROUTE_LANE0:ROUTE_LANE0 + N_EXPERTS].astype(jnp.int32)
    padded = (cnt + bm - 1) // bm * bm
    ends = jnp.cumsum(padded)
    pstarts = ends - padded
    dest = (pstarts[expert] + rank).reshape(-1)
    n_blocks = -(-(t * 2) // bm) + N_EXPERTS
    flat_tok = jnp.arange(t * 2, dtype=jnp.int32) // 2
    slot_tok = jnp.zeros((n_blocks * bm,), jnp.int32).at[dest].set(flat_tok)
    block_start = jnp.arange(n_blocks, dtype=jnp.int32) * bm
    block_e = jnp.minimum(jnp.sum(ends[None, :] <= block_start[:, None], axis=1), N_EXPERTS - 1).astype(jnp.int32)
    n_used = (ends[-1:] // bm).astype(jnp.int32)
    ys = _experts(h2, slot_tok, block_e, n_used, li, w_eg, w_eu, w_ed)
    return ys, dest.astype(jnp.int32), route


def _layer_weights(li, norm1_g, norm2_g, w_in, a_ln_g, a_ln_b, a_ws, a_bs, conv_b_w, pool_w, pool_scale,
                   conv_d_w, conv_d_b, d_ln_g, d_ln_b, w_branch, w_out, w_rg, b_rg, w_re, b_re):
    row = lambda a: a[li].reshape(1, -1)
    cg = MIX_W // A_GROUPS
    pad = LANES - N_GROUPS_MOE - N_EXPERTS
    return {
        "n1g": row(norm1_g), "n2g": row(norm2_g), "w_in": w_in[li].astype(BF16),
        "a_ln_g": row(a_ln_g), "a_ln_b": row(a_ln_b), "a_ws": a_ws[li],
        "a_bs_full": jnp.repeat(a_bs[li].T, cg, axis=1),
        "a_w0": jnp.repeat(a_ws[li, :, 0, 0], cg).reshape(1, MIX_W),
        "a_b0": jnp.repeat(a_bs[li, :, 0], cg).reshape(1, MIX_W),
        "conv_b_w": conv_b_w[li], "pool_w": pool_w[li], "pool_scale": row(pool_scale),
        "conv_d_w": conv_d_w[li], "conv_d_b": row(conv_d_b), "d_ln_g": row(d_ln_g), "d_ln_b": row(d_ln_b),
        "w_branch": w_branch[li].astype(BF16), "w_out": w_out[li].astype(BF16),
        "w_r": jnp.concatenate([w_rg[li], w_re[li], jnp.zeros((D_MODEL, pad), F32)], axis=1),
        "b_r": jnp.concatenate([b_rg[li], b_re[li], jnp.zeros((pad,), F32)]).reshape(1, LANES),
    }


def kernel(x_prompt, x_sample, state_conv_b, state_pool, state_conv_d, c_prompt, c_sample, w_mod, b_mod, norm1_g, norm2_g, w_in, a_ln_g, a_ln_b, a_ws, a_bs, conv_b_w, pool_w, pool_scale, conv_d_w, conv_d_b, d_ln_g, d_ln_b, w_branch, w_out, w_rg, b_rg, w_re, b_re, w_eg, w_eu, w_ed, final_g):
    depth = w_mod.shape[0]
    nb, seq, _ = x_prompt.shape
    ns = x_sample.shape[0]
    mod_all = _modulation(jnp.concatenate([c_prompt, c_sample], axis=0), w_mod, b_mod)
    fg = final_g.reshape(1, D_MODEL)

    xp = x_prompt
    xs = x_sample.reshape(ns, D_MODEL)
    outs = {k: [] for k in ("bp", "pp", "dp", "bs", "ps", "ds", "vs")}
    for li in range(depth):
        lw = _layer_weights(li, norm1_g, norm2_g, w_in, a_ln_g, a_ln_b, a_ws, a_bs, conv_b_w, pool_w, pool_scale,
                            conv_d_w, conv_d_b, d_ln_g, d_ln_b, w_branch, w_out, w_rg, b_rg, w_re, b_re)
        final = li == depth - 1
        mod_p = mod_all[li, :nb].reshape(nb, 1, 6 * D_MODEL)
        mod_s = mod_all[li, nb:]

        x1p, h2p, lgp, sb, sp, sd = _prompt_mixer(xp, mod_p, lw)
        outs["bp"].append(sb)
        outs["pp"].append(sp)
        outs["dp"].append(sd)

        st_b, st_p, st_d = state_conv_b[li], state_pool[li], state_conv_d[li]
        x1s, h2s, lgs, cx, pin, gl, v = _sample_mixer(xs, mod_s, st_b.reshape(ns, -1), st_p.reshape(ns, -1),
                                                      st_d.reshape(ns, -1), lw)
        outs["bs"].append(jnp.concatenate([st_b[:, 1:], cx[:, None]], axis=1))
        outs["ps"].append(jnp.concatenate([st_p[:, 1:], pin[:, None]], axis=1))
        outs["ds"].append(jnp.concatenate([st_d[:, 1:], gl[:, None]], axis=1))
        outs["vs"].append(v[:, None])

        t = nb * seq
        ys, dest, route = _moe_experts(jnp.concatenate([h2p.reshape(t, D_MODEL), h2s], axis=0),
                                       jnp.concatenate([lgp.reshape(t, LANES), lgs], axis=0), li, w_eg, w_eu, w_ed)
        xp = _combine(ys, dest[:2 * t], x1p.reshape(t, D_MODEL), route[:t], mod_p[:, :, 5 * D_MODEL:], fg,
                      final).reshape(nb, seq, D_MODEL)
        xs = _combine(ys, dest[2 * t:], x1s, route[t:], mod_s[:, 5 * D_MODEL:], fg, final)

    stack = lambda k: jnp.stack(outs[k])
    return (xp, xs.reshape(ns, 1, D_MODEL), stack("bp"), stack("pp"), stack("dp"),
            stack("bs"), stack("ps"), stack("ds"), stack("vs"))
```

```python
import functools

import jax
import jax.numpy as jnp
from jax import lax
from jax.experimental import pallas as pl
from jax.experimental.pallas import tpu as pltpu

F32 = jnp.float32
BF16 = jnp.bfloat16

D_MODEL = 1024
MIX_W = D_MODEL // 2
CHUNK = 128
A_GROUPS = 4
CONV_B = 3
POOL_WINDOWS = (2, 4, 8, 16)
POOL_CTX = 15
CONV_D = 31
N_BRANCH = 4
N_GROUPS_MOE = 4
EXPERTS_PER_GROUP = 8
N_EXPERTS = N_GROUPS_MOE * EXPERTS_PER_GROUP
D_EXPERT = D_MODEL // 4
EPS = 1e-6
D_MIX_IN = 8 * MIX_W
D_IN = D_MIX_IN + N_BRANCH * D_MODEL
PAST_LEN = 16384

LANES = 128
SUBLANES = 8
ROUTE_LANE0 = N_GROUPS_MOE

MIX_TM = 512
CONV_ROWS = 32
CTX_B = 8
CTX_P = 32
CTX_D = 32
MOE_BM = 128
ROUTE_TR = 512
COMB_TC = 256
DMA_UNROLL = 8
MIXER_VMEM_BYTES = 56 * 1024 * 1024


def _rms(x, g):
    return x * lax.rsqrt(jnp.mean(x * x, axis=-1, keepdims=True) + EPS) * g


def _ln(x, g, b):
    mu = jnp.mean(x, axis=-1, keepdims=True)
    xc = x - mu
    return xc * lax.rsqrt(jnp.mean(xc * xc, axis=-1, keepdims=True) + EPS) * g + b


def _bdot(a, b):
    return jnp.dot(a.astype(BF16), b.astype(BF16), preferred_element_type=F32)


def _mod_kernel(c_ref, w_ref, b_ref, o_ref):
    o_ref[0] = _bdot(c_ref[...], w_ref[0]) + b_ref[0]


def _modulation(c_all, w_mod, b_mod):
    depth = w_mod.shape[0]
    rows = c_all.shape[0]
    tn = 1536
    return pl.pallas_call(
        _mod_kernel,
        grid=(depth, 6 * D_MODEL // tn),
        in_specs=[pl.BlockSpec((rows, D_MODEL), lambda l, n: (0, 0)),
                  pl.BlockSpec((1, D_MODEL, tn), lambda l, n: (l, 0, n)),
                  pl.BlockSpec((1, 1, tn), lambda l, n: (l, 0, n))],
        out_specs=pl.BlockSpec((1, rows, tn), lambda l, n: (l, 0, n)),
        out_shape=jax.ShapeDtypeStruct((depth, rows, 6 * D_MODEL), F32),
        name="modulation",
    )(c_all, w_mod, b_mod.reshape(depth, 1, 6 * D_MODEL))


def _project(h, win_ref, hb_ref, proj_ref):
    hb_ref[...] = h.astype(BF16)
    for c in range(0, D_MIX_IN, D_MODEL):
        proj_ref[:, c:c + D_MODEL] = jnp.dot(hb_ref[...], win_ref[:, c:c + D_MODEL], preferred_element_type=F32)


def _merge_branch(i, hb_ref, win_ref, ys_ref, wbr_ref, m_ref):
    lo = D_MIX_IN + i * D_MODEL
    gates = jnp.dot(hb_ref[...], win_ref[:, lo:lo + D_MODEL], preferred_element_type=F32)
    br = jnp.dot(ys_ref[i], wbr_ref[i], preferred_element_type=F32)
    t = jax.nn.sigmoid(gates) * br
    m_ref[...] = t if i == 0 else m_ref[...] + t


def _merge_tail(x, g1, sh2, sc2, n2g, m_ref, wout_ref, wr_ref, br_ref):
    out = jnp.dot(m_ref[...].astype(BF16), wout_ref[...], preferred_element_type=F32)
    x1 = x + g1 * out
    h2 = _rms(x1, n2g) * (1.0 + sc2) + sh2
    logits = _bdot(h2, wr_ref[...]) + br_ref[...]
    return x1, h2, logits


def _tap(w8_ref, k, x):
    rows, c = x.shape
    return (x.reshape(rows // SUBLANES, SUBLANES, c) * w8_ref[k][None]).reshape(rows, c)


def _prompt_mixer_kernel(x_ref, mod_ref, n1g_ref, n2g_ref, win_ref, alng_ref, alnb_ref, aws_ref, absf_ref,
                         cbw_ref, poolw_ref, pscale_ref, cdw_ref, cdb_ref, dlng_ref, dlnb_ref,
                         wbr_ref, wout_ref, wr_ref, br_ref,
                         x1_ref, h2_ref, lg_ref, nb_ref, np_ref, nd_ref,
                         hb_ref, proj_ref, ys_ref, m_ref, cx_ext, p_ext, gl_ext, pool_tmp, conv_tmp):
    tm = x_ref.shape[1]
    j = pl.program_id(1)

    @pl.when(j == 0)
    def _():
        cx_ext[0:CTX_B, :] = jnp.zeros((CTX_B, MIX_W), F32)
        p_ext[0:CTX_P, :] = jnp.zeros((CTX_P, MIX_W), F32)
        gl_ext[0:CTX_D, :] = jnp.zeros((CTX_D, MIX_W), F32)

    x = x_ref[0]
    mod = mod_ref[0]
    sh1, sc1, g1, sh2, sc2 = (mod[:, i * D_MODEL:(i + 1) * D_MODEL] for i in range(5))
    h = _rms(x, n1g_ref[...]) * (1.0 + sc1) + sh1
    _project(h, win_ref, hb_ref, proj_ref)

    u = jax.nn.gelu(proj_ref[:, 0:MIX_W])
    v = _ln(jax.nn.gelu(proj_ref[:, MIX_W:2 * MIX_W]), alng_ref[...], alnb_ref[...])
    vb = v.astype(BF16)
    rr = lax.broadcasted_iota(jnp.int32, (CHUNK, CHUNK), 0)
    cc = lax.broadcasted_iota(jnp.int32, (CHUNK, CHUNK), 1)
    wtril = [jnp.where(rr >= cc, aws_ref[g], 0.0).astype(BF16) for g in range(A_GROUPS)]
    cg = MIX_W // A_GROUPS
    for c0 in range(0, tm, CHUNK):
        mixed = jnp.concatenate(
            [jnp.dot(wtril[g], vb[c0:c0 + CHUNK, g * cg:(g + 1) * cg], preferred_element_type=F32)
             for g in range(A_GROUPS)], axis=1) + absf_ref[...]
        ys_ref[0, c0:c0 + CHUNK, :] = (u[c0:c0 + CHUNK] * mixed).astype(BF16)
    _merge_branch(0, hb_ref, win_ref, ys_ref, wbr_ref, m_ref)

    cx_ext[CTX_B:CTX_B + tm, :] = proj_ref[:, 3 * MIX_W:4 * MIX_W] * proj_ref[:, 4 * MIX_W:5 * MIX_W]
    for r0 in range(0, tm, CONV_ROWS):
        acc = None
        for k in range(CONV_B):
            t = _tap(cbw_ref, k, cx_ext[pl.ds(r0 + CTX_B - (CONV_B - 1) + k, CONV_ROWS), :])
            acc = t if acc is None else acc + t
        ys_ref[1, r0:r0 + CONV_ROWS, :] = (proj_ref[r0:r0 + CONV_ROWS, 2 * MIX_W:3 * MIX_W] * acc).astype(BF16)
    nb_ref[0] = cx_ext[pl.ds(CTX_B + tm - (CONV_B - 1), CONV_B - 1), :]
    cx_ext[0:CTX_B, :] = cx_ext[tm:tm + CTX_B, :]
    _merge_branch(1, hb_ref, win_ref, ys_ref, wbr_ref, m_ref)

    p_ext[CTX_P:CTX_P + tm, :] = proj_ref[:, 5 * MIX_W:6 * MIX_W]
    pos = j * tm + lax.broadcasted_iota(jnp.int32, (tm, 1), 0)
    pg = MIX_W // len(POOL_WINDOWS)
    mixed = []
    for gi, w in enumerate(POOL_WINDOWS):
        lo = gi * pg
        levels = w.bit_length() - 1
        for lv in range(levels):
            span = 1 << lv
            start = CTX_P if lv == levels - 1 else SUBLANES * (lv + 1)
            rows = CTX_P + tm - start
            if lv == 0:
                a = p_ext[start:start + rows, lo:lo + pg] + p_ext[pl.ds(start - span, rows), lo:lo + pg]
            else:
                a = pool_tmp[gi, lv - 1, start:start + rows, :] + pool_tmp[gi, lv - 1, pl.ds(start - span, rows), :]
            if lv == levels - 1:
                s = a
            else:
                pool_tmp[gi, lv, start:start + rows, :] = a
        cnt = jnp.minimum(pos + 1, w).astype(F32)
        pooled = s / cnt - p_ext[CTX_P:CTX_P + tm, lo:lo + pg]
        mixed.append(_bdot(pooled, poolw_ref[gi]))
    ys_ref[2] = (jnp.concatenate(mixed, axis=1) * pscale_ref[...]).astype(BF16)
    np_ref[0] = p_ext[pl.ds(CTX_P + tm - POOL_CTX, POOL_CTX), :]
    p_ext[0:CTX_P, :] = p_ext[tm:tm + CTX_P, :]
    _merge_branch(2, hb_ref, win_ref, ys_ref, wbr_ref, m_ref)

    gl_ext[CTX_D:CTX_D + tm, :] = proj_ref[:, 6 * MIX_W:7 * MIX_W] * jax.nn.sigmoid(proj_ref[:, 7 * MIX_W:8 * MIX_W])
    off0 = CTX_D - (CONV_D - 1)
    for r0 in range(0, tm, CONV_ROWS):
        acc = None
        for r in range(SUBLANES):
            rows = CONV_ROWS if r == 0 else CONV_ROWS + SUBLANES
            y = None
            for q in range(-(-(off0 + CONV_D) // SUBLANES)):
                k = SUBLANES * q + r - off0
                if 0 <= k < CONV_D:
                    t = _tap(cdw_ref, k, gl_ext[r0 + SUBLANES * q:r0 + SUBLANES * q + rows, :])
                    y = t if y is None else y + t
            if r > 0:
                conv_tmp[r - 1] = y
                y = conv_tmp[r - 1, pl.ds(r, CONV_ROWS), :]
            acc = y if acc is None else acc + y
        dc = acc + cdb_ref[...]
        ys_ref[3, r0:r0 + CONV_ROWS, :] = jax.nn.silu(_ln(dc, dlng_ref[...], dlnb_ref[...])).astype(BF16)
    nd_ref[0] = gl_ext[pl.ds(CTX_D + tm - (CONV_D - 1), CONV_D - 1), :]
    gl_ext[0:CTX_D, :] = gl_ext[tm:tm + CTX_D, :]
    _merge_branch(3, hb_ref, win_ref, ys_ref, wbr_ref, m_ref)

    x1, h2, logits = _merge_tail(x, g1, sh2, sc2, n2g_ref[...], m_ref, wout_ref, wr_ref, br_ref)
    x1_ref[0] = x1
    h2_ref[0] = h2
    lg_ref[0] = logits


def _const_spec(shape):
    nd = len(shape)
    return pl.BlockSpec(shape, lambda *_: (0,) * nd, pipeline_mode=pl.Buffered(1))


def _prompt_mixer(x, mod, lw):
    n, l, _ = x.shape
    tm = MIX_TM
    consts = [lw["n1g"], lw["n2g"], lw["w_in"], lw["a_ln_g"], lw["a_ln_b"], lw["a_ws"], lw["a_bs_full"],
              lw["conv_b_w8"], lw["pool_w"], lw["pool_scale"], lw["conv_d_w8"], lw["conv_d_b"], lw["d_ln_g"], lw["d_ln_b"],
              lw["w_branch"], lw["w_out"], lw["w_r"], lw["b_r"]]
    tok = lambda last: pl.BlockSpec((1, tm, last), lambda b, j: (b, j, 0))
    st = lambda rows: pl.BlockSpec((1, rows, MIX_W), lambda b, j: (b, 0, 0))
    return pl.pallas_call(
        _prompt_mixer_kernel,
        grid=(n, l // tm),
        in_specs=[tok(D_MODEL), pl.BlockSpec((1, 1, 6 * D_MODEL), lambda b, j: (b, 0, 0))]
                 + [_const_spec(c.shape) for c in consts],
        out_specs=[tok(D_MODEL), tok(D_MODEL), tok(LANES), st(CONV_B - 1), st(POOL_CTX), st(CONV_D - 1)],
        out_shape=[jax.ShapeDtypeStruct((n, l, D_MODEL), F32), jax.ShapeDtypeStruct((n, l, D_MODEL), F32),
                   jax.ShapeDtypeStruct((n, l, LANES), F32),
                   jax.ShapeDtypeStruct((n, CONV_B - 1, MIX_W), F32), jax.ShapeDtypeStruct((n, POOL_CTX, MIX_W), F32),
                   jax.ShapeDtypeStruct((n, CONV_D - 1, MIX_W), F32)],
        scratch_shapes=[pltpu.VMEM((tm, D_MODEL), BF16), pltpu.VMEM((tm, D_MIX_IN), F32),
                        pltpu.VMEM((N_BRANCH, tm, MIX_W), BF16), pltpu.VMEM((tm, D_MODEL), F32),
                        pltpu.VMEM((CTX_B + tm, MIX_W), F32), pltpu.VMEM((CTX_P + tm, MIX_W), F32),
                        pltpu.VMEM((CTX_D + tm, MIX_W), F32),
                        pltpu.VMEM((len(POOL_WINDOWS), 3, CTX_P + tm, MIX_W // len(POOL_WINDOWS)), F32),
                        pltpu.VMEM((SUBLANES - 1, CONV_ROWS + SUBLANES, MIX_W), F32)],
        compiler_params=pltpu.CompilerParams(dimension_semantics=("arbitrary", "arbitrary"),
                                             vmem_limit_bytes=MIXER_VMEM_BYTES),
        name="prompt_mixer",
    )(x, mod, *consts)


def _sample_mixer_kernel(x_ref, mod_ref, stb_ref, stp_ref, std_ref,
                         n1g_ref, n2g_ref, win_ref, alng_ref, alnb_ref, aw0_ref, ab0_ref,
                         cbw_ref, poolw_ref, pscale_ref, cdw_ref, cdb_ref, dlng_ref, dlnb_ref,
                         wbr_ref, wout_ref, wr_ref, br_ref,
                         x1_ref, h2_ref, lg_ref, cx_ref, pin_ref, gl_ref, v_ref,
                         hb_ref, proj_ref, ys_ref, m_ref):
    x = x_ref[...]
    mod = mod_ref[...]
    sh1, sc1, g1, sh2, sc2 = (mod[:, i * D_MODEL:(i + 1) * D_MODEL] for i in range(5))
    h = _rms(x, n1g_ref[...]) * (1.0 + sc1) + sh1
    _project(h, win_ref, hb_ref, proj_ref)

    u = jax.nn.gelu(proj_ref[:, 0:MIX_W])
    v = _ln(jax.nn.gelu(proj_ref[:, MIX_W:2 * MIX_W]), alng_ref[...], alnb_ref[...])
    v_ref[...] = v
    ys_ref[0] = (u * (aw0_ref[...] * v + ab0_ref[...])).astype(BF16)

    cx = proj_ref[:, 3 * MIX_W:4 * MIX_W] * proj_ref[:, 4 * MIX_W:5 * MIX_W]
    cx_ref[...] = cx
    acc = cbw_ref[CONV_B - 1:CONV_B, :] * cx
    for k in range(CONV_B - 1):
        acc = acc + cbw_ref[k:k + 1, :] * stb_ref[:, k * MIX_W:(k + 1) * MIX_W]
    ys_ref[1] = (proj_ref[:, 2 * MIX_W:3 * MIX_W] * acc).astype(BF16)

    pin = proj_ref[:, 5 * MIX_W:6 * MIX_W]
    pin_ref[...] = pin
    pg = MIX_W // len(POOL_WINDOWS)
    mixed = []
    for gi, w in enumerate(POOL_WINDOWS):
        lo = gi * pg
        s = pin[:, lo:lo + pg]
        for i in range(1, w):
            r = POOL_CTX - i
            s = s + stp_ref[:, r * MIX_W + lo:r * MIX_W + lo + pg]
        pooled = s / float(min(PAST_LEN + 1, w)) - pin[:, lo:lo + pg]
        mixed.append(_bdot(pooled, poolw_ref[gi]))
    ys_ref[2] = (jnp.concatenate(mixed, axis=1) * pscale_ref[...]).astype(BF16)

    gl = proj_ref[:, 6 * MIX_W:7 * MIX_W] * jax.nn.sigmoid(proj_ref[:, 7 * MIX_W:8 * MIX_W])
    gl_ref[...] = gl
    acc = cdw_ref[CONV_D - 1:CONV_D, :] * gl
    for k in range(CONV_D - 1):
        acc = acc + cdw_ref[k:k + 1, :] * std_ref[:, k * MIX_W:(k + 1) * MIX_W]
    dc = acc + cdb_ref[...]
    ys_ref[3] = jax.nn.silu(_ln(dc, dlng_ref[...], dlnb_ref[...])).astype(BF16)

    for i in range(N_BRANCH):
        _merge_branch(i, hb_ref, win_ref, ys_ref, wbr_ref, m_ref)
    x1, h2, logits = _merge_tail(x, g1, sh2, sc2, n2g_ref[...], m_ref, wout_ref, wr_ref, br_ref)
    x1_ref[...] = x1
    h2_ref[...] = h2
    lg_ref[...] = logits


def _sample_mixer(x, mod, st_b, st_p, st_d, lw):
    rows = x.shape[0]
    ins = [x, mod, st_b, st_p, st_d,
           lw["n1g"], lw["n2g"], lw["w_in"], lw["a_ln_g"], lw["a_ln_b"], lw["a_w0"], lw["a_b0"],
           lw["conv_b_w"], lw["pool_w"], lw["pool_scale"], lw["conv_d_w"], lw["conv_d_b"], lw["d_ln_g"], lw["d_ln_b"],
           lw["w_branch"], lw["w_out"], lw["w_r"], lw["b_r"]]
    mk = lambda last: jax.ShapeDtypeStruct((rows, last), F32)
    return pl.pallas_call(
        _sample_mixer_kernel,
        out_shape=[mk(D_MODEL), mk(D_MODEL), mk(LANES), mk(MIX_W), mk(MIX_W), mk(MIX_W), mk(MIX_W)],
        scratch_shapes=[pltpu.VMEM((rows, D_MODEL), BF16), pltpu.VMEM((rows, D_MIX_IN), F32),
                        pltpu.VMEM((N_BRANCH, rows, MIX_W), BF16), pltpu.VMEM((rows, D_MODEL), F32)],
        compiler_params=pltpu.CompilerParams(vmem_limit_bytes=MIXER_VMEM_BYTES),
        name="sample_mixer",
    )(*ins)


def _route_kernel(lg_ref, route_ref, cnt_ref, carry_ref):
    tr = lg_ref.shape[0]
    i = pl.program_id(0)

    @pl.when(i == 0)
    def _():
        carry_ref[...] = jnp.zeros_like(carry_ref)

    lg = lg_ref[...]
    lane = lax.broadcasted_iota(jnp.int32, (tr, LANES), 1)
    lanef = lane.astype(F32)
    big = float(LANES)

    def first_argmax(vals):
        mx = jnp.max(vals, axis=-1, keepdims=True)
        return mx, jnp.min(jnp.where(vals == mx, lanef, big), axis=-1, keepdims=True)

    neg = -jnp.inf
    gl = jnp.where(lane < N_GROUPS_MOE, lg, neg)
    gmax, gsel = first_argmax(gl)
    p_g = 1.0 / jnp.sum(jnp.exp(gl - gmax), axis=-1, keepdims=True)
    e_lo = ROUTE_LANE0 + EXPERTS_PER_GROUP * gsel
    emask = jnp.logical_and(lanef >= e_lo, lanef < e_lo + EXPERTS_PER_GROUP)
    el = jnp.where(emask, lg, neg)
    emax = jnp.max(el, axis=-1, keepdims=True)
    ex = jnp.exp(el - emax)
    prob = jnp.where(emask, ex / jnp.sum(ex, axis=-1, keepdims=True), -1.0)
    p1, i1 = first_argmax(prob)
    p2, i2 = first_argmax(jnp.where(lanef == i1, -1.0, prob))
    w1 = p_g * p1 / (p1 + p2)
    w2 = p_g * p2 / (p1 + p2)

    hot1 = lanef == i1
    hot2 = lanef == i2
    hot = jnp.logical_or(hot1, hot2).astype(F32)
    rr = lax.broadcasted_iota(jnp.int32, (tr, tr), 0)
    cc = lax.broadcasted_iota(jnp.int32, (tr, tr), 1)
    before = jnp.dot((rr > cc).astype(BF16), hot.astype(BF16), preferred_element_type=F32) + carry_ref[...]
    rank1 = jnp.sum(jnp.where(hot1, before, 0.0), axis=-1, keepdims=True)
    rank2 = jnp.sum(jnp.where(hot2, before, 0.0), axis=-1, keepdims=True)
    carry_ref[...] = carry_ref[...] + jnp.sum(hot, axis=0, keepdims=True)
    cnt_ref[...] = carry_ref[...]

    fields = (i1 - ROUTE_LANE0, i2 - ROUTE_LANE0, rank1, rank2, w1, w2)
    out = jnp.zeros((tr, LANES), F32)
    for k, f in enumerate(fields):
        out = jnp.where(lane == k, f, out)
    route_ref[...] = out


def _route(logits):
    t = logits.shape[0]
    tr = next(c for c in range(ROUTE_TR, 0, -LANES) if t % c == 0)
    return pl.pallas_call(
        _route_kernel,
        grid=(t // tr,),
        in_specs=[pl.BlockSpec((tr, LANES), lambda i: (i, 0))],
        out_specs=[pl.BlockSpec((tr, LANES), lambda i: (i, 0)), pl.BlockSpec((1, LANES), lambda i: (0, 0))],
        out_shape=[jax.ShapeDtypeStruct((t, LANES), F32), jax.ShapeDtypeStruct((1, LANES), F32)],
        scratch_shapes=[pltpu.VMEM((1, LANES), F32)],
        compiler_params=pltpu.CompilerParams(dimension_semantics=("arbitrary",)),
        name="route",
    )(logits)


def _expert_kernel(be_ref, tok_ref, nu_ref, h2_hbm, wg_ref, wu_ref, wd_ref, ys_ref, buf, sem, wgb, wub, wdb):
    bm = buf.shape[1]
    b = pl.program_id(0)
    nu = nu_ref[0]
    slot = b % 2

    def issue(blk, s):
        def body(r, carry):
            tok = tok_ref[blk * bm + r]
            pltpu.make_async_copy(h2_hbm.at[pl.ds(tok, 1)], buf.at[s, pl.ds(r, 1)], sem.at[s]).start()
            return carry
        lax.fori_loop(0, bm, body, 0, unroll=DMA_UNROLL)

    @pl.when(b == 0)
    def _():
        issue(0, 0)

    @pl.when(b + 1 < nu)
    def _():
        issue(b + 1, 1 - slot)

    @pl.when(b >= nu)
    def _():
        ys_ref[...] = jnp.zeros_like(ys_ref)

    @pl.when(b < nu)
    def _():
        pltpu.make_async_copy(h2_hbm.at[pl.ds(0, bm)], buf.at[slot], sem.at[slot]).wait()

        @pl.when(jnp.logical_or(b == 0, be_ref[b] != be_ref[jnp.maximum(b - 1, 0)]))
        def _():
            wgb[...] = wg_ref[0, 0].astype(BF16)
            wub[...] = wu_ref[0, 0].astype(BF16)
            wdb[...] = wd_ref[0, 0].astype(BF16)

        xb = buf[slot].astype(BF16)
        g = jnp.dot(xb, wgb[...], preferred_element_type=F32)
        u = jnp.dot(xb, wub[...], preferred_element_type=F32)
        ys_ref[...] = jnp.dot((jax.nn.silu(g) * u).astype(BF16), wdb[...], preferred_element_type=F32)


def _experts(h2, slot_tok, block_e, n_used, li, w_eg, w_eu, w_ed):
    n_blocks = block_e.shape[0]
    bm = MOE_BM
    wspec = lambda shape: pl.BlockSpec((1, 1) + shape, lambda b, be, tok, nu: (li, be[b], 0, 0))
    return pl.pallas_call(
        _expert_kernel,
        grid_spec=pltpu.PrefetchScalarGridSpec(
            num_scalar_prefetch=3,
            grid=(n_blocks,),
            in_specs=[pl.BlockSpec(memory_space=pl.ANY), wspec((D_MODEL, D_EXPERT)), wspec((D_MODEL, D_EXPERT)),
                      wspec((D_EXPERT, D_MODEL))],
            out_specs=pl.BlockSpec((bm, D_MODEL), lambda b, be, tok, nu: (b, 0)),
            scratch_shapes=[pltpu.VMEM((2, bm, D_MODEL), F32), pltpu.SemaphoreType.DMA((2,)),
                            pltpu.VMEM((D_MODEL, D_EXPERT), BF16), pltpu.VMEM((D_MODEL, D_EXPERT), BF16),
                            pltpu.VMEM((D_EXPERT, D_MODEL), BF16)]),
        out_shape=jax.ShapeDtypeStruct((n_blocks * bm, D_MODEL), F32),
        compiler_params=pltpu.CompilerParams(dimension_semantics=("arbitrary",), disable_bounds_checks=True),
        name="experts",
    )(block_e, slot_tok, n_used, h2, w_eg, w_eu, w_ed)


def _combine_kernel(pos_ref, ys_hbm, x1_ref, route_ref, g2_ref, fg_ref, out_ref, buf, sem, *, final):
    tc = x1_ref.shape[0]
    i = pl.program_id(0)
    n = pl.num_programs(0)
    slot = i % 2

    def issue(tile, s):
        def body(r, carry):
            for k in range(2):
                p = pos_ref[(tile * tc + r) * 2 + k]
                pltpu.make_async_copy(ys_hbm.at[pl.ds(p, 1)], buf.at[s, k, pl.ds(r, 1)], sem.at[s]).start()
            return carry
        lax.fori_loop(0, tc, body, 0, unroll=DMA_UNROLL)

    @pl.when(i == 0)
    def _():
        issue(0, 0)

    @pl.when(i + 1 < n)
    def _():
        issue(i + 1, 1 - slot)

    for k in range(2):
        pltpu.make_async_copy(ys_hbm.at[pl.ds(0, tc)], buf.at[slot, k], sem.at[slot]).wait()
    route = route_ref[...]
    y = route[:, 4:5] * buf[slot, 0] + route[:, 5:6] * buf[slot, 1]
    x2 = x1_ref[...] + g2_ref[...].reshape(-1, D_MODEL) * y
    out_ref[...] = _rms(x2, fg_ref[...]) if final else x2


def _combine(ys, pos, x1, route, g2, fg, final):
    t = x1.shape[0]
    tc = min(t, COMB_TC)
    if g2.ndim == 3:
        seq = t // g2.shape[0]
        g2spec = pl.BlockSpec((1, 1, D_MODEL), lambda i, pos: (i * tc // seq, 0, 0))
    else:
        g2spec = pl.BlockSpec((tc, D_MODEL), lambda i, pos: (i, 0))
    tokspec = lambda last: pl.BlockSpec((tc, last), lambda i, pos: (i, 0))
    return pl.pallas_call(
        functools.partial(_combine_kernel, final=final),
        grid_spec=pltpu.PrefetchScalarGridSpec(
            num_scalar_prefetch=1,
            grid=(t // tc,),
            in_specs=[pl.BlockSpec(memory_space=pl.ANY), tokspec(D_MODEL), tokspec(LANES), g2spec,
                      pl.BlockSpec((1, D_MODEL), lambda i, pos: (0, 0))],
            out_specs=tokspec(D_MODEL),
            scratch_shapes=[pltpu.VMEM((2, 2, tc, D_MODEL), F32), pltpu.SemaphoreType.DMA((2,))]),
        out_shape=jax.ShapeDtypeStruct((t, D_MODEL), F32),
        compiler_params=pltpu.CompilerParams(dimension_semantics=("arbitrary",), disable_bounds_checks=True),
        name="combine",
    )(pos, ys, x1, route, g2, fg)


def _moe_experts(h2, logits, li, w_eg, w_eu, w_ed):
    t = h2.shape[0]
    bm = MOE_BM
    route, counts = _route(logits)
    expert = route[:, 0:2].astype(jnp.int32)
    rank = route[:, 2:4].astype(jnp.int32)
    cnt = counts[0, ROUTE_LANE0:ROUTE_LANE0 + N_EXPERTS].astype(jnp.int32)
    padded = (cnt + bm - 1) // bm * bm
    ends = jnp.cumsum(padded)
    pstarts = ends - padded
    dest = (pstarts[expert] + rank).reshape(-1)
    n_blocks = -(-(t * 2) // bm) + N_EXPERTS
    flat_tok = jnp.arange(t * 2, dtype=jnp.int32) // 2
    slot_tok = jnp.zeros((n_blocks * bm,), jnp.int32).at[dest].set(flat_tok)
    block_start = jnp.arange(n_blocks, dtype=jnp.int32) * bm
    block_e = jnp.minimum(jnp.sum(ends[None, :] <= block_start[:, None], axis=1), N_EXPERTS - 1).astype(jnp.int32)
    n_used = (ends[-1:] // bm).astype(jnp.int32)
    ys = _experts(h2, slot_tok, block_e, n_used, li, w_eg, w_eu, w_ed)
    return ys, dest.astype(jnp.int32), route


def _layer_weights(li, norm1_g, norm2_g, w_in, a_ln_g, a_ln_b, a_ws, a_bs, conv_b_w, pool_w, pool_scale,
                   conv_d_w, conv_d_b, d_ln_g, d_ln_b, w_branch, w_out, w_rg, b_rg, w_re, b_re):
    row = lambda a: a[li].reshape(1, -1)
    cg = MIX_W // A_GROUPS
    pad = LANES - N_GROUPS_MOE - N_EXPERTS
    return {
        "n1g": row(norm1_g), "n2g": row(norm2_g), "w_in": w_in[li].astype(BF16),
        "a_ln_g": row(a_ln_g), "a_ln_b": row(a_ln_b), "a_ws": a_ws[li],
        "a_bs_full": jnp.repeat(a_bs[li].T, cg, axis=1),
        "a_w0": jnp.repeat(a_ws[li, :, 0, 0], cg).reshape(1, MIX_W),
        "a_b0": jnp.repeat(a_bs[li, :, 0], cg).reshape(1, MIX_W),
        "conv_b_w": conv_b_w[li], "pool_w": pool_w[li], "pool_scale": row(pool_scale),
        "conv_b_w8": jnp.broadcast_to(conv_b_w[li][:, None, :], (CONV_B, SUBLANES, MIX_W)),
        "conv_d_w8": jnp.broadcast_to(conv_d_w[li][:, None, :], (CONV_D, SUBLANES, MIX_W)),
        "conv_d_w": conv_d_w[li], "conv_d_b": row(conv_d_b), "d_ln_g": row(d_ln_g), "d_ln_b": row(d_ln_b),
        "w_branch": w_branch[li].astype(BF16), "w_out": w_out[li].astype(BF16),
        "w_r": jnp.concatenate([w_rg[li], w_re[li], jnp.zeros((D_MODEL, pad), F32)], axis=1),
        "b_r": jnp.concatenate([b_rg[li], b_re[li], jnp.zeros((pad,), F32)]).reshape(1, LANES),
    }


def kernel(x_prompt, x_sample, state_conv_b, state_pool, state_conv_d, c_prompt, c_sample, w_mod, b_mod, norm1_g, norm2_g, w_in, a_ln_g, a_ln_b, a_ws, a_bs, conv_b_w, pool_w, pool_scale, conv_d_w, conv_d_b, d_ln_g, d_ln_b, w_branch, w_out, w_rg, b_rg, w_re, b_re, w_eg, w_eu, w_ed, final_g):
    depth = w_mod.shape[0]
    nb, seq, _ = x_prompt.shape
    ns = x_sample.shape[0]
    mod_all = _modulation(jnp.concatenate([c_prompt, c_sample], axis=0), w_mod, b_mod)
    fg = final_g.reshape(1, D_MODEL)

    xp = x_prompt
    xs = x_sample.reshape(ns, D_MODEL)
    outs = {k: [] for k in ("bp", "pp", "dp", "bs", "ps", "ds", "vs")}
    for li in range(depth):
        lw = _layer_weights(li, norm1_g, norm2_g, w_in, a_ln_g, a_ln_b, a_ws, a_bs, conv_b_w, pool_w, pool_scale,
                            conv_d_w, conv_d_b, d_ln_g, d_ln_b, w_branch, w_out, w_rg, b_rg, w_re, b_re)
        final = li == depth - 1
        mod_p = mod_all[li, :nb].reshape(nb, 1, 6 * D_MODEL)
        mod_s = mod_all[li, nb:]

        x1p, h2p, lgp, sb, sp, sd = _prompt_mixer(xp, mod_p, lw)
        outs["bp"].append(sb)
        outs["pp"].append(sp)
        outs["dp"].append(sd)

        st_b, st_p, st_d = state_conv_b[li], state_pool[li], state_conv_d[li]
        x1s, h2s, lgs, cx, pin, gl, v = _sample_mixer(xs, mod_s, st_b.reshape(ns, -1), st_p.reshape(ns, -1),
                                                      st_d.reshape(ns, -1), lw)
        outs["bs"].append(jnp.concatenate([st_b[:, 1:], cx[:, None]], axis=1))
        outs["ps"].append(jnp.concatenate([st_p[:, 1:], pin[:, None]], axis=1))
        outs["ds"].append(jnp.concatenate([st_d[:, 1:], gl[:, None]], axis=1))
        outs["vs"].append(v[:, None])

        t = nb * seq
        ys, dest, route = _moe_experts(jnp.concatenate([h2p.reshape(t, D_MODEL), h2s], axis=0),
                                       jnp.concatenate([lgp.reshape(t, LANES), lgs], axis=0), li, w_eg, w_eu, w_ed)
        xp = _combine(ys, dest[:2 * t], x1p.reshape(t, D_MODEL), route[:t], mod_p[:, :, 5 * D_MODEL:], fg,
                      final).reshape(nb, seq, D_MODEL)
        xs = _combine(ys, dest[2 * t:], x1s, route[t:], mod_s[:, 5 * D_MODEL:], fg, final)

    stack = lambda k: jnp.stack(outs[k])
    return (xp, xs.reshape(ns, 1, D_MODEL), stack("bp"), stack("pp"), stack("dp"),
            stack("bs"), stack("ps"), stack("ds"), stack("vs"))
```

```python
import functools

import jax
import jax.numpy as jnp
from jax import lax
from jax.experimental import pallas as pl
from jax.experimental.pallas import tpu as pltpu

F32 = jnp.float32
BF16 = jnp.bfloat16

D_MODEL = 1024
MIX_W = D_MODEL // 2
CHUNK = 128
A_GROUPS = 4
CONV_B = 3
POOL_WINDOWS = (2, 4, 8, 16)
POOL_CTX = 15
CONV_D = 31
N_BRANCH = 4
N_GROUPS_MOE = 4
EXPERTS_PER_GROUP = 8
N_EXPERTS = N_GROUPS_MOE * EXPERTS_PER_GROUP
D_EXPERT = D_MODEL // 4
EPS = 1e-6
D_MIX_IN = 8 * MIX_W
D_IN = D_MIX_IN + N_BRANCH * D_MODEL
PAST_LEN = 16384

LANES = 128
SUBLANES = 8
ROUTE_LANE0 = N_GROUPS_MOE

MIX_TM = 512
CONV_ROWS = 32
TAIL_ROWS = 256
CTX_B = 8
CTX_P = 32
CTX_D = 32
MOE_BM = 128
ROUTE_TR = 512
COMB_TC = 256
DMA_UNROLL = 8
MIXER_VMEM_BYTES = 56 * 1024 * 1024


def _rms(x, g):
    return x * lax.rsqrt(jnp.mean(x * x, axis=-1, keepdims=True) + EPS) * g


def _ln(x, g, b):
    mu = jnp.mean(x, axis=-1, keepdims=True)
    xc = x - mu
    return xc * lax.rsqrt(jnp.mean(xc * xc, axis=-1, keepdims=True) + EPS) * g + b


def _conv_in(x):
    return x.astype(BF16).astype(F32)


def _bdot(a, b):
    return jnp.dot(a.astype(BF16), b.astype(BF16), preferred_element_type=F32)


def _mod_kernel(c_ref, w_ref, b_ref, o_ref):
    o_ref[0] = _bdot(c_ref[...], w_ref[0]) + b_ref[0]


def _modulation(c_all, w_mod, b_mod):
    depth = w_mod.shape[0]
    rows = c_all.shape[0]
    tn = 1536
    return pl.pallas_call(
        _mod_kernel,
        grid=(depth, 6 * D_MODEL // tn),
        in_specs=[pl.BlockSpec((rows, D_MODEL), lambda l, n: (0, 0)),
                  pl.BlockSpec((1, D_MODEL, tn), lambda l, n: (l, 0, n)),
                  pl.BlockSpec((1, 1, tn), lambda l, n: (l, 0, n))],
        out_specs=pl.BlockSpec((1, rows, tn), lambda l, n: (l, 0, n)),
        out_shape=jax.ShapeDtypeStruct((depth, rows, 6 * D_MODEL), F32),
        name="modulation",
    )(c_all, w_mod, b_mod.reshape(depth, 1, 6 * D_MODEL))


def _project(h, win_ref, hb_ref, proj_ref):
    hb_ref[...] = h.astype(BF16)
    for c in range(0, D_MIX_IN, D_MODEL):
        proj_ref[:, c:c + D_MODEL] = jnp.dot(hb_ref[...], win_ref[:, c:c + D_MODEL], preferred_element_type=F32)


def _merge_branch(i, hb_ref, win_ref, ys_ref, wbr_ref, m_ref):
    lo = D_MIX_IN + i * D_MODEL
    rows = hb_ref.shape[0]
    for r0 in range(0, rows, TAIL_ROWS):
        rs = slice(r0, min(r0 + TAIL_ROWS, rows))
        gates = jnp.dot(hb_ref[rs, :], win_ref[:, lo:lo + D_MODEL], preferred_element_type=F32)
        br = jnp.dot(ys_ref[i, rs, :], wbr_ref[i], preferred_element_type=F32)
        t = jax.nn.sigmoid(gates) * br
        m_ref[rs, :] = t if i == 0 else m_ref[rs, :] + t


def _merge_tail(x_ref, g1, sh2, sc2, n2g, m_ref, wout_ref, wr_ref, br_ref, x1_ref, h2_ref, lg_ref):
    rows = m_ref.shape[0]
    for r0 in range(0, rows, TAIL_ROWS):
        rs = slice(r0, min(r0 + TAIL_ROWS, rows))
        per_row = lambda a: a if a.shape[0] == 1 else a[rs]
        out = jnp.dot(m_ref[rs, :].astype(BF16), wout_ref[...], preferred_element_type=F32)
        x1 = x_ref[rs, :] + per_row(g1) * out
        h2 = _rms(x1, n2g) * (1.0 + per_row(sc2)) + per_row(sh2)
        x1_ref[rs, :] = x1
        h2_ref[rs, :] = h2
        lg_ref[rs, :] = _bdot(h2, wr_ref[...]) + br_ref[...]


def _tap(w8_ref, k, x):
    rows, c = x.shape
    return (x.reshape(rows // SUBLANES, SUBLANES, c) * w8_ref[k][None]).reshape(rows, c)


def _prompt_mixer_kernel(x_ref, mod_ref, n1g_ref, n2g_ref, win_ref, alng_ref, alnb_ref, aws_ref, absf_ref,
                         cbw_ref, poolw_ref, pscale_ref, cdw_ref, cdb_ref, dlng_ref, dlnb_ref,
                         wbr_ref, wout_ref, wr_ref, br_ref,
                         x1_ref, h2_ref, lg_ref, nb_ref, np_ref, nd_ref,
                         hb_ref, proj_ref, ys_ref, cx_ext, p_ext, gl_ext, pool_tmp, conv_tmp):
    tm = x_ref.shape[1]
    m_ref = x1_ref.at[0]
    j = pl.program_id(1)

    @pl.when(j == 0)
    def _():
        cx_ext[0:CTX_B, :] = jnp.zeros((CTX_B, MIX_W), F32)
        p_ext[0:CTX_P, :] = jnp.zeros((CTX_P, MIX_W), F32)
        gl_ext[0:CTX_D, :] = jnp.zeros((CTX_D, MIX_W), F32)

    x = x_ref[0]
    mod = mod_ref[0]
    sh1, sc1, g1, sh2, sc2 = (mod[:, i * D_MODEL:(i + 1) * D_MODEL] for i in range(5))
    h = _rms(x, n1g_ref[...]) * (1.0 + sc1) + sh1
    _project(h, win_ref, hb_ref, proj_ref)

    u = jax.nn.gelu(proj_ref[:, 0:MIX_W])
    v = _ln(jax.nn.gelu(proj_ref[:, MIX_W:2 * MIX_W]), alng_ref[...], alnb_ref[...])
    vb = v.astype(BF16)
    rr = lax.broadcasted_iota(jnp.int32, (CHUNK, CHUNK), 0)
    cc = lax.broadcasted_iota(jnp.int32, (CHUNK, CHUNK), 1)
    wtril = [jnp.where(rr >= cc, aws_ref[g], 0.0).astype(BF16) for g in range(A_GROUPS)]
    cg = MIX_W // A_GROUPS
    for c0 in range(0, tm, CHUNK):
        mixed = jnp.concatenate(
            [jnp.dot(wtril[g], vb[c0:c0 + CHUNK, g * cg:(g + 1) * cg], preferred_element_type=F32)
             for g in range(A_GROUPS)], axis=1) + absf_ref[...]
        ys_ref[0, c0:c0 + CHUNK, :] = (u[c0:c0 + CHUNK] * mixed).astype(BF16)
    _merge_branch(0, hb_ref, win_ref, ys_ref, wbr_ref, m_ref)

    for c0 in range(0, tm, CHUNK):
        cx_ext[CTX_B + c0:CTX_B + c0 + CHUNK, :] = _conv_in(
            proj_ref[c0:c0 + CHUNK, 3 * MIX_W:4 * MIX_W] * proj_ref[c0:c0 + CHUNK, 4 * MIX_W:5 * MIX_W])
    for r0 in range(0, tm, CONV_ROWS):
        acc = None
        for k in range(CONV_B):
            t = _tap(cbw_ref, k, cx_ext[pl.ds(r0 + CTX_B - (CONV_B - 1) + k, CONV_ROWS), :])
            acc = t if acc is None else acc + t
        ys_ref[1, r0:r0 + CONV_ROWS, :] = (proj_ref[r0:r0 + CONV_ROWS, 2 * MIX_W:3 * MIX_W] * acc).astype(BF16)
    tail = pl.ds(tm - (CONV_B - 1), CONV_B - 1)
    nb_ref[0] = proj_ref[tail, 3 * MIX_W:4 * MIX_W] * proj_ref[tail, 4 * MIX_W:5 * MIX_W]
    cx_ext[0:CTX_B, :] = cx_ext[tm:tm + CTX_B, :]
    _merge_branch(1, hb_ref, win_ref, ys_ref, wbr_ref, m_ref)

    p_ext[CTX_P:CTX_P + tm, :] = proj_ref[:, 5 * MIX_W:6 * MIX_W]
    pos = j * tm + lax.broadcasted_iota(jnp.int32, (tm, 1), 0)
    pg = MIX_W // len(POOL_WINDOWS)
    mixed = []
    for gi, w in enumerate(POOL_WINDOWS):
        lo = gi * pg
        levels = w.bit_length() - 1
        for lv in range(levels):
            span = 1 << lv
            start = CTX_P if lv == levels - 1 else SUBLANES * (lv + 1)
            rows = CTX_P + tm - start
            if lv == 0:
                a = p_ext[start:start + rows, lo:lo + pg] + p_ext[pl.ds(start - span, rows), lo:lo + pg]
            else:
                prev = pool_tmp.at[gi * (gi - 1) // 2 + lv - 1]
                a = prev[start:start + rows, :] + prev[pl.ds(start - span, rows), :]
            if lv == levels - 1:
                s = a
            else:
                pool_tmp[gi * (gi - 1) // 2 + lv, start:start + rows, :] = a
        cnt = jnp.minimum(pos + 1, w).astype(F32)
        pooled = s / cnt - p_ext[CTX_P:CTX_P + tm, lo:lo + pg]
        mixed.append(_bdot(pooled, poolw_ref[gi]))
    ys_ref[2] = (jnp.concatenate(mixed, axis=1) * pscale_ref[...]).astype(BF16)
    np_ref[0] = p_ext[pl.ds(CTX_P + tm - POOL_CTX, POOL_CTX), :]
    p_ext[0:CTX_P, :] = p_ext[tm:tm + CTX_P, :]
    _merge_branch(2, hb_ref, win_ref, ys_ref, wbr_ref, m_ref)

    for c0 in range(0, tm, CHUNK):
        gl_ext[CTX_D + c0:CTX_D + c0 + CHUNK, :] = _conv_in(
            proj_ref[c0:c0 + CHUNK, 6 * MIX_W:7 * MIX_W] * jax.nn.sigmoid(proj_ref[c0:c0 + CHUNK, 7 * MIX_W:8 * MIX_W]))
    off0 = CTX_D - (CONV_D - 1)
    for r0 in range(0, tm, CONV_ROWS):
        acc = None
        for r in range(SUBLANES):
            rows = CONV_ROWS if r == 0 else CONV_ROWS + SUBLANES
            y = None
            for q in range(-(-(off0 + CONV_D) // SUBLANES)):
                k = SUBLANES * q + r - off0
                if 0 <= k < CONV_D:
                    t = _tap(cdw_ref, k, gl_ext[r0 + SUBLANES * q:r0 + SUBLANES * q + rows, :])
                    y = t if y is None else y + t
            if r > 0:
                conv_tmp[r - 1] = y
                y = conv_tmp[r - 1, pl.ds(r, CONV_ROWS), :]
            acc = y if acc is None else acc + y
        dc = acc + cdb_ref[...]
        ys_ref[3, r0:r0 + CONV_ROWS, :] = jax.nn.silu(_ln(dc, dlng_ref[...], dlnb_ref[...])).astype(BF16)
    tail = pl.ds(tm - (CONV_D - 1), CONV_D - 1)
    nd_ref[0] = proj_ref[tail, 6 * MIX_W:7 * MIX_W] * jax.nn.sigmoid(proj_ref[tail, 7 * MIX_W:8 * MIX_W])
    gl_ext[0:CTX_D, :] = gl_ext[tm:tm + CTX_D, :]
    _merge_branch(3, hb_ref, win_ref, ys_ref, wbr_ref, m_ref)

    _merge_tail(x_ref.at[0], g1, sh2, sc2, n2g_ref[...], m_ref, wout_ref, wr_ref, br_ref,
                x1_ref.at[0], h2_ref.at[0], lg_ref.at[0])


def _const_spec(shape):
    nd = len(shape)
    return pl.BlockSpec(shape, lambda *_: (0,) * nd, pipeline_mode=pl.Buffered(1))


def _prompt_mixer(x, mod, lw):
    n, l, _ = x.shape
    tm = MIX_TM
    consts = [lw["n1g"], lw["n2g"], lw["w_in"], lw["a_ln_g"], lw["a_ln_b"], lw["a_ws"], lw["a_bs_full"],
              lw["conv_b_w8"], lw["pool_w"], lw["pool_scale"], lw["conv_d_w8"], lw["conv_d_b"], lw["d_ln_g"], lw["d_ln_b"],
              lw["w_branch"], lw["w_out"], lw["w_r"], lw["b_r"]]
    tok = lambda last: pl.BlockSpec((1, tm, last), lambda b, j: (b, j, 0))
    st = lambda rows: pl.BlockSpec((1, rows, MIX_W), lambda b, j: (b, 0, 0))
    return pl.pallas_call(
        _prompt_mixer_kernel,
        grid=(n, l // tm),
        in_specs=[tok(D_MODEL), pl.BlockSpec((1, 1, 6 * D_MODEL), lambda b, j: (b, 0, 0))]
                 + [_const_spec(c.shape) for c in consts],
        out_specs=[tok(D_MODEL), tok(D_MODEL), tok(LANES), st(CONV_B - 1), st(POOL_CTX), st(CONV_D - 1)],
        out_shape=[jax.ShapeDtypeStruct((n, l, D_MODEL), F32), jax.ShapeDtypeStruct((n, l, D_MODEL), F32),
                   jax.ShapeDtypeStruct((n, l, LANES), F32),
                   jax.ShapeDtypeStruct((n, CONV_B - 1, MIX_W), F32), jax.ShapeDtypeStruct((n, POOL_CTX, MIX_W), F32),
                   jax.ShapeDtypeStruct((n, CONV_D - 1, MIX_W), F32)],
        scratch_shapes=[pltpu.VMEM((tm, D_MODEL), BF16), pltpu.VMEM((tm, D_MIX_IN), F32),
                        pltpu.VMEM((N_BRANCH, tm, MIX_W), BF16),
                        pltpu.VMEM((CTX_B + tm, MIX_W), F32), pltpu.VMEM((CTX_P + tm, MIX_W), F32),
                        pltpu.VMEM((CTX_D + tm, MIX_W), F32),
                        pltpu.VMEM((6, CTX_P + tm, MIX_W // len(POOL_WINDOWS)), F32),
                        pltpu.VMEM((SUBLANES - 1, CONV_ROWS + SUBLANES, MIX_W), F32)],
        compiler_params=pltpu.CompilerParams(dimension_semantics=("arbitrary", "arbitrary"),
                                             vmem_limit_bytes=MIXER_VMEM_BYTES),
        name="prompt_mixer",
    )(x, mod, *consts)


def _sample_mixer_kernel(x_ref, mod_ref, stb_ref, stp_ref, std_ref,
                         n1g_ref, n2g_ref, win_ref, alng_ref, alnb_ref, aw0_ref, ab0_ref,
                         cbw_ref, poolw_ref, pscale_ref, cdw_ref, cdb_ref, dlng_ref, dlnb_ref,
                         wbr_ref, wout_ref, wr_ref, br_ref,
                         x1_ref, h2_ref, lg_ref, cx_ref, pin_ref, gl_ref, v_ref,
                         hb_ref, proj_ref, ys_ref, m_ref):
    x = x_ref[...]
    mod = mod_ref[...]
    sh1, sc1, g1, sh2, sc2 = (mod[:, i * D_MODEL:(i + 1) * D_MODEL] for i in range(5))
    h = _rms(x, n1g_ref[...]) * (1.0 + sc1) + sh1
    _project(h, win_ref, hb_ref, proj_ref)

    u = jax.nn.gelu(proj_ref[:, 0:MIX_W])
    v = _ln(jax.nn.gelu(proj_ref[:, MIX_W:2 * MIX_W]), alng_ref[...], alnb_ref[...])
    v_ref[...] = v
    ys_ref[0] = (u * (aw0_ref[...] * v + ab0_ref[...])).astype(BF16)

    cx = proj_ref[:, 3 * MIX_W:4 * MIX_W] * proj_ref[:, 4 * MIX_W:5 * MIX_W]
    cx_ref[...] = cx
    acc = cbw_ref[CONV_B - 1:CONV_B, :] * _conv_in(cx)
    for k in range(CONV_B - 1):
        acc = acc + cbw_ref[k:k + 1, :] * _conv_in(stb_ref[:, k * MIX_W:(k + 1) * MIX_W])
    ys_ref[1] = (proj_ref[:, 2 * MIX_W:3 * MIX_W] * acc).astype(BF16)

    pin = proj_ref[:, 5 * MIX_W:6 * MIX_W]
    pin_ref[...] = pin
    pg = MIX_W // len(POOL_WINDOWS)
    mixed = []
    for gi, w in enumerate(POOL_WINDOWS):
        lo = gi * pg
        s = pin[:, lo:lo + pg]
        for i in range(1, w):
            r = POOL_CTX - i
            s = s + stp_ref[:, r * MIX_W + lo:r * MIX_W + lo + pg]
        pooled = s / float(min(PAST_LEN + 1, w)) - pin[:, lo:lo + pg]
        mixed.append(_bdot(pooled, poolw_ref[gi]))
    ys_ref[2] = (jnp.concatenate(mixed, axis=1) * pscale_ref[...]).astype(BF16)

    gl = proj_ref[:, 6 * MIX_W:7 * MIX_W] * jax.nn.sigmoid(proj_ref[:, 7 * MIX_W:8 * MIX_W])
    gl_ref[...] = gl
    acc = cdw_ref[CONV_D - 1:CONV_D, :] * _conv_in(gl)
    for k in range(CONV_D - 1):
        acc = acc + cdw_ref[k:k + 1, :] * _conv_in(std_ref[:, k * MIX_W:(k + 1) * MIX_W])
    dc = acc + cdb_ref[...]
    ys_ref[3] = jax.nn.silu(_ln(dc, dlng_ref[...], dlnb_ref[...])).astype(BF16)

    for i in range(N_BRANCH):
        _merge_branch(i, hb_ref, win_ref, ys_ref, wbr_ref, m_ref)
    _merge_tail(x_ref, g1, sh2, sc2, n2g_ref[...], m_ref, wout_ref, wr_ref, br_ref, x1_ref, h2_ref, lg_ref)


def _sample_mixer(x, mod, st_b, st_p, st_d, lw):
    rows = x.shape[0]
    ins = [x, mod, st_b, st_p, st_d,
           lw["n1g"], lw["n2g"], lw["w_in"], lw["a_ln_g"], lw["a_ln_b"], lw["a_w0"], lw["a_b0"],
           lw["conv_b_w"], lw["pool_w"], lw["pool_scale"], lw["conv_d_w"], lw["conv_d_b"], lw["d_ln_g"], lw["d_ln_b"],
           lw["w_branch"], lw["w_out"], lw["w_r"], lw["b_r"]]
    mk = lambda last: jax.ShapeDtypeStruct((rows, last), F32)
    return pl.pallas_call(
        _sample_mixer_kernel,
        out_shape=[mk(D_MODEL), mk(D_MODEL), mk(LANES), mk(MIX_W), mk(MIX_W), mk(MIX_W), mk(MIX_W)],
        scratch_shapes=[pltpu.VMEM((rows, D_MODEL), BF16), pltpu.VMEM((rows, D_MIX_IN), F32),
                        pltpu.VMEM((N_BRANCH, rows, MIX_W), BF16), pltpu.VMEM((rows, D_MODEL), F32)],
        compiler_params=pltpu.CompilerParams(vmem_limit_bytes=MIXER_VMEM_BYTES),
        name="sample_mixer",
    )(*ins)


def _route_kernel(lg_ref, route_ref, cnt_ref, carry_ref):
    tr = lg_ref.shape[0]
    i = pl.program_id(0)

    @pl.when(i == 0)
    def _():
        carry_ref[...] = jnp.zeros_like(carry_ref)

    lg = lg_ref[...]
    lane = lax.broadcasted_iota(jnp.int32, (tr, LANES), 1)
    lanef = lane.astype(F32)
    big = float(LANES)

    def first_argmax(vals):
        mx = jnp.max(vals, axis=-1, keepdims=True)
        return mx, jnp.min(jnp.where(vals == mx, lanef, big), axis=-1, keepdims=True)

    neg = -jnp.inf
    gl = jnp.where(lane < N_GROUPS_MOE, lg, neg)
    gmax, gsel = first_argmax(gl)
    p_g = 1.0 / jnp.sum(jnp.exp(gl - gmax), axis=-1, keepdims=True)
    e_lo = ROUTE_LANE0 + EXPERTS_PER_GROUP * gsel
    emask = jnp.logical_and(lanef >= e_lo, lanef < e_lo + EXPERTS_PER_GROUP)
    el = jnp.where(emask, lg, neg)
    emax = jnp.max(el, axis=-1, keepdims=True)
    ex = jnp.exp(el - emax)
    prob = jnp.where(emask, ex / jnp.sum(ex, axis=-1, keepdims=True), -1.0)
    p1, i1 = first_argmax(prob)
    p2, i2 = first_argmax(jnp.where(lanef == i1, -1.0, prob))
    w1 = p_g * p1 / (p1 + p2)
    w2 = p_g * p2 / (p1 + p2)

    hot1 = lanef == i1
    hot2 = lanef == i2
    hot = jnp.logical_or(hot1, hot2).astype(F32)
    rr = lax.broadcasted_iota(jnp.int32, (tr, tr), 0)
    cc = lax.broadcasted_iota(jnp.int32, (tr, tr), 1)
    before = jnp.dot((rr > cc).astype(BF16), hot.astype(BF16), preferred_element_type=F32) + carry_ref[...]
    rank1 = jnp.sum(jnp.where(hot1, before, 0.0), axis=-1, keepdims=True)
    rank2 = jnp.sum(jnp.where(hot2, before, 0.0), axis=-1, keepdims=True)
    carry_ref[...] = carry_ref[...] + jnp.sum(hot, axis=0, keepdims=True)
    cnt_ref[...] = carry_ref[...]

    fields = (i1 - ROUTE_LANE0, i2 - ROUTE_LANE0, rank1, rank2, w1, w2)
    out = jnp.zeros((tr, LANES), F32)
    for k, f in enumerate(fields):
        out = jnp.where(lane == k, f, out)
    route_ref[...] = out


def _route(logits):
    t = logits.shape[0]
    tr = next(c for c in range(ROUTE_TR, 0, -LANES) if t % c == 0)
    return pl.pallas_call(
        _route_kernel,
        grid=(t // tr,),
        in_specs=[pl.BlockSpec((tr, LANES), lambda i: (i, 0))],
        out_specs=[pl.BlockSpec((tr, LANES), lambda i: (i, 0)), pl.BlockSpec((1, LANES), lambda i: (0, 0))],
        out_shape=[jax.ShapeDtypeStruct((t, LANES), F32), jax.ShapeDtypeStruct((1, LANES), F32)],
        scratch_shapes=[pltpu.VMEM((1, LANES), F32)],
        compiler_params=pltpu.CompilerParams(dimension_semantics=("arbitrary",)),
        name="route",
    )(logits)


def _expert_kernel(be_ref, tok_ref, nu_ref, h2_hbm, wg_ref, wu_ref, wd_ref, ys_ref, buf, sem, wgb, wub, wdb):
    bm = buf.shape[1]
    b = pl.program_id(0)
    nu = nu_ref[0]
    slot = b % 2

    def issue(blk, s):
        def body(r, carry):
            tok = tok_ref[blk * bm + r]
            pltpu.make_async_copy(h2_hbm.at[pl.ds(tok, 1)], buf.at[s, pl.ds(r, 1)], sem.at[s]).start()
            return carry
        lax.fori_loop(0, bm, body, 0, unroll=DMA_UNROLL)

    @pl.when(b == 0)
    def _():
        issue(0, 0)

    @pl.when(b + 1 < nu)
    def _():
        issue(b + 1, 1 - slot)

    @pl.when(b >= nu)
    def _():
        ys_ref[...] = jnp.zeros_like(ys_ref)

    @pl.when(b < nu)
    def _():
        pltpu.make_async_copy(h2_hbm.at[pl.ds(0, bm)], buf.at[slot], sem.at[slot]).wait()

        @pl.when(jnp.logical_or(b == 0, be_ref[b] != be_ref[jnp.maximum(b - 1, 0)]))
        def _():
            wgb[...] = wg_ref[0, 0].astype(BF16)
            wub[...] = wu_ref[0, 0].astype(BF16)
            wdb[...] = wd_ref[0, 0].astype(BF16)

        xb = buf[slot].astype(BF16)
        g = jnp.dot(xb, wgb[...], preferred_element_type=F32)
        u = jnp.dot(xb, wub[...], preferred_element_type=F32)
        ys_ref[...] = jnp.dot((jax.nn.silu(g) * u).astype(BF16), wdb[...], preferred_element_type=F32)


def _experts(h2, slot_tok, block_e, n_used, li, w_eg, w_eu, w_ed):
    n_blocks = block_e.shape[0]
    bm = MOE_BM
    wspec = lambda shape: pl.BlockSpec((1, 1) + shape, lambda b, be, tok, nu: (li, be[b], 0, 0))
    return pl.pallas_call(
        _expert_kernel,
        grid_spec=pltpu.PrefetchScalarGridSpec(
            num_scalar_prefetch=3,
            grid=(n_blocks,),
            in_specs=[pl.BlockSpec(memory_space=pl.ANY), wspec((D_MODEL, D_EXPERT)), wspec((D_MODEL, D_EXPERT)),
                      wspec((D_EXPERT, D_MODEL))],
            out_specs=pl.BlockSpec((bm, D_MODEL), lambda b, be, tok, nu: (b, 0)),
            scratch_shapes=[pltpu.VMEM((2, bm, D_MODEL), F32), pltpu.SemaphoreType.DMA((2,)),
                            pltpu.VMEM((D_MODEL, D_EXPERT), BF16), pltpu.VMEM((D_MODEL, D_EXPERT), BF16),
                            pltpu.VMEM((D_EXPERT, D_MODEL), BF16)]),
        out_shape=jax.ShapeDtypeStruct((n_blocks * bm, D_MODEL), F32),
        compiler_params=pltpu.CompilerParams(dimension_semantics=("arbitrary",), disable_bounds_checks=True),
        name="experts",
    )(block_e, slot_tok, n_used, h2, w_eg, w_eu, w_ed)


def _combine_kernel(pos_ref, ys_hbm, x1_ref, route_ref, g2_ref, fg_ref, out_ref, buf, sem, *, final):
    tc = x1_ref.shape[0]
    i = pl.program_id(0)
    n = pl.num_programs(0)
    slot = i % 2

    def issue(tile, s):
        def body(r, carry):
            for k in range(2):
                p = pos_ref[(tile * tc + r) * 2 + k]
                pltpu.make_async_copy(ys_hbm.at[pl.ds(p, 1)], buf.at[s, k, pl.ds(r, 1)], sem.at[s]).start()
            return carry
        lax.fori_loop(0, tc, body, 0, unroll=DMA_UNROLL)

    @pl.when(i == 0)
    def _():
        issue(0, 0)

    @pl.when(i + 1 < n)
    def _():
        issue(i + 1, 1 - slot)

    for k in range(2):
        pltpu.make_async_copy(ys_hbm.at[pl.ds(0, tc)], buf.at[slot, k], sem.at[slot]).wait()
    route = route_ref[...]
    y = route[:, 4:5] * buf[slot, 0] + route[:, 5:6] * buf[slot, 1]
    x2 = x1_ref[...] + g2_ref[...].reshape(-1, D_MODEL) * y
    out_ref[...] = _rms(x2, fg_ref[...]) if final else x2


def _combine(ys, pos, x1, route, g2, fg, final):
    t = x1.shape[0]
    tc = min(t, COMB_TC)
    if g2.ndim == 3:
        seq = t // g2.shape[0]
        g2spec = pl.BlockSpec((1, 1, D_MODEL), lambda i, pos: (i * tc // seq, 0, 0))
    else:
        g2spec = pl.BlockSpec((tc, D_MODEL), lambda i, pos: (i, 0))
    tokspec = lambda last: pl.BlockSpec((tc, last), lambda i, pos: (i, 0))
    return pl.pallas_call(
        functools.partial(_combine_kernel, final=final),
        grid_spec=pltpu.PrefetchScalarGridSpec(
            num_scalar_prefetch=1,
            grid=(t // tc,),
            in_specs=[pl.BlockSpec(memory_space=pl.ANY), tokspec(D_MODEL), tokspec(LANES), g2spec,
                      pl.BlockSpec((1, D_MODEL), lambda i, pos: (0, 0))],
            out_specs=tokspec(D_MODEL),
            scratch_shapes=[pltpu.VMEM((2, 2, tc, D_MODEL), F32), pltpu.SemaphoreType.DMA((2,))]),
        out_shape=jax.ShapeDtypeStruct((t, D_MODEL), F32),
        compiler_params=pltpu.CompilerParams(dimension_semantics=("arbitrary",), disable_bounds_checks=True),
        name="combine",
    )(pos, ys, x1, route, g2, fg)


def _moe_experts(h2, logits, li, w_eg, w_eu, w_ed):
    t = h2.shape[0]
    bm = MOE_BM
    route, counts = _route(logits)
    expert = route[:, 0:2].astype(jnp.int32)
    rank = route[:, 2:4].astype(jnp.int32)
    cnt = counts[0, ROUTE_LANE0:ROUTE_LANE0 + N_EXPERTS].astype(jnp.int32)
    padded = (cnt + bm - 1) // bm * bm
    ends = jnp.cumsum(padded)
    pstarts = ends - padded
    dest = (pstarts[expert] + rank).reshape(-1)
    n_blocks = -(-(t * 2) // bm) + N_EXPERTS
    flat_tok = jnp.arange(t * 2, dtype=jnp.int32) // 2
    slot_tok = jnp.zeros((n_blocks * bm,), jnp.int32).at[dest].set(flat_tok)
    block_start = jnp.arange(n_blocks, dtype=jnp.int32) * bm
    block_e = jnp.minimum(jnp.sum(ends[None, :] <= block_start[:, None], axis=1), N_EXPERTS - 1).astype(jnp.int32)
    n_used = (ends[-1:] // bm).astype(jnp.int32)
    ys = _experts(h2, slot_tok, block_e, n_used, li, w_eg, w_eu, w_ed)
    return ys, dest.astype(jnp.int32), route


def _layer_weights(li, norm1_g, norm2_g, w_in, a_ln_g, a_ln_b, a_ws, a_bs, conv_b_w, pool_w, pool_scale,
                   conv_d_w, conv_d_b, d_ln_g, d_ln_b, w_branch, w_out, w_rg, b_rg, w_re, b_re):
    row = lambda a: a[li].reshape(1, -1)
    cg = MIX_W // A_GROUPS
    pad = LANES - N_GROUPS_MOE - N_EXPERTS
    return {
        "n1g": row(norm1_g), "n2g": row(norm2_g), "w_in": w_in[li].astype(BF16),
        "a_ln_g": row(a_ln_g), "a_ln_b": row(a_ln_b), "a_ws": a_ws[li],
        "a_bs_full": jnp.repeat(a_bs[li].T, cg, axis=1),
        "a_w0": jnp.repeat(a_ws[li, :, 0, 0], cg).reshape(1, MIX_W),
        "a_b0": jnp.repeat(a_bs[li, :, 0], cg).reshape(1, MIX_W),
        "conv_b_w": conv_b_w[li], "pool_w": pool_w[li], "pool_scale": row(pool_scale),
        "conv_b_w8": jnp.broadcast_to(conv_b_w[li][:, None, :], (CONV_B, SUBLANES, MIX_W)),
        "conv_d_w8": jnp.broadcast_to(conv_d_w[li][:, None, :], (CONV_D, SUBLANES, MIX_W)),
        "conv_d_w": conv_d_w[li], "conv_d_b": row(conv_d_b), "d_ln_g": row(d_ln_g), "d_ln_b": row(d_ln_b),
        "w_branch": w_branch[li].astype(BF16), "w_out": w_out[li].astype(BF16),
        "w_r": jnp.concatenate([w_rg[li], w_re[li], jnp.zeros((D_MODEL, pad), F32)], axis=1),
        "b_r": jnp.concatenate([b_rg[li], b_re[li], jnp.zeros((pad,), F32)]).reshape(1, LANES),
    }


def kernel(x_prompt, x_sample, state_conv_b, state_pool, state_conv_d, c_prompt, c_sample, w_mod, b_mod, norm1_g, norm2_g, w_in, a_ln_g, a_ln_b, a_ws, a_bs, conv_b_w, pool_w, pool_scale, conv_d_w, conv_d_b, d_ln_g, d_ln_b, w_branch, w_out, w_rg, b_rg, w_re, b_re, w_eg, w_eu, w_ed, final_g):
    depth = w_mod.shape[0]
    nb, seq, _ = x_prompt.shape
    ns = x_sample.shape[0]
    mod_all = _modulation(jnp.concatenate([c_prompt, c_sample], axis=0), w_mod, b_mod)
    fg = final_g.reshape(1, D_MODEL)

    xp = x_prompt
    xs = x_sample.reshape(ns, D_MODEL)
    outs = {k: [] for k in ("bp", "pp", "dp", "bs", "ps", "ds", "vs")}
    for li in range(depth):
        lw = _layer_weights(li, norm1_g, norm2_g, w_in, a_ln_g, a_ln_b, a_ws, a_bs, conv_b_w, pool_w, pool_scale,
                            conv_d_w, conv_d_b, d_ln_g, d_ln_b, w_branch, w_out, w_rg, b_rg, w_re, b_re)
        final = li == depth - 1
        mod_p = mod_all[li, :nb].reshape(nb, 1, 6 * D_MODEL)
        mod_s = mod_all[li, nb:]

        x1p, h2p, lgp, sb, sp, sd = _prompt_mixer(xp, mod_p, lw)
        outs["bp"].append(sb)
        outs["pp"].append(sp)
        outs["dp"].append(sd)

        st_b, st_p, st_d = state_conv_b[li], state_pool[li], state_conv_d[li]
        x1s, h2s, lgs, cx, pin, gl, v = _sample_mixer(xs, mod_s, st_b.reshape(ns, -1), st_p.reshape(ns, -1),
                                                      st_d.reshape(ns, -1), lw)
        outs["bs"].append(jnp.concatenate([st_b[:, 1:], cx[:, None]], axis=1))
        outs["ps"].append(jnp.concatenate([st_p[:, 1:], pin[:, None]], axis=1))
        outs["ds"].append(jnp.concatenate([st_d[:, 1:], gl[:, None]], axis=1))
        outs["vs"].append(v[:, None])

        t = nb * seq
        ys, dest, route = _moe_experts(jnp.concatenate([h2p.reshape(t, D_MODEL), h2s], axis=0),
                                       jnp.concatenate([lgp.reshape(t, LANES), lgs], axis=0), li, w_eg, w_eu, w_ed)
        xp = _combine(ys, dest[:2 * t], x1p.reshape(t, D_MODEL), route[:t], mod_p[:, :, 5 * D_MODEL:], fg,
                      final).reshape(nb, seq, D_MODEL)
        xs = _combine(ys, dest[2 * t:], x1s, route[t:], mod_s[:, 5 * D_MODEL:], fg, final)

    stack = lambda k: jnp.stack(outs[k])
    return (xp, xs.reshape(ns, 1, D_MODEL), stack("bp"), stack("pp"), stack("dp"),
            stack("bs"), stack("ps"), stack("ds"), stack("vs"))
```

```python
import functools

import jax
import jax.numpy as jnp
from jax import lax
from jax.experimental import pallas as pl
from jax.experimental.pallas import tpu as pltpu

F32 = jnp.float32
BF16 = jnp.bfloat16
I32 = jnp.int32

D_MODEL = 1024
MIX_W = D_MODEL // 2
CHUNK = 128
A_GROUPS = 4
CONV_B = 3
POOL_WINDOWS = (2, 4, 8, 16)
POOL_CTX = 15
CONV_D = 31
N_BRANCH = 4
N_GROUPS_MOE = 4
EXPERTS_PER_GROUP = 8
N_EXPERTS = N_GROUPS_MOE * EXPERTS_PER_GROUP
D_EXPERT = D_MODEL // 4
EPS = 1e-6
D_MIX_IN = 8 * MIX_W
D_IN = D_MIX_IN + N_BRANCH * D_MODEL
PAST_LEN = 16384

LANES = 128
SUBLANES = 8
ROUTE_LANE0 = N_GROUPS_MOE
ROUTE_E, ROUTE_SLOT, ROUTE_W = 0, 2, 4

MIX_TM = 256
CONV_ROWS = 32
TAIL_ROWS = 256
CTX_B = 8
CTX_P = 32
CTX_D = 32
POOL_TMP_SLOTS = sum(w.bit_length() - 2 for w in POOL_WINDOWS)
MOE_BM = 128
COMB_TC = 256
DMA_UNROLL = 8
MIXER_VMEM_BYTES = 56 * 1024 * 1024


def _rms(x, g):
    return x * lax.rsqrt(jnp.mean(x * x, axis=-1, keepdims=True) + EPS) * g


def _ln(x, g, b):
    mu = jnp.mean(x, axis=-1, keepdims=True)
    xc = x - mu
    return xc * lax.rsqrt(jnp.mean(xc * xc, axis=-1, keepdims=True) + EPS) * g + b


def _conv_in(x):
    return x.astype(BF16).astype(F32)


def _bdot(a, b):
    return jnp.dot(a.astype(BF16), b.astype(BF16), preferred_element_type=F32)


def _mod_kernel(c_ref, w_ref, b_ref, o_ref):
    o_ref[0] = _bdot(c_ref[...], w_ref[0]) + b_ref[0]


def _modulation(c_all, w_mod, b_mod):
    depth = w_mod.shape[0]
    rows = c_all.shape[0]
    tn = 1536
    return pl.pallas_call(
        _mod_kernel,
        grid=(depth, 6 * D_MODEL // tn),
        in_specs=[pl.BlockSpec((rows, D_MODEL), lambda l, n: (0, 0)),
                  pl.BlockSpec((1, D_MODEL, tn), lambda l, n: (l, 0, n)),
                  pl.BlockSpec((1, 1, tn), lambda l, n: (l, 0, n))],
        out_specs=pl.BlockSpec((1, rows, tn), lambda l, n: (l, 0, n)),
        out_shape=jax.ShapeDtypeStruct((depth, rows, 6 * D_MODEL), F32),
        name="modulation",
    )(c_all, w_mod, b_mod.reshape(depth, 1, 6 * D_MODEL))


def _project(h, win_ref, hb_ref, proj_ref, between=None):
    hb_ref[...] = h.astype(BF16)
    for i, c in enumerate(range(0, D_MIX_IN, D_MODEL)):
        proj_ref[:, c:c + D_MODEL] = jnp.dot(hb_ref[...], win_ref[:, c:c + D_MODEL], preferred_element_type=F32)
        if between is not None:
            between(i)


def _merge_branch(i, hb_ref, win_ref, ys_ref, wbr_ref, m_ref):
    lo = D_MIX_IN + i * D_MODEL
    rows = hb_ref.shape[0]
    for r0 in range(0, rows, TAIL_ROWS):
        rs = slice(r0, min(r0 + TAIL_ROWS, rows))
        gates = jnp.dot(hb_ref[rs, :], win_ref[:, lo:lo + D_MODEL], preferred_element_type=F32)
        br = jnp.dot(ys_ref[i, rs, :], wbr_ref[i], preferred_element_type=F32)
        t = jax.nn.sigmoid(gates) * br
        m_ref[rs, :] = t if i == 0 else m_ref[rs, :] + t


def _merge_tail(x_ref, g1, sh2, sc2, n2g, m_ref, wout_ref, wr_ref, br_ref, x1_ref, h2_ref, lg_ref):
    rows = m_ref.shape[0]
    for r0 in range(0, rows, TAIL_ROWS):
        rs = slice(r0, min(r0 + TAIL_ROWS, rows))
        per_row = lambda a: a if a.shape[0] == 1 else a[rs]
        out = jnp.dot(m_ref[rs, :].astype(BF16), wout_ref[...], preferred_element_type=F32)
        x1 = x_ref[rs, :] + per_row(g1) * out
        h2 = _rms(x1, n2g) * (1.0 + per_row(sc2)) + per_row(sh2)
        x1_ref[rs, :] = x1
        h2_ref[rs, :] = h2
        lg_ref[rs, :] = _bdot(h2, wr_ref[...]) + br_ref[...]


def _route_rows(lg, carry_ref, cap):
    tr = lg.shape[0]
    lane = lax.broadcasted_iota(I32, (tr, LANES), 1)
    lanef = lane.astype(F32)

    def first_argmax(vals):
        mx = jnp.max(vals, axis=-1, keepdims=True)
        return mx, jnp.min(jnp.where(vals == mx, lanef, float(LANES)), axis=-1, keepdims=True)

    neg = -jnp.inf
    gl = jnp.where(lane < N_GROUPS_MOE, lg, neg)
    gmax, gsel = first_argmax(gl)
    p_g = 1.0 / jnp.sum(jnp.exp(gl - gmax), axis=-1, keepdims=True)
    e_lo = ROUTE_LANE0 + EXPERTS_PER_GROUP * gsel
    emask = jnp.logical_and(lanef >= e_lo, lanef < e_lo + EXPERTS_PER_GROUP)
    el = jnp.where(emask, lg, neg)
    emax = jnp.max(el, axis=-1, keepdims=True)
    ex = jnp.exp(el - emax)
    prob = jnp.where(emask, ex / jnp.sum(ex, axis=-1, keepdims=True), -1.0)
    p1, i1 = first_argmax(prob)
    p2, i2 = first_argmax(jnp.where(lanef == i1, -1.0, prob))
    w1 = p_g * p1 / (p1 + p2)
    w2 = p_g * p2 / (p1 + p2)

    hot1 = lanef == i1
    hot2 = lanef == i2
    hot = jnp.logical_or(hot1, hot2).astype(F32)
    rr = lax.broadcasted_iota(I32, (tr, tr), 0)
    cc = lax.broadcasted_iota(I32, (tr, tr), 1)
    before = jnp.dot((rr > cc).astype(BF16), hot.astype(BF16), preferred_element_type=F32) + carry_ref[...]
    rank1 = jnp.sum(jnp.where(hot1, before, 0.0), axis=-1, keepdims=True)
    rank2 = jnp.sum(jnp.where(hot2, before, 0.0), axis=-1, keepdims=True)
    carry_ref[...] = carry_ref[...] + jnp.sum(hot, axis=0, keepdims=True)

    e1 = i1 - ROUTE_LANE0
    e2 = i2 - ROUTE_LANE0
    fields = (e1, e2, e1 * float(cap) + rank1, e2 * float(cap) + rank2, w1, w2)
    out = jnp.zeros((tr, LANES), F32)
    for k, f in enumerate(fields):
        out = jnp.where(lane == k, f, out)
    return out


def _slots_to_smem(route, slot_v, slot_s, sem):
    slot_v[0:2, :] = jnp.transpose(route)[ROUTE_SLOT:ROUTE_SLOT + 2, :].astype(I32)
    pltpu.make_async_copy(slot_v.at[pl.ds(0, 2)], slot_s, sem).start()


def _combine_rows(x1_prev, route_prev, g2, gbuf):
    return x1_prev + g2 * (route_prev[:, ROUTE_W:ROUTE_W + 1] * gbuf[0]
                           + route_prev[:, ROUTE_W + 1:ROUTE_W + 2] * gbuf[1])


def _tap(w8_ref, k, x):
    rows, c = x.shape
    return (x.reshape(rows // SUBLANES, SUBLANES, c) * w8_ref[k][None]).reshape(rows, c)


PROMPT_CONSTS = ("n1g", "n2g", "w_in", "a_ln_g", "a_ln_b", "a_ws", "a_bs_full", "conv_b_w8", "pool_w", "pool_scale",
                 "conv_d_w8", "conv_d_b", "d_ln_g", "d_ln_b", "w_branch", "w_out", "w_r", "b_r")


def _prompt_mixer_kernel(slots_prev, *refs, names, first, cap):
    r = dict(zip(names, refs))
    tm = r["x1"].shape[1]
    hb_ref, proj_ref, ys_ref = r["hb"], r["proj"], r["ys"]
    cx_ext, p_ext, gl_ext, pool_tmp, conv_tmp = r["cx_ext"], r["p_ext"], r["gl_ext"], r["pool_tmp"], r["conv_tmp"]
    h2buf, slot_v, slot_s, carry, sem = r["h2buf"], r["slot_v"], r["slot_s"], r["carry"], r["sem"]
    xs_hbm = r["xs"]
    win_ref, wbr_ref = r["w_in"], r["w_branch"]
    m_ref = r["x1"].at[0]
    j = pl.program_id(1)
    nj = pl.num_programs(1)
    step = pl.program_id(0) * nj + j
    nsteps = pl.num_programs(0) * nj
    trash = N_EXPERTS * cap
    sem_g, sem_d, sem_s = sem.at[0], sem.at[1], sem.at[2]

    def gather_row(tile, row):
        for k in range(2):
            p = slots_prev[(tile * tm + row) * 2 + k]
            pltpu.make_async_copy(r["ys_prev"].at[pl.ds(p, 1)], r["gbuf"].at[k, pl.ds(row, 1)], sem_g).start()

    def dispatch_row(row):
        for k in range(2):
            pltpu.make_async_copy(h2buf.at[pl.ds(row, 1)], xs_hbm.at[pl.ds(slot_s[k, row], 1)], sem_d).start()

    def wait_rows(src, dst, sem_, times):
        for _ in range(times):
            pltpu.make_async_copy(src, dst, sem_).wait()

    @pl.when(step == 0)
    def _():
        carry[...] = r["carry_in"][...]
        h2buf[...] = jnp.zeros_like(h2buf)
        slot_v[...] = (trash + tm * lax.broadcasted_iota(I32, slot_v.shape, 0)
                       + lax.broadcasted_iota(I32, slot_v.shape, 1))
        pltpu.make_async_copy(slot_v.at[pl.ds(0, 2)], slot_s, sem_s).start()
        if not first:
            lax.fori_loop(0, tm, lambda i, c: (gather_row(0, i), c)[1], 0, unroll=DMA_UNROLL)

    @pl.when(j == 0)
    def _():
        cx_ext[0:CTX_B, :] = jnp.zeros((CTX_B, MIX_W), F32)
        p_ext[0:CTX_P, :] = jnp.zeros((CTX_P, MIX_W), F32)
        gl_ext[0:CTX_D, :] = jnp.zeros((CTX_D, MIX_W), F32)

    mod = r["mod"][0]
    sh1, sc1, g1, sh2, sc2 = (mod[:, i * D_MODEL:(i + 1) * D_MODEL] for i in range(5))
    if first:
        x_src = r["x"].at[0]
    else:
        wait_rows(r["ys_prev"].at[pl.ds(0, tm)], r["gbuf"].at[0], sem_g, 2)
        x_src = r["xin"]
        x_src[...] = _combine_rows(r["x"][0], r["route_prev"][0], r["mod_prev"][0][:, 5 * D_MODEL:], r["gbuf"])
    h = _rms(x_src[...], r["n1g"][...]) * (1.0 + sc1) + sh1

    pltpu.make_async_copy(slot_v.at[pl.ds(0, 2)], slot_s, sem_s).wait()
    n_groups = D_MIX_IN // D_MODEL

    def dispatch_chunk(i):
        for row in range(i * tm // n_groups, (i + 1) * tm // n_groups):
            dispatch_row(row)

    _project(h, win_ref, hb_ref, proj_ref, between=dispatch_chunk)

    u = jax.nn.gelu(proj_ref[:, 0:MIX_W])
    v = _ln(jax.nn.gelu(proj_ref[:, MIX_W:2 * MIX_W]), r["a_ln_g"][...], r["a_ln_b"][...])
    vb = v.astype(BF16)
    rr = lax.broadcasted_iota(I32, (CHUNK, CHUNK), 0)
    cc = lax.broadcasted_iota(I32, (CHUNK, CHUNK), 1)
    wtril = [jnp.where(rr >= cc, r["a_ws"][g], 0.0).astype(BF16) for g in range(A_GROUPS)]
    cg = MIX_W // A_GROUPS
    for c0 in range(0, tm, CHUNK):
        mixed = jnp.concatenate(
            [jnp.dot(wtril[g], vb[c0:c0 + CHUNK, g * cg:(g + 1) * cg], preferred_element_type=F32)
             for g in range(A_GROUPS)], axis=1) + r["a_bs_full"][...]
        ys_ref[0, c0:c0 + CHUNK, :] = (u[c0:c0 + CHUNK] * mixed).astype(BF16)
    _merge_branch(0, hb_ref, win_ref, ys_ref, wbr_ref, m_ref)

    cbw_ref = r["conv_b_w8"]
    for c0 in range(0, tm, CHUNK):
        cx_ext[CTX_B + c0:CTX_B + c0 + CHUNK, :] = _conv_in(
            proj_ref[c0:c0 + CHUNK, 3 * MIX_W:4 * MIX_W] * proj_ref[c0:c0 + CHUNK, 4 * MIX_W:5 * MIX_W])
    for r0 in range(0, tm, CONV_ROWS):
        acc = None
        for k in range(CONV_B):
            t = _tap(cbw_ref, k, cx_ext[pl.ds(r0 + CTX_B - (CONV_B - 1) + k, CONV_ROWS), :])
            acc = t if acc is None else acc + t
        ys_ref[1, r0:r0 + CONV_ROWS, :] = (proj_ref[r0:r0 + CONV_ROWS, 2 * MIX_W:3 * MIX_W] * acc).astype(BF16)
    tail = pl.ds(tm - (CONV_B - 1), CONV_B - 1)
    r["new_b"][0] = proj_ref[tail, 3 * MIX_W:4 * MIX_W] * proj_ref[tail, 4 * MIX_W:5 * MIX_W]
    cx_ext[0:CTX_B, :] = cx_ext[tm:tm + CTX_B, :]
    _merge_branch(1, hb_ref, win_ref, ys_ref, wbr_ref, m_ref)

    p_ext[CTX_P:CTX_P + tm, :] = proj_ref[:, 5 * MIX_W:6 * MIX_W]
    pos = j * tm + lax.broadcasted_iota(I32, (tm, 1), 0)
    pg = MIX_W // len(POOL_WINDOWS)
    mixed = []
    for gi, w in enumerate(POOL_WINDOWS):
        lo = gi * pg
        levels = w.bit_length() - 1
        for lv in range(levels):
            span = 1 << lv
            start = CTX_P if lv == levels - 1 else SUBLANES * (lv + 1)
            rows = CTX_P + tm - start
            if lv == 0:
                a = p_ext[start:start + rows, lo:lo + pg] + p_ext[pl.ds(start - span, rows), lo:lo + pg]
            else:
                prev = pool_tmp.at[gi * (gi - 1) // 2 + lv - 1]
                a = prev[start:start + rows, :] + prev[pl.ds(start - span, rows), :]
            if lv == levels - 1:
                s = a
            else:
                pool_tmp[gi * (gi - 1) // 2 + lv, start:start + rows, :] = a
        cnt = jnp.minimum(pos + 1, w).astype(F32)
        pooled = s / cnt - p_ext[CTX_P:CTX_P + tm, lo:lo + pg]
        mixed.append(_bdot(pooled, r["pool_w"][gi]))
    ys_ref[2] = (jnp.concatenate(mixed, axis=1) * r["pool_scale"][...]).astype(BF16)
    r["new_p"][0] = p_ext[pl.ds(CTX_P + tm - POOL_CTX, POOL_CTX), :]
    p_ext[0:CTX_P, :] = p_ext[tm:tm + CTX_P, :]
    _merge_branch(2, hb_ref, win_ref, ys_ref, wbr_ref, m_ref)

    cdw_ref = r["conv_d_w8"]
    for c0 in range(0, tm, CHUNK):
        gl_ext[CTX_D + c0:CTX_D + c0 + CHUNK, :] = _conv_in(
            proj_ref[c0:c0 + CHUNK, 6 * MIX_W:7 * MIX_W] * jax.nn.sigmoid(proj_ref[c0:c0 + CHUNK, 7 * MIX_W:8 * MIX_W]))
    off0 = CTX_D - (CONV_D - 1)
    next_tile = jnp.minimum(step + 1, nsteps - 1)
    for r0 in range(0, tm, CONV_ROWS):
        acc = None
        for rr_ in range(SUBLANES):
            rows = CONV_ROWS if rr_ == 0 else CONV_ROWS + SUBLANES
            y = None
            for q in range(-(-(off0 + CONV_D) // SUBLANES)):
                k = SUBLANES * q + rr_ - off0
                if 0 <= k < CONV_D:
                    t = _tap(cdw_ref, k, gl_ext[r0 + SUBLANES * q:r0 + SUBLANES * q + rows, :])
                    y = t if y is None else y + t
            if rr_ > 0:
                conv_tmp[rr_ - 1] = y
                y = conv_tmp[rr_ - 1, pl.ds(rr_, CONV_ROWS), :]
            acc = y if acc is None else acc + y
        dc = acc + r["conv_d_b"][...]
        ys_ref[3, r0:r0 + CONV_ROWS, :] = jax.nn.silu(_ln(dc, r["d_ln_g"][...], r["d_ln_b"][...])).astype(BF16)
        if not first:
            for row in range(r0, r0 + CONV_ROWS):
                gather_row(next_tile, row)
    tail = pl.ds(tm - (CONV_D - 1), CONV_D - 1)
    r["new_d"][0] = proj_ref[tail, 6 * MIX_W:7 * MIX_W] * jax.nn.sigmoid(proj_ref[tail, 7 * MIX_W:8 * MIX_W])
    gl_ext[0:CTX_D, :] = gl_ext[tm:tm + CTX_D, :]
    _merge_branch(3, hb_ref, win_ref, ys_ref, wbr_ref, m_ref)

    wait_rows(h2buf, xs_hbm.at[pl.ds(0, tm)], sem_d, 2)
    _merge_tail(x_src, g1, sh2, sc2, r["n2g"][...], m_ref, r["w_out"], r["w_r"], r["b_r"],
                r["x1"].at[0], h2buf, r["lg"])
    route = _route_rows(r["lg"][...], carry, cap)
    r["route"][0] = route
    r["carry_out"][...] = carry[...]
    _slots_to_smem(route, slot_v, slot_s, sem_s)

    @pl.when(step == nsteps - 1)
    def _():
        pltpu.make_async_copy(slot_v.at[pl.ds(0, 2)], slot_s, sem_s).wait()
        lax.fori_loop(0, tm, lambda i, c: (dispatch_row(i), c)[1], 0, unroll=DMA_UNROLL)
        wait_rows(h2buf, xs_hbm.at[pl.ds(0, tm)], sem_d, 2)
        if not first:
            wait_rows(r["ys_prev"].at[pl.ds(0, tm)], r["gbuf"].at[0], sem_g, 2)


def _const_spec(shape):
    nd = len(shape)
    return pl.BlockSpec(shape, lambda *_: (0,) * nd, pipeline_mode=pl.Buffered(1))


def _prompt_mixer(x, mod, lw, cap, prev=None):
    n, l, _ = x.shape
    tm = MIX_TM
    first = prev is None
    tok = lambda last: pl.BlockSpec((1, tm, last), lambda b, j, sp: (b, j, 0))
    per_seq = lambda rows, last: pl.BlockSpec((1, rows, last), lambda b, j, sp: (b, 0, 0))
    anyspec = pl.BlockSpec(memory_space=pl.ANY)
    names, ins, in_specs = ["x", "mod", "carry_in"], [x, mod, jnp.zeros((1, LANES), F32)], \
        [tok(D_MODEL), per_seq(1, 6 * D_MODEL), _const_spec((1, LANES))]
    if first:
        slots_prev = jnp.zeros((SUBLANES,), I32)
    else:
        route_prev, mod_prev, ys_prev, slots_prev = prev
        names += ["route_prev", "mod_prev", "ys_prev"]
        ins += [route_prev, mod_prev, ys_prev]
        in_specs += [tok(LANES), per_seq(1, 6 * D_MODEL), anyspec]
    names += list(PROMPT_CONSTS)
    ins += [lw[k] for k in PROMPT_CONSTS]
    in_specs += [_const_spec(lw[k].shape) for k in PROMPT_CONSTS]

    xs_rows = N_EXPERTS * cap + 2 * tm
    out_names = ["x1", "route", "carry_out", "xs", "new_b", "new_p", "new_d"]
    out_specs = [tok(D_MODEL), tok(LANES), pl.BlockSpec((1, LANES), lambda b, j, sp: (0, 0)), anyspec,
                 per_seq(CONV_B - 1, MIX_W), per_seq(POOL_CTX, MIX_W), per_seq(CONV_D - 1, MIX_W)]
    out_shape = [jax.ShapeDtypeStruct((n, l, D_MODEL), F32), jax.ShapeDtypeStruct((n, l, LANES), F32),
                 jax.ShapeDtypeStruct((1, LANES), F32), jax.ShapeDtypeStruct((xs_rows, D_MODEL), F32),
                 jax.ShapeDtypeStruct((n, CONV_B - 1, MIX_W), F32), jax.ShapeDtypeStruct((n, POOL_CTX, MIX_W), F32),
                 jax.ShapeDtypeStruct((n, CONV_D - 1, MIX_W), F32)]
    scratch = {"hb": pltpu.VMEM((tm, D_MODEL), BF16), "proj": pltpu.VMEM((tm, D_MIX_IN), F32),
               "ys": pltpu.VMEM((N_BRANCH, tm, MIX_W), BF16),
               "cx_ext": pltpu.VMEM((CTX_B + tm, MIX_W), F32), "p_ext": pltpu.VMEM((CTX_P + tm, MIX_W), F32),
               "gl_ext": pltpu.VMEM((CTX_D + tm, MIX_W), F32),
               "pool_tmp": pltpu.VMEM((POOL_TMP_SLOTS, CTX_P + tm, MIX_W // len(POOL_WINDOWS)), F32),
               "conv_tmp": pltpu.VMEM((SUBLANES - 1, CONV_ROWS + SUBLANES, MIX_W), F32),
               "h2buf": pltpu.VMEM((tm, D_MODEL), F32), "lg": pltpu.VMEM((tm, LANES), F32),
               "slot_v": pltpu.VMEM((SUBLANES, tm), I32), "slot_s": pltpu.SMEM((2, tm), I32),
               "carry": pltpu.VMEM((1, LANES), F32), "sem": pltpu.SemaphoreType.DMA((3,))}
    if not first:
        scratch["xin"] = pltpu.VMEM((tm, D_MODEL), F32)
        scratch["gbuf"] = pltpu.VMEM((2, tm, D_MODEL), F32)
    names = tuple(names + out_names + list(scratch))
    outs = pl.pallas_call(
        functools.partial(_prompt_mixer_kernel, names=names, first=first, cap=cap),
        grid_spec=pltpu.PrefetchScalarGridSpec(
            num_scalar_prefetch=1, grid=(n, l // tm), in_specs=in_specs, out_specs=out_specs,
            scratch_shapes=list(scratch.values())),
        out_shape=out_shape,
        compiler_params=pltpu.CompilerParams(dimension_semantics=("arbitrary", "arbitrary"),
                                             vmem_limit_bytes=MIXER_VMEM_BYTES, disable_bounds_checks=True),
        name="prompt_mixer",
    )(slots_prev, *ins)
    return dict(zip(out_names, outs))


SAMPLE_CONSTS = ("n1g", "n2g", "w_in", "a_ln_g", "a_ln_b", "a_w0", "a_b0", "conv_b_w", "pool_w", "pool_scale",
                 "conv_d_w", "conv_d_b", "d_ln_g", "d_ln_b", "w_branch", "w_out", "w_r", "b_r")


def _sample_mixer_kernel(slots_prev, *refs, names, first, cap):
    r = dict(zip(names, refs))
    rows = r["x1"].shape[0]
    hb_ref, proj_ref, ys_ref, m_ref = r["hb"], r["proj"], r["ys"], r["m"]
    h2buf, slot_v, slot_s, carry, sem = r["h2buf"], r["slot_v"], r["slot_s"], r["carry"], r["sem"]
    xs_hbm = r["xs"]
    win_ref, wbr_ref = r["w_in"], r["w_branch"]
    cbw_ref, cdw_ref, stb_ref, stp_ref, std_ref = r["conv_b_w"], r["conv_d_w"], r["st_b"], r["st_p"], r["st_d"]
    sem_g, sem_d, sem_s = sem.at[0], sem.at[1], sem.at[2]

    mod = r["mod"][...]
    sh1, sc1, g1, sh2, sc2 = (mod[:, i * D_MODEL:(i + 1) * D_MODEL] for i in range(5))
    if first:
        x_src = r["x"]
    else:
        def gather_row(i, c):
            for k in range(2):
                p = slots_prev[i * 2 + k]
                pltpu.make_async_copy(r["ys_prev"].at[pl.ds(p, 1)], r["gbuf"].at[k, pl.ds(i, 1)], sem_g).start()
            return c
        lax.fori_loop(0, rows, gather_row, 0, unroll=DMA_UNROLL)
        for _ in range(2):
            pltpu.make_async_copy(r["ys_prev"].at[pl.ds(0, rows)], r["gbuf"].at[0], sem_g).wait()
        x_src = r["xin"]
        x_src[...] = _combine_rows(r["x"][...], r["route_prev"][...], r["mod_prev"][...][:, 5 * D_MODEL:], r["gbuf"])
    h = _rms(x_src[...], r["n1g"][...]) * (1.0 + sc1) + sh1
    _project(h, win_ref, hb_ref, proj_ref)

    u = jax.nn.gelu(proj_ref[:, 0:MIX_W])
    v = _ln(jax.nn.gelu(proj_ref[:, MIX_W:2 * MIX_W]), r["a_ln_g"][...], r["a_ln_b"][...])
    r["v"][...] = v
    ys_ref[0] = (u * (r["a_w0"][...] * v + r["a_b0"][...])).astype(BF16)

    cx = proj_ref[:, 3 * MIX_W:4 * MIX_W] * proj_ref[:, 4 * MIX_W:5 * MIX_W]
    r["cx"][...] = cx
    acc = cbw_ref[CONV_B - 1:CONV_B, :] * _conv_in(cx)
    for k in range(CONV_B - 1):
        acc = acc + cbw_ref[k:k + 1, :] * _conv_in(stb_ref[:, k * MIX_W:(k + 1) * MIX_W])
    ys_ref[1] = (proj_ref[:, 2 * MIX_W:3 * MIX_W] * acc).astype(BF16)

    pin = proj_ref[:, 5 * MIX_W:6 * MIX_W]
    r["pin"][...] = pin
    pg = MIX_W // len(POOL_WINDOWS)
    mixed = []
    for gi, w in enumerate(POOL_WINDOWS):
        lo = gi * pg
        s = pin[:, lo:lo + pg]
        for i in range(1, w):
            row = POOL_CTX - i
            s = s + stp_ref[:, row * MIX_W + lo:row * MIX_W + lo + pg]
        pooled = s / float(min(PAST_LEN + 1, w)) - pin[:, lo:lo + pg]
        mixed.append(_bdot(pooled, r["pool_w"][gi]))
    ys_ref[2] = (jnp.concatenate(mixed, axis=1) * r["pool_scale"][...]).astype(BF16)

    gl = proj_ref[:, 6 * MIX_W:7 * MIX_W] * jax.nn.sigmoid(proj_ref[:, 7 * MIX_W:8 * MIX_W])
    r["gl"][...] = gl
    acc = cdw_ref[CONV_D - 1:CONV_D, :] * _conv_in(gl)
    for k in range(CONV_D - 1):
        acc = acc + cdw_ref[k:k + 1, :] * _conv_in(std_ref[:, k * MIX_W:(k + 1) * MIX_W])
    dc = acc + r["conv_d_b"][...]
    ys_ref[3] = jax.nn.silu(_ln(dc, r["d_ln_g"][...], r["d_ln_b"][...])).astype(BF16)

    for i in range(N_BRANCH):
        _merge_branch(i, hb_ref, win_ref, ys_ref, wbr_ref, m_ref)
    _merge_tail(x_src, g1, sh2, sc2, r["n2g"][...], m_ref, r["w_out"], r["w_r"], r["b_r"], r["x1"], h2buf, r["lg"])

    carry[...] = r["carry_in"][...]
    route = _route_rows(r["lg"][...], carry, cap)
    r["route"][...] = route
    r["carry_out"][...] = carry[...]
    _slots_to_smem(route, slot_v, slot_s, sem_s)
    pltpu.make_async_copy(slot_v.at[pl.ds(0, 2)], slot_s, sem_s).wait()

    def dispatch_row(i, c):
        for k in range(2):
            pltpu.make_async_copy(h2buf.at[pl.ds(i, 1)], xs_hbm.at[pl.ds(slot_s[k, i], 1)], sem_d).start()
        return c
    lax.fori_loop(0, rows, dispatch_row, 0, unroll=DMA_UNROLL)
    for _ in range(2):
        pltpu.make_async_copy(h2buf, xs_hbm.at[pl.ds(0, rows)], sem_d).wait()


def _sample_mixer(x, mod, st_b, st_p, st_d, lw, cap, carry_in, xs, prev=None):
    rows = x.shape[0]
    first = prev is None
    full = lambda a: pl.BlockSpec(a.shape, lambda i, sp: (0,) * a.ndim)
    anyspec = pl.BlockSpec(memory_space=pl.ANY)
    names, ins = ["x", "mod", "carry_in", "xs_in", "st_b", "st_p", "st_d"], [x, mod, carry_in, xs, st_b, st_p, st_d]
    in_specs = [full(x), full(mod), full(carry_in), anyspec, full(st_b), full(st_p), full(st_d)]
    if first:
        slots_prev = jnp.zeros((SUBLANES,), I32)
    else:
        route_prev, mod_prev, ys_prev, slots_prev = prev
        names += ["route_prev", "mod_prev", "ys_prev"]
        ins += [route_prev, mod_prev, ys_prev]
        in_specs += [full(route_prev), full(mod_prev), anyspec]
    names += list(SAMPLE_CONSTS)
    ins += [lw[k] for k in SAMPLE_CONSTS]
    in_specs += [full(lw[k]) for k in SAMPLE_CONSTS]

    mk = lambda last: jax.ShapeDtypeStruct((rows, last), F32)
    out_names = ["x1", "route", "carry_out", "xs", "cx", "pin", "gl", "v"]
    out_shape = [mk(D_MODEL), mk(LANES), jax.ShapeDtypeStruct((1, LANES), F32),
                 jax.ShapeDtypeStruct(xs.shape, xs.dtype), mk(MIX_W), mk(MIX_W), mk(MIX_W), mk(MIX_W)]
    out_specs = [pl.BlockSpec(s.shape, lambda i, sp, nd=len(s.shape): (0,) * nd) for s in out_shape]
    out_specs[3] = anyspec
    scratch = {"hb": pltpu.VMEM((rows, D_MODEL), BF16), "proj": pltpu.VMEM((rows, D_MIX_IN), F32),
               "ys": pltpu.VMEM((N_BRANCH, rows, MIX_W), BF16), "m": pltpu.VMEM((rows, D_MODEL), F32),
               "h2buf": pltpu.VMEM((rows, D_MODEL), F32), "lg": pltpu.VMEM((rows, LANES), F32),
               "slot_v": pltpu.VMEM((SUBLANES, rows), I32), "slot_s": pltpu.SMEM((2, rows), I32),
               "carry": pltpu.VMEM((1, LANES), F32), "sem": pltpu.SemaphoreType.DMA((3,))}
    if not first:
        scratch["xin"] = pltpu.VMEM((rows, D_MODEL), F32)
        scratch["gbuf"] = pltpu.VMEM((2, rows, D_MODEL), F32)
    names = tuple(names + out_names + list(scratch))
    outs = pl.pallas_call(
        functools.partial(_sample_mixer_kernel, names=names, first=first, cap=cap),
        grid_spec=pltpu.PrefetchScalarGridSpec(
            num_scalar_prefetch=1, grid=(1,), in_specs=in_specs, out_specs=out_specs,
            scratch_shapes=list(scratch.values())),
        out_shape=out_shape,
        input_output_aliases={1 + names.index("xs_in"): out_names.index("xs")},
        compiler_params=pltpu.CompilerParams(dimension_semantics=("arbitrary",), vmem_limit_bytes=MIXER_VMEM_BYTES,
                                             disable_bounds_checks=True),
        name="sample_mixer",
    )(slots_prev, *ins)
    return dict(zip(out_names, outs))


def _expert_kernel(vb_ref, be_ref, nv_ref, nu_ref, xs_ref, wg_ref, wu_ref, wd_ref, ys_ref, wgb, wub, wdb):
    b = pl.program_id(0)

    @pl.when(b < nu_ref[0])
    def _():
        @pl.when(jnp.logical_or(b == 0, be_ref[b] != be_ref[jnp.maximum(b - 1, 0)]))
        def _():
            wgb[...] = wg_ref[0, 0].astype(BF16)
            wub[...] = wu_ref[0, 0].astype(BF16)
            wdb[...] = wd_ref[0, 0].astype(BF16)

        valid = lax.broadcasted_iota(I32, (xs_ref.shape[0], 1), 0) < nv_ref[b]
        xb = jnp.where(valid, xs_ref[...], 0.0).astype(BF16)
        g = jnp.dot(xb, wgb[...], preferred_element_type=F32)
        u = jnp.dot(xb, wub[...], preferred_element_type=F32)
        ys_ref[...] = jnp.dot((jax.nn.silu(g) * u).astype(BF16), wdb[...], preferred_element_type=F32)


def _experts(xs, counts, cap, li, w_eg, w_eu, w_ed):
    bm = MOE_BM
    n_blocks = -(-xs.shape[0] // bm)
    max_used = min(N_EXPERTS * (cap // bm), -(-(cap * 2) // bm) + N_EXPERTS)
    nblk = (counts + bm - 1) // bm
    ends = jnp.cumsum(nblk)
    n_used = ends[-1:]
    b = jnp.arange(max_used, dtype=I32)
    bc = jnp.minimum(b, n_used[0] - 1)
    be = jnp.minimum(jnp.sum(ends[None, :] <= bc[:, None], axis=1), N_EXPERTS - 1).astype(I32)
    local = bc - (ends - nblk)[be]
    vblock = (be * (cap // bm) + local).astype(I32)
    nvalid = jnp.clip(counts[be] - local * bm, 0, bm).astype(I32)
    wspec = lambda shape: pl.BlockSpec((1, 1) + shape, lambda i, vb, be_, nv, nu: (li, be_[i], 0, 0))
    return pl.pallas_call(
        _expert_kernel,
        grid_spec=pltpu.PrefetchScalarGridSpec(
            num_scalar_prefetch=4,
            grid=(max_used,),
            in_specs=[pl.BlockSpec((bm, D_MODEL), lambda i, vb, be_, nv, nu: (vb[i], 0)),
                      wspec((D_MODEL, D_EXPERT)), wspec((D_MODEL, D_EXPERT)), wspec((D_EXPERT, D_MODEL))],
            out_specs=pl.BlockSpec((bm, D_MODEL), lambda i, vb, be_, nv, nu: (vb[i], 0)),
            scratch_shapes=[pltpu.VMEM((D_MODEL, D_EXPERT), BF16), pltpu.VMEM((D_MODEL, D_EXPERT), BF16),
                            pltpu.VMEM((D_EXPERT, D_MODEL), BF16)]),
        out_shape=jax.ShapeDtypeStruct((n_blocks * bm, D_MODEL), F32),
        compiler_params=pltpu.CompilerParams(dimension_semantics=("arbitrary",)),
        name="experts",
    )(vblock, be, nvalid, n_used.astype(I32), xs, w_eg, w_eu, w_ed)


def _combine_kernel(pos_ref, ys_hbm, x1_ref, route_ref, g2_ref, fg_ref, out_ref, buf, sem):
    tc = x1_ref.shape[0]
    i = pl.program_id(0)
    n = pl.num_programs(0)
    slot = i % 2

    def issue(tile, s):
        def body(row, carry):
            for k in range(2):
                p = pos_ref[(tile * tc + row) * 2 + k]
                pltpu.make_async_copy(ys_hbm.at[pl.ds(p, 1)], buf.at[s, k, pl.ds(row, 1)], sem.at[s]).start()
            return carry
        lax.fori_loop(0, tc, body, 0, unroll=DMA_UNROLL)

    @pl.when(i == 0)
    def _():
        issue(0, 0)

    @pl.when(i + 1 < n)
    def _():
        issue(i + 1, 1 - slot)

    for k in range(2):
        pltpu.make_async_copy(ys_hbm.at[pl.ds(0, tc)], buf.at[slot, k], sem.at[slot]).wait()
    x2 = _combine_rows(x1_ref[...], route_ref[...], g2_ref[...].reshape(-1, D_MODEL), buf.at[slot])
    out_ref[...] = _rms(x2, fg_ref[...])


def _combine(ys, pos, x1, route, g2, fg):
    t = x1.shape[0]
    tc = min(t, COMB_TC)
    if g2.ndim == 3:
        seq = t // g2.shape[0]
        g2spec = pl.BlockSpec((1, 1, D_MODEL), lambda i, pos: (i * tc // seq, 0, 0))
    else:
        g2spec = pl.BlockSpec((tc, D_MODEL), lambda i, pos: (i, 0))
    tokspec = lambda last: pl.BlockSpec((tc, last), lambda i, pos: (i, 0))
    return pl.pallas_call(
        _combine_kernel,
        grid_spec=pltpu.PrefetchScalarGridSpec(
            num_scalar_prefetch=1,
            grid=(t // tc,),
            in_specs=[pl.BlockSpec(memory_space=pl.ANY), tokspec(D_MODEL), tokspec(LANES), g2spec,
                      pl.BlockSpec((1, D_MODEL), lambda i, pos: (0, 0))],
            out_specs=tokspec(D_MODEL),
            scratch_shapes=[pltpu.VMEM((2, 2, tc, D_MODEL), F32), pltpu.SemaphoreType.DMA((2,))]),
        out_shape=jax.ShapeDtypeStruct((t, D_MODEL), F32),
        compiler_params=pltpu.CompilerParams(dimension_semantics=("arbitrary",), disable_bounds_checks=True),
        name="combine",
    )(pos, ys, x1, route, g2, fg)


def _layer_weights(li, norm1_g, norm2_g, w_in, a_ln_g, a_ln_b, a_ws, a_bs, conv_b_w, pool_w, pool_scale,
                   conv_d_w, conv_d_b, d_ln_g, d_ln_b, w_branch, w_out, w_rg, b_rg, w_re, b_re):
    row = lambda a: a[li].reshape(1, -1)
    cg = MIX_W // A_GROUPS
    pad = LANES - N_GROUPS_MOE - N_EXPERTS
    return {
        "n1g": row(norm1_g), "n2g": row(norm2_g), "w_in": w_in[li].astype(BF16),
        "a_ln_g": row(a_ln_g), "a_ln_b": row(a_ln_b), "a_ws": a_ws[li],
        "a_bs_full": jnp.repeat(a_bs[li].T, cg, axis=1),
        "a_w0": jnp.repeat(a_ws[li, :, 0, 0], cg).reshape(1, MIX_W),
        "a_b0": jnp.repeat(a_bs[li, :, 0], cg).reshape(1, MIX_W),
        "conv_b_w": conv_b_w[li], "pool_w": pool_w[li], "pool_scale": row(pool_scale),
        "conv_b_w8": jnp.broadcast_to(conv_b_w[li][:, None, :], (CONV_B, SUBLANES, MIX_W)),
        "conv_d_w8": jnp.broadcast_to(conv_d_w[li][:, None, :], (CONV_D, SUBLANES, MIX_W)),
        "conv_d_w": conv_d_w[li], "conv_d_b": row(conv_d_b), "d_ln_g": row(d_ln_g), "d_ln_b": row(d_ln_b),
        "w_branch": w_branch[li].astype(BF16), "w_out": w_out[li].astype(BF16),
        "w_r": jnp.concatenate([w_rg[li], w_re[li], jnp.zeros((D_MODEL, pad), F32)], axis=1),
        "b_r": jnp.concatenate([b_rg[li], b_re[li], jnp.zeros((pad,), F32)]).reshape(1, LANES),
    }


def _slots(route):
    return route[:, ROUTE_SLOT:ROUTE_SLOT + 2].astype(I32).reshape(-1)


def kernel(x_prompt, x_sample, state_conv_b, state_pool, state_conv_d, c_prompt, c_sample, w_mod, b_mod, norm1_g, norm2_g, w_in, a_ln_g, a_ln_b, a_ws, a_bs, conv_b_w, pool_w, pool_scale, conv_d_w, conv_d_b, d_ln_g, d_ln_b, w_branch, w_out, w_rg, b_rg, w_re, b_re, w_eg, w_eu, w_ed, final_g):
    depth = w_mod.shape[0]
    nb, seq, _ = x_prompt.shape
    ns = x_sample.shape[0]
    t = nb * seq
    cap = -(-(t + ns) // MOE_BM) * MOE_BM
    mod_all = _modulation(jnp.concatenate([c_prompt, c_sample], axis=0), w_mod, b_mod)
    fg = final_g.reshape(1, D_MODEL)

    xp = x_prompt
    xs_tok = x_sample.reshape(ns, D_MODEL)
    prev_p = prev_s = None
    outs = {k: [] for k in ("bp", "pp", "dp", "bs", "ps", "ds", "vs")}
    for li in range(depth):
        lw = _layer_weights(li, norm1_g, norm2_g, w_in, a_ln_g, a_ln_b, a_ws, a_bs, conv_b_w, pool_w, pool_scale,
                            conv_d_w, conv_d_b, d_ln_g, d_ln_b, w_branch, w_out, w_rg, b_rg, w_re, b_re)
        mod_p = mod_all[li, :nb].reshape(nb, 1, 6 * D_MODEL)
        mod_s = mod_all[li, nb:]

        pm = _prompt_mixer(xp, mod_p, lw, cap, prev_p)
        outs["bp"].append(pm["new_b"])
        outs["pp"].append(pm["new_p"])
        outs["dp"].append(pm["new_d"])

        st_b, st_p, st_d = state_conv_b[li], state_pool[li], state_conv_d[li]
        sm = _sample_mixer(xs_tok, mod_s, st_b.reshape(ns, -1), st_p.reshape(ns, -1), st_d.reshape(ns, -1), lw, cap,
                           pm["carry_out"], pm["xs"], prev_s)
        outs["bs"].append(jnp.concatenate([st_b[:, 1:], sm["cx"][:, None]], axis=1))
        outs["ps"].append(jnp.concatenate([st_p[:, 1:], sm["pin"][:, None]], axis=1))
        outs["ds"].append(jnp.concatenate([st_d[:, 1:], sm["gl"][:, None]], axis=1))
        outs["vs"].append(sm["v"][:, None])

        counts = sm["carry_out"][0, ROUTE_LANE0:ROUTE_LANE0 + N_EXPERTS].astype(I32)
        ys = _experts(sm["xs"], counts, cap, li, w_eg, w_eu, w_ed)
        route_p = pm["route"]
        prev_p = (route_p, mod_p, ys, _slots(route_p.reshape(t, LANES)))
        prev_s = (sm["route"], mod_s, ys, _slots(sm["route"]))
        xp, xs_tok = pm["x1"], sm["x1"]

    route_p, mod_p, ys, slots_p = prev_p
    route_s, mod_s, _, slots_s = prev_s
    y_p = _combine(ys, slots_p, xp.reshape(t, D_MODEL), route_p.reshape(t, LANES), mod_p[:, :, 5 * D_MODEL:], fg)
    y_s = _combine(ys, slots_s, xs_tok, route_s, mod_s[:, 5 * D_MODEL:], fg)

    stack = lambda k: jnp.stack(outs[k])
    return (y_p.reshape(nb, seq, D_MODEL), y_s.reshape(ns, 1, D_MODEL), stack("bp"), stack("pp"), stack("dp"),
            stack("bs"), stack("ps"), stack("ds"), stack("vs"))
```

```python
import functools

import jax
import jax.numpy as jnp
from jax import lax
from jax.experimental import pallas as pl
from jax.experimental.pallas import tpu as pltpu

F32 = jnp.float32
BF16 = jnp.bfloat16
I32 = jnp.int32

D_MODEL = 1024
MIX_W = D_MODEL // 2
CHUNK = 128
A_GROUPS = 4
CONV_B = 3
POOL_WINDOWS = (2, 4, 8, 16)
POOL_CTX = 15
CONV_D = 31
N_BRANCH = 4
N_GROUPS_MOE = 4
EXPERTS_PER_GROUP = 8
N_EXPERTS = N_GROUPS_MOE * EXPERTS_PER_GROUP
D_EXPERT = D_MODEL // 4
EPS = 1e-6
D_MIX_IN = 8 * MIX_W
D_IN = D_MIX_IN + N_BRANCH * D_MODEL
PAST_LEN = 16384

LANES = 128
SUBLANES = 8
ROUTE_LANE0 = N_GROUPS_MOE
ROUTE_E, ROUTE_SLOT, ROUTE_W = 0, 2, 4

MIX_TM = 256
CONV_ROWS = 32
TAIL_ROWS = 256
MXU_COLS = 256
CTX_B = 8
CTX_P = 32
CTX_D = 32
POOL_TMP_SLOTS = sum(w.bit_length() - 2 for w in POOL_WINDOWS)
MOE_BM = 256
COMB_TC = 256
DMA_UNROLL = 8
MIXER_VMEM_BYTES = 56 * 1024 * 1024


def _rms(x, g):
    return x * lax.rsqrt(jnp.mean(x * x, axis=-1, keepdims=True) + EPS) * g


def _ln(x, g, b):
    mu = jnp.mean(x, axis=-1, keepdims=True)
    xc = x - mu
    return xc * lax.rsqrt(jnp.mean(xc * xc, axis=-1, keepdims=True) + EPS) * g + b


def _conv_in(x):
    return x.astype(BF16).astype(F32)


def _bdot(a, b):
    return jnp.dot(a.astype(BF16), b.astype(BF16), preferred_element_type=F32)


def _mod_kernel(c_ref, w_ref, b_ref, o_ref):
    o_ref[0] = _bdot(c_ref[...], w_ref[0]) + b_ref[0]


def _modulation(c_all, w_mod, b_mod):
    depth = w_mod.shape[0]
    rows = c_all.shape[0]
    tn = 1536
    return pl.pallas_call(
        _mod_kernel,
        grid=(depth, 6 * D_MODEL // tn),
        in_specs=[pl.BlockSpec((rows, D_MODEL), lambda l, n: (0, 0)),
                  pl.BlockSpec((1, D_MODEL, tn), lambda l, n: (l, 0, n)),
                  pl.BlockSpec((1, 1, tn), lambda l, n: (l, 0, n))],
        out_specs=pl.BlockSpec((1, rows, tn), lambda l, n: (l, 0, n)),
        out_shape=jax.ShapeDtypeStruct((depth, rows, 6 * D_MODEL), F32),
        name="modulation",
    )(c_all, w_mod, b_mod.reshape(depth, 1, 6 * D_MODEL))


def _project(h, win_ref, hb_ref, proj_ref, between=None):
    hb_ref[...] = h.astype(BF16)
    for i, c in enumerate(range(0, D_MIX_IN, D_MODEL)):
        proj_ref[:, c:c + D_MODEL] = jnp.dot(hb_ref[...], win_ref[:, c:c + D_MODEL], preferred_element_type=F32)
        if between is not None:
            between(i)


def _merge_branch(i, hb_ref, win_ref, ys_ref, wbr_ref, m_ref):
    lo = D_MIX_IN + i * D_MODEL
    rows = hb_ref.shape[0]
    for r0 in range(0, rows, TAIL_ROWS):
        rs = slice(r0, min(r0 + TAIL_ROWS, rows))
        gates = jnp.dot(hb_ref[rs, :], win_ref[:, lo:lo + D_MODEL], preferred_element_type=F32)
        br = jnp.dot(ys_ref[i, rs, :], wbr_ref[i], preferred_element_type=F32)
        t = jax.nn.sigmoid(gates) * br
        m_ref[rs, :] = t if i == 0 else m_ref[rs, :] + t


def _merge_tail(x_ref, g1, sh2, sc2, n2g, m_ref, wout_ref, wr_ref, br_ref, x1_ref, h2_ref, lg_ref):
    rows = m_ref.shape[0]
    for r0 in range(0, rows, TAIL_ROWS):
        rs = slice(r0, min(r0 + TAIL_ROWS, rows))
        per_row = lambda a: a if a.shape[0] == 1 else a[rs]
        out = jnp.dot(m_ref[rs, :].astype(BF16), wout_ref[...], preferred_element_type=F32)
        x1 = x_ref[rs, :] + per_row(g1) * out
        h2 = _rms(x1, n2g) * (1.0 + per_row(sc2)) + per_row(sh2)
        x1_ref[rs, :] = x1
        h2_ref[rs, :] = h2
        lg_ref[rs, :] = _bdot(h2, wr_ref[...]) + br_ref[...]


def _route_rows(lg, carry_ref, cap):
    tr = lg.shape[0]
    lane = lax.broadcasted_iota(I32, (tr, LANES), 1)
    lanef = lane.astype(F32)

    def first_argmax(vals):
        mx = jnp.max(vals, axis=-1, keepdims=True)
        return mx, jnp.min(jnp.where(vals == mx, lanef, float(LANES)), axis=-1, keepdims=True)

    neg = -jnp.inf
    gl = jnp.where(lane < N_GROUPS_MOE, lg, neg)
    gmax, gsel = first_argmax(gl)
    p_g = 1.0 / jnp.sum(jnp.exp(gl - gmax), axis=-1, keepdims=True)
    e_lo = ROUTE_LANE0 + EXPERTS_PER_GROUP * gsel
    emask = jnp.logical_and(lanef >= e_lo, lanef < e_lo + EXPERTS_PER_GROUP)
    el = jnp.where(emask, lg, neg)
    emax = jnp.max(el, axis=-1, keepdims=True)
    ex = jnp.exp(el - emax)
    prob = jnp.where(emask, ex / jnp.sum(ex, axis=-1, keepdims=True), -1.0)
    p1, i1 = first_argmax(prob)
    p2, i2 = first_argmax(jnp.where(lanef == i1, -1.0, prob))
    w1 = p_g * p1 / (p1 + p2)
    w2 = p_g * p2 / (p1 + p2)

    hot1 = lanef == i1
    hot2 = lanef == i2
    hot = jnp.logical_or(hot1, hot2).astype(F32)
    rr = lax.broadcasted_iota(I32, (tr, tr), 0)
    cc = lax.broadcasted_iota(I32, (tr, tr), 1)
    before = jnp.dot((rr > cc).astype(BF16), hot.astype(BF16), preferred_element_type=F32) + carry_ref[...]
    rank1 = jnp.sum(jnp.where(hot1, before, 0.0), axis=-1, keepdims=True)
    rank2 = jnp.sum(jnp.where(hot2, before, 0.0), axis=-1, keepdims=True)
    carry_ref[...] = carry_ref[...] + jnp.sum(hot, axis=0, keepdims=True)

    e1 = i1 - ROUTE_LANE0
    e2 = i2 - ROUTE_LANE0
    fields = (e1, e2, e1 * float(cap) + rank1, e2 * float(cap) + rank2, w1, w2)
    out = jnp.zeros((tr, LANES), F32)
    for k, f in enumerate(fields):
        out = jnp.where(lane == k, f, out)
    return out


def _slots_to_smem(route, slot_v, slot_s, sem):
    slot_v[0:2, :] = jnp.transpose(route)[ROUTE_SLOT:ROUTE_SLOT + 2, :].astype(I32)
    pltpu.make_async_copy(slot_v.at[pl.ds(0, 2)], slot_s, sem).start()


def _combine_rows(x1_prev, route_prev, g2, gbuf):
    return x1_prev + g2 * (route_prev[:, ROUTE_W:ROUTE_W + 1] * gbuf[0]
                           + route_prev[:, ROUTE_W + 1:ROUTE_W + 2] * gbuf[1])


def _tap(w8_ref, k, x):
    rows, c = x.shape
    return (x.reshape(rows // SUBLANES, SUBLANES, c) * w8_ref[k][None]).reshape(rows, c)


PROMPT_CONSTS = ("n1g", "n2g", "w_in", "a_ln_g", "a_ln_b", "a_ws", "a_bs_full", "conv_b_w8", "pool_w", "pool_scale",
                 "conv_d_w8", "conv_d_b", "d_ln_g", "d_ln_b", "w_branch", "w_out", "w_r", "b_r")


def _prompt_mixer_kernel(slots_prev, *refs, names, first, cap):
    r = dict(zip(names, refs))
    tm = r["x1"].shape[1]
    hb_ref, proj_ref, ys_ref = r["hb"], r["proj"], r["ys"]
    cx_ext, p_ext, gl_ext, pool_tmp, conv_tmp = r["cx_ext"], r["p_ext"], r["gl_ext"], r["pool_tmp"], r["conv_tmp"]
    h2buf, slot_v, slot_s, carry, sem = r["h2buf"], r["slot_v"], r["slot_s"], r["carry"], r["sem"]
    xs_hbm = r["xs"]
    win_ref, wbr_ref = r["w_in"], r["w_branch"]
    m_ref = r["x1"].at[0]
    j = pl.program_id(1)
    nj = pl.num_programs(1)
    step = pl.program_id(0) * nj + j
    nsteps = pl.num_programs(0) * nj
    trash = N_EXPERTS * cap
    sem_g, sem_d, sem_s = sem.at[0], sem.at[1], sem.at[2]

    def gather_row(tile, row):
        for k in range(2):
            p = slots_prev[(tile * tm + row) * 2 + k]
            pltpu.make_async_copy(r["ys_prev"].at[pl.ds(p, 1)], r["gbuf"].at[k, pl.ds(row, 1)], sem_g).start()

    def dispatch_row(row):
        for k in range(2):
            pltpu.make_async_copy(h2buf.at[pl.ds(row, 1)], xs_hbm.at[pl.ds(slot_s[k, row], 1)], sem_d).start()

    def wait_rows(src, dst, sem_, times):
        for _ in range(times):
            pltpu.make_async_copy(src, dst, sem_).wait()

    @pl.when(step == 0)
    def _():
        carry[...] = r["carry_in"][...]
        h2buf[...] = jnp.zeros_like(h2buf)
        slot_v[...] = (trash + tm * lax.broadcasted_iota(I32, slot_v.shape, 0)
                       + lax.broadcasted_iota(I32, slot_v.shape, 1))
        pltpu.make_async_copy(slot_v.at[pl.ds(0, 2)], slot_s, sem_s).start()
        if not first:
            lax.fori_loop(0, tm, lambda i, c: (gather_row(0, i), c)[1], 0, unroll=DMA_UNROLL)

    @pl.when(j == 0)
    def _():
        cx_ext[0:CTX_B, :] = jnp.zeros((CTX_B, MIX_W), F32)
        p_ext[0:CTX_P, :] = jnp.zeros((CTX_P, MIX_W), F32)
        gl_ext[0:CTX_D, :] = jnp.zeros((CTX_D, MIX_W), F32)

    mod = r["mod"][0]
    sh1, sc1, g1, sh2, sc2 = (mod[:, i * D_MODEL:(i + 1) * D_MODEL] for i in range(5))
    if first:
        x_src = r["x"].at[0]
    else:
        wait_rows(r["ys_prev"].at[pl.ds(0, tm)], r["gbuf"].at[0], sem_g, 2)
        x_src = r["xin"]
        x_src[...] = _combine_rows(r["x"][0], r["route_prev"][0], r["mod_prev"][0][:, 5 * D_MODEL:], r["gbuf"])
    h = _rms(x_src[...], r["n1g"][...]) * (1.0 + sc1) + sh1

    pltpu.make_async_copy(slot_v.at[pl.ds(0, 2)], slot_s, sem_s).wait()

    def proj_piece(c):
        def f():
            proj_ref[:, c:c + MXU_COLS] = jnp.dot(hb_ref[...], win_ref[:, c:c + MXU_COLS], preferred_element_type=F32)
        return f

    def merge_piece(i, c):
        def f():
            lo = D_MIX_IN + i * D_MODEL + c
            gates = jnp.dot(hb_ref[...], win_ref[:, lo:lo + MXU_COLS], preferred_element_type=F32)
            br = jnp.dot(ys_ref[i], wbr_ref[i, :, c:c + MXU_COLS], preferred_element_type=F32)
            t = jax.nn.sigmoid(gates) * br
            m_ref[:, c:c + MXU_COLS] = t if i == 0 else m_ref[:, c:c + MXU_COLS] + t
        return f

    def merge_pieces(i):
        return [merge_piece(i, c) for c in range(0, D_MODEL, MXU_COLS)]

    def emit(vector_pieces, matrix_pieces):
        nv, nm = len(vector_pieces), len(matrix_pieces)
        for i, vp in enumerate(vector_pieces):
            vp()
            for mp in matrix_pieces[i * nm // nv:(i + 1) * nm // nv]:
                mp()

    def dispatch_chunk(i, n):
        def f():
            for row in range(i * tm // n, (i + 1) * tm // n):
                dispatch_row(row)
        return f

    hb_ref[...] = h.astype(BF16)
    a_cols = list(range(0, 2 * MIX_W, MXU_COLS))
    emit([proj_piece(c) for c in a_cols], [dispatch_chunk(i, len(a_cols)) for i in range(len(a_cols))])

    rr = lax.broadcasted_iota(I32, (CHUNK, CHUNK), 0)
    cc = lax.broadcasted_iota(I32, (CHUNK, CHUNK), 1)
    wtril = [jnp.where(rr >= cc, r["a_ws"][g], 0.0).astype(BF16) for g in range(A_GROUPS)]
    cg = MIX_W // A_GROUPS

    def a_chunk(c0):
        def f():
            u = jax.nn.gelu(proj_ref[c0:c0 + CHUNK, 0:MIX_W])
            v = _ln(jax.nn.gelu(proj_ref[c0:c0 + CHUNK, MIX_W:2 * MIX_W]), r["a_ln_g"][...], r["a_ln_b"][...])
            vb = v.astype(BF16)
            mixed = jnp.concatenate(
                [jnp.dot(wtril[g], vb[:, g * cg:(g + 1) * cg], preferred_element_type=F32)
                 for g in range(A_GROUPS)], axis=1) + r["a_bs_full"][...]
            ys_ref[0, c0:c0 + CHUNK, :] = (u * mixed).astype(BF16)
        return f

    emit([a_chunk(c0) for c0 in range(0, tm, CHUNK)], [proj_piece(c) for c in range(2 * MIX_W, D_MIX_IN, MXU_COLS)])

    cbw_ref = r["conv_b_w8"]

    def b_prep(c0):
        def f():
            cx_ext[CTX_B + c0:CTX_B + c0 + CHUNK, :] = _conv_in(
                proj_ref[c0:c0 + CHUNK, 3 * MIX_W:4 * MIX_W] * proj_ref[c0:c0 + CHUNK, 4 * MIX_W:5 * MIX_W])
        return f

    def b_block(r0):
        def f():
            acc = None
            for k in range(CONV_B):
                t = _tap(cbw_ref, k, cx_ext[pl.ds(r0 + CTX_B - (CONV_B - 1) + k, CONV_ROWS), :])
                acc = t if acc is None else acc + t
            ys_ref[1, r0:r0 + CONV_ROWS, :] = (proj_ref[r0:r0 + CONV_ROWS, 2 * MIX_W:3 * MIX_W] * acc).astype(BF16)
        return f

    def b_end():
        tail = pl.ds(tm - (CONV_B - 1), CONV_B - 1)
        r["new_b"][0] = proj_ref[tail, 3 * MIX_W:4 * MIX_W] * proj_ref[tail, 4 * MIX_W:5 * MIX_W]
        cx_ext[0:CTX_B, :] = cx_ext[tm:tm + CTX_B, :]

    emit([b_prep(c0) for c0 in range(0, tm, CHUNK)] + [b_block(r0) for r0 in range(0, tm, CONV_ROWS)] + [b_end],
         merge_pieces(0))

    pos = j * tm + lax.broadcasted_iota(I32, (tm, 1), 0)
    pg = MIX_W // len(POOL_WINDOWS)

    def c_prep():
        p_ext[CTX_P:CTX_P + tm, :] = proj_ref[:, 5 * MIX_W:6 * MIX_W]

    def c_group(gi, w):
        def f():
            lo = gi * pg
            levels = w.bit_length() - 1
            for lv in range(levels):
                span = 1 << lv
                start = CTX_P if lv == levels - 1 else SUBLANES * (lv + 1)
                rows = CTX_P + tm - start
                if lv == 0:
                    a = p_ext[start:start + rows, lo:lo + pg] + p_ext[pl.ds(start - span, rows), lo:lo + pg]
                else:
                    prev = pool_tmp.at[gi * (gi - 1) // 2 + lv - 1]
                    a = prev[start:start + rows, :] + prev[pl.ds(start - span, rows), :]
                if lv == levels - 1:
                    s = a
                else:
                    pool_tmp[gi * (gi - 1) // 2 + lv, start:start + rows, :] = a
            cnt = jnp.minimum(pos + 1, w).astype(F32)
            pooled = s / cnt - p_ext[CTX_P:CTX_P + tm, lo:lo + pg]
            ys_ref[2, :, lo:lo + pg] = (_bdot(pooled, r["pool_w"][gi]) * r["pool_scale"][:, lo:lo + pg]).astype(BF16)
        return f

    def c_end():
        r["new_p"][0] = p_ext[pl.ds(CTX_P + tm - POOL_CTX, POOL_CTX), :]
        p_ext[0:CTX_P, :] = p_ext[tm:tm + CTX_P, :]

    emit([c_prep] + [c_group(gi, w) for gi, w in enumerate(POOL_WINDOWS)] + [c_end], merge_pieces(1))

    cdw_ref = r["conv_d_w8"]
    off0 = CTX_D - (CONV_D - 1)
    next_tile = jnp.minimum(step + 1, nsteps - 1)

    def d_prep(c0):
        def f():
            gl_ext[CTX_D + c0:CTX_D + c0 + CHUNK, :] = _conv_in(
                proj_ref[c0:c0 + CHUNK, 6 * MIX_W:7 * MIX_W]
                * jax.nn.sigmoid(proj_ref[c0:c0 + CHUNK, 7 * MIX_W:8 * MIX_W]))
        return f

    def d_block(r0):
        def f():
            acc = None
            for rr_ in range(SUBLANES):
                rows = CONV_ROWS if rr_ == 0 else CONV_ROWS + SUBLANES
                y = None
                for q in range(-(-(off0 + CONV_D) // SUBLANES)):
                    k = SUBLANES * q + rr_ - off0
                    if 0 <= k < CONV_D:
                        t = _tap(cdw_ref, k, gl_ext[r0 + SUBLANES * q:r0 + SUBLANES * q + rows, :])
                        y = t if y is None else y + t
                if rr_ > 0:
                    conv_tmp[rr_ - 1] = y
                    y = conv_tmp[rr_ - 1, pl.ds(rr_, CONV_ROWS), :]
                acc = y if acc is None else acc + y
            dc = acc + r["conv_d_b"][...]
            ys_ref[3, r0:r0 + CONV_ROWS, :] = jax.nn.silu(_ln(dc, r["d_ln_g"][...], r["d_ln_b"][...])).astype(BF16)
            if not first:
                for row in range(r0, r0 + CONV_ROWS):
                    gather_row(next_tile, row)
        return f

    def d_end():
        tail = pl.ds(tm - (CONV_D - 1), CONV_D - 1)
        r["new_d"][0] = proj_ref[tail, 6 * MIX_W:7 * MIX_W] * jax.nn.sigmoid(proj_ref[tail, 7 * MIX_W:8 * MIX_W])
        gl_ext[0:CTX_D, :] = gl_ext[tm:tm + CTX_D, :]

    emit([d_prep(c0) for c0 in range(0, tm, CHUNK)] + [d_block(r0) for r0 in range(0, tm, CONV_ROWS)] + [d_end],
         merge_pieces(2))
    for mp in merge_pieces(3):
        mp()

    wait_rows(h2buf, xs_hbm.at[pl.ds(0, tm)], sem_d, 2)
    _merge_tail(x_src, g1, sh2, sc2, r["n2g"][...], m_ref, r["w_out"], r["w_r"], r["b_r"],
                r["x1"].at[0], h2buf, r["lg"])
    route = _route_rows(r["lg"][...], carry, cap)
    r["route"][0] = route
    r["carry_out"][...] = carry[...]
    _slots_to_smem(route, slot_v, slot_s, sem_s)

    @pl.when(step == nsteps - 1)
    def _():
        pltpu.make_async_copy(slot_v.at[pl.ds(0, 2)], slot_s, sem_s).wait()
        lax.fori_loop(0, tm, lambda i, c: (dispatch_row(i), c)[1], 0, unroll=DMA_UNROLL)
        wait_rows(h2buf, xs_hbm.at[pl.ds(0, tm)], sem_d, 2)
        if not first:
            wait_rows(r["ys_prev"].at[pl.ds(0, tm)], r["gbuf"].at[0], sem_g, 2)


def _const_spec(shape):
    nd = len(shape)
    return pl.BlockSpec(shape, lambda *_: (0,) * nd, pipeline_mode=pl.Buffered(1))


def _prompt_mixer(x, mod, lw, cap, prev=None):
    n, l, _ = x.shape
    tm = MIX_TM
    first = prev is None
    tok = lambda last: pl.BlockSpec((1, tm, last), lambda b, j, sp: (b, j, 0))
    per_seq = lambda rows, last: pl.BlockSpec((1, rows, last), lambda b, j, sp: (b, 0, 0))
    anyspec = pl.BlockSpec(memory_space=pl.ANY)
    names, ins, in_specs = ["x", "mod", "carry_in"], [x, mod, jnp.zeros((1, LANES), F32)], \
        [tok(D_MODEL), per_seq(1, 6 * D_MODEL), _const_spec((1, LANES))]
    if first:
        slots_prev = jnp.zeros((SUBLANES,), I32)
    else:
        route_prev, mod_prev, ys_prev, slots_prev = prev
        names += ["route_prev", "mod_prev", "ys_prev"]
        ins += [route_prev, mod_prev, ys_prev]
        in_specs += [tok(LANES), per_seq(1, 6 * D_MODEL), anyspec]
    names += list(PROMPT_CONSTS)
    ins += [lw[k] for k in PROMPT_CONSTS]
    in_specs += [_const_spec(lw[k].shape) for k in PROMPT_CONSTS]

    xs_rows = N_EXPERTS * cap + 2 * tm
    out_names = ["x1", "route", "carry_out", "xs", "new_b", "new_p", "new_d"]
    out_specs = [tok(D_MODEL), tok(LANES), pl.BlockSpec((1, LANES), lambda b, j, sp: (0, 0)), anyspec,
                 per_seq(CONV_B - 1, MIX_W), per_seq(POOL_CTX, MIX_W), per_seq(CONV_D - 1, MIX_W)]
    out_shape = [jax.ShapeDtypeStruct((n, l, D_MODEL), F32), jax.ShapeDtypeStruct((n, l, LANES), F32),
                 jax.ShapeDtypeStruct((1, LANES), F32), jax.ShapeDtypeStruct((xs_rows, D_MODEL), F32),
                 jax.ShapeDtypeStruct((n, CONV_B - 1, MIX_W), F32), jax.ShapeDtypeStruct((n, POOL_CTX, MIX_W), F32),
                 jax.ShapeDtypeStruct((n, CONV_D - 1, MIX_W), F32)]
    scratch = {"hb": pltpu.VMEM((tm, D_MODEL), BF16), "proj": pltpu.VMEM((tm, D_MIX_IN), F32),
               "ys": pltpu.VMEM((N_BRANCH, tm, MIX_W), BF16),
               "cx_ext": pltpu.VMEM((CTX_B + tm, MIX_W), F32), "p_ext": pltpu.VMEM((CTX_P + tm, MIX_W), F32),
               "gl_ext": pltpu.VMEM((CTX_D + tm, MIX_W), F32),
               "pool_tmp": pltpu.VMEM((POOL_TMP_SLOTS, CTX_P + tm, MIX_W // len(POOL_WINDOWS)), F32),
               "conv_tmp": pltpu.VMEM((SUBLANES - 1, CONV_ROWS + SUBLANES, MIX_W), F32),
               "h2buf": pltpu.VMEM((tm, D_MODEL), F32), "lg": pltpu.VMEM((tm, LANES), F32),
               "slot_v": pltpu.VMEM((SUBLANES, tm), I32), "slot_s": pltpu.SMEM((2, tm), I32),
               "carry": pltpu.VMEM((1, LANES), F32), "sem": pltpu.SemaphoreType.DMA((3,))}
    if not first:
        scratch["xin"] = pltpu.VMEM((tm, D_MODEL), F32)
        scratch["gbuf"] = pltpu.VMEM((2, tm, D_MODEL), F32)
    names = tuple(names + out_names + list(scratch))
    outs = pl.pallas_call(
        functools.partial(_prompt_mixer_kernel, names=names, first=first, cap=cap),
        grid_spec=pltpu.PrefetchScalarGridSpec(
            num_scalar_prefetch=1, grid=(n, l // tm), in_specs=in_specs, out_specs=out_specs,
            scratch_shapes=list(scratch.values())),
        out_shape=out_shape,
        compiler_params=pltpu.CompilerParams(dimension_semantics=("arbitrary", "arbitrary"),
                                             vmem_limit_bytes=MIXER_VMEM_BYTES, disable_bounds_checks=True),
        name="prompt_mixer",
    )(slots_prev, *ins)
    return dict(zip(out_names, outs))


SAMPLE_CONSTS = ("n1g", "n2g", "w_in", "a_ln_g", "a_ln_b", "a_w0", "a_b0", "conv_b_w", "pool_w", "pool_scale",
                 "conv_d_w", "conv_d_b", "d_ln_g", "d_ln_b", "w_branch", "w_out", "w_r", "b_r")


def _sample_mixer_kernel(slots_prev, *refs, names, first, cap):
    r = dict(zip(names, refs))
    rows = r["x1"].shape[0]
    hb_ref, proj_ref, ys_ref, m_ref = r["hb"], r["proj"], r["ys"], r["m"]
    h2buf, slot_v, slot_s, carry, sem = r["h2buf"], r["slot_v"], r["slot_s"], r["carry"], r["sem"]
    xs_hbm = r["xs"]
    win_ref, wbr_ref = r["w_in"], r["w_branch"]
    cbw_ref, cdw_ref, stb_ref, stp_ref, std_ref = r["conv_b_w"], r["conv_d_w"], r["st_b"], r["st_p"], r["st_d"]
    sem_g, sem_d, sem_s = sem.at[0], sem.at[1], sem.at[2]

    mod = r["mod"][...]
    sh1, sc1, g1, sh2, sc2 = (mod[:, i * D_MODEL:(i + 1) * D_MODEL] for i in range(5))
    if first:
        x_src = r["x"]
    else:
        def gather_row(i, c):
            for k in range(2):
                p = slots_prev[i * 2 + k]
                pltpu.make_async_copy(r["ys_prev"].at[pl.ds(p, 1)], r["gbuf"].at[k, pl.ds(i, 1)], sem_g).start()
            return c
        lax.fori_loop(0, rows, gather_row, 0, unroll=DMA_UNROLL)
        for _ in range(2):
            pltpu.make_async_copy(r["ys_prev"].at[pl.ds(0, rows)], r["gbuf"].at[0], sem_g).wait()
        x_src = r["xin"]
        x_src[...] = _combine_rows(r["x"][...], r["route_prev"][...], r["mod_prev"][...][:, 5 * D_MODEL:], r["gbuf"])
    h = _rms(x_src[...], r["n1g"][...]) * (1.0 + sc1) + sh1
    _project(h, win_ref, hb_ref, proj_ref)

    u = jax.nn.gelu(proj_ref[:, 0:MIX_W])
    v = _ln(jax.nn.gelu(proj_ref[:, MIX_W:2 * MIX_W]), r["a_ln_g"][...], r["a_ln_b"][...])
    r["v"][...] = v
    ys_ref[0] = (u * (r["a_w0"][...] * v + r["a_b0"][...])).astype(BF16)

    cx = proj_ref[:, 3 * MIX_W:4 * MIX_W] * proj_ref[:, 4 * MIX_W:5 * MIX_W]
    r["cx"][...] = cx
    acc = cbw_ref[CONV_B - 1:CONV_B, :] * _conv_in(cx)
    for k in range(CONV_B - 1):
        acc = acc + cbw_ref[k:k + 1, :] * _conv_in(stb_ref[:, k * MIX_W:(k + 1) * MIX_W])
    ys_ref[1] = (proj_ref[:, 2 * MIX_W:3 * MIX_W] * acc).astype(BF16)

    pin = proj_ref[:, 5 * MIX_W:6 * MIX_W]
    r["pin"][...] = pin
    pg = MIX_W // len(POOL_WINDOWS)
    mixed = []
    for gi, w in enumerate(POOL_WINDOWS):
        lo = gi * pg
        s = pin[:, lo:lo + pg]
        for i in range(1, w):
            row = POOL_CTX - i
            s = s + stp_ref[:, row * MIX_W + lo:row * MIX_W + lo + pg]
        pooled = s / float(min(PAST_LEN + 1, w)) - pin[:, lo:lo + pg]
        mixed.append(_bdot(pooled, r["pool_w"][gi]))
    ys_ref[2] = (jnp.concatenate(mixed, axis=1) * r["pool_scale"][...]).astype(BF16)

    gl = proj_ref[:, 6 * MIX_W:7 * MIX_W] * jax.nn.sigmoid(proj_ref[:, 7 * MIX_W:8 * MIX_W])
    r["gl"][...] = gl
    acc = cdw_ref[CONV_D - 1:CONV_D, :] * _conv_in(gl)
    for k in range(CONV_D - 1):
        acc = acc + cdw_ref[k:k + 1, :] * _conv_in(std_ref[:, k * MIX_W:(k + 1) * MIX_W])
    dc = acc + r["conv_d_b"][...]
    ys_ref[3] = jax.nn.silu(_ln(dc, r["d_ln_g"][...], r["d_ln_b"][...])).astype(BF16)

    for i in range(N_BRANCH):
        _merge_branch(i, hb_ref, win_ref, ys_ref, wbr_ref, m_ref)
    _merge_tail(x_src, g1, sh2, sc2, r["n2g"][...], m_ref, r["w_out"], r["w_r"], r["b_r"], r["x1"], h2buf, r["lg"])

    carry[...] = r["carry_in"][...]
    route = _route_rows(r["lg"][...], carry, cap)
    r["route"][...] = route
    r["carry_out"][...] = carry[...]
    _slots_to_smem(route, slot_v, slot_s, sem_s)
    pltpu.make_async_copy(slot_v.at[pl.ds(0, 2)], slot_s, sem_s).wait()

    def dispatch_row(i, c):
        for k in range(2):
            pltpu.make_async_copy(h2buf.at[pl.ds(i, 1)], xs_hbm.at[pl.ds(slot_s[k, i], 1)], sem_d).start()
        return c
    lax.fori_loop(0, rows, dispatch_row, 0, unroll=DMA_UNROLL)
    for _ in range(2):
        pltpu.make_async_copy(h2buf, xs_hbm.at[pl.ds(0, rows)], sem_d).wait()


def _sample_mixer(x, mod, st_b, st_p, st_d, lw, cap, carry_in, xs, prev=None):
    rows = x.shape[0]
    first = prev is None
    full = lambda a: pl.BlockSpec(a.shape, lambda i, sp: (0,) * a.ndim)
    anyspec = pl.BlockSpec(memory_space=pl.ANY)
    names, ins = ["x", "mod", "carry_in", "xs_in", "st_b", "st_p", "st_d"], [x, mod, carry_in, xs, st_b, st_p, st_d]
    in_specs = [full(x), full(mod), full(carry_in), anyspec, full(st_b), full(st_p), full(st_d)]
    if first:
        slots_prev = jnp.zeros((SUBLANES,), I32)
    else:
        route_prev, mod_prev, ys_prev, slots_prev = prev
        names += ["route_prev", "mod_prev", "ys_prev"]
        ins += [route_prev, mod_prev, ys_prev]
        in_specs += [full(route_prev), full(mod_prev), anyspec]
    names += list(SAMPLE_CONSTS)
    ins += [lw[k] for k in SAMPLE_CONSTS]
    in_specs += [full(lw[k]) for k in SAMPLE_CONSTS]

    mk = lambda last: jax.ShapeDtypeStruct((rows, last), F32)
    out_names = ["x1", "route", "carry_out", "xs", "cx", "pin", "gl", "v"]
    out_shape = [mk(D_MODEL), mk(LANES), jax.ShapeDtypeStruct((1, LANES), F32),
                 jax.ShapeDtypeStruct(xs.shape, xs.dtype), mk(MIX_W), mk(MIX_W), mk(MIX_W), mk(MIX_W)]
    out_specs = [pl.BlockSpec(s.shape, lambda i, sp, nd=len(s.shape): (0,) * nd) for s in out_shape]
    out_specs[3] = anyspec
    scratch = {"hb": pltpu.VMEM((rows, D_MODEL), BF16), "proj": pltpu.VMEM((rows, D_MIX_IN), F32),
               "ys": pltpu.VMEM((N_BRANCH, rows, MIX_W), BF16), "m": pltpu.VMEM((rows, D_MODEL), F32),
               "h2buf": pltpu.VMEM((rows, D_MODEL), F32), "lg": pltpu.VMEM((rows, LANES), F32),
               "slot_v": pltpu.VMEM((SUBLANES, rows), I32), "slot_s": pltpu.SMEM((2, rows), I32),
               "carry": pltpu.VMEM((1, LANES), F32), "sem": pltpu.SemaphoreType.DMA((3,))}
    if not first:
        scratch["xin"] = pltpu.VMEM((rows, D_MODEL), F32)
        scratch["gbuf"] = pltpu.VMEM((2, rows, D_MODEL), F32)
    names = tuple(names + out_names + list(scratch))
    outs = pl.pallas_call(
        functools.partial(_sample_mixer_kernel, names=names, first=first, cap=cap),
        grid_spec=pltpu.PrefetchScalarGridSpec(
            num_scalar_prefetch=1, grid=(1,), in_specs=in_specs, out_specs=out_specs,
            scratch_shapes=list(scratch.values())),
        out_shape=out_shape,
        input_output_aliases={1 + names.index("xs_in"): out_names.index("xs")},
        compiler_params=pltpu.CompilerParams(dimension_semantics=("arbitrary",), vmem_limit_bytes=MIXER_VMEM_BYTES,
                                             disable_bounds_checks=True),
        name="sample_mixer",
    )(slots_prev, *ins)
    return dict(zip(out_names, outs))


def _expert_kernel(vb_ref, be_ref, nv_ref, nu_ref, xs_ref, wg_ref, wu_ref, wd_ref, ys_ref, wgb, wub, wdb):
    b = pl.program_id(0)

    @pl.when(b < nu_ref[0])
    def _():
        @pl.when(jnp.logical_or(b == 0, be_ref[b] != be_ref[jnp.maximum(b - 1, 0)]))
        def _():
            wgb[...] = wg_ref[0, 0].astype(BF16)
            wub[...] = wu_ref[0, 0].astype(BF16)
            wdb[...] = wd_ref[0, 0].astype(BF16)

        valid = lax.broadcasted_iota(I32, (xs_ref.shape[0], 1), 0) < nv_ref[b]
        xb = jnp.where(valid, xs_ref[...], 0.0).astype(BF16)
        g = jnp.dot(xb, wgb[...], preferred_element_type=F32)
        u = jnp.dot(xb, wub[...], preferred_element_type=F32)
        ys_ref[...] = jnp.dot((jax.nn.silu(g) * u).astype(BF16), wdb[...], preferred_element_type=F32)


def _experts(xs, counts, cap, li, w_eg, w_eu, w_ed):
    bm = MOE_BM
    n_blocks = -(-xs.shape[0] // bm)
    max_used = min(N_EXPERTS * (cap // bm), -(-(cap * 2) // bm) + N_EXPERTS)
    nblk = (counts + bm - 1) // bm
    ends = jnp.cumsum(nblk)
    n_used = ends[-1:]
    b = jnp.arange(max_used, dtype=I32)
    bc = jnp.minimum(b, n_used[0] - 1)
    be = jnp.minimum(jnp.sum(ends[None, :] <= bc[:, None], axis=1), N_EXPERTS - 1).astype(I32)
    local = bc - (ends - nblk)[be]
    vblock = (be * (cap // bm) + local).astype(I32)
    nvalid = jnp.clip(counts[be] - local * bm, 0, bm).astype(I32)
    wspec = lambda shape: pl.BlockSpec((1, 1) + shape, lambda i, vb, be_, nv, nu: (li, be_[i], 0, 0))
    return pl.pallas_call(
        _expert_kernel,
        grid_spec=pltpu.PrefetchScalarGridSpec(
            num_scalar_prefetch=4,
            grid=(max_used,),
            in_specs=[pl.BlockSpec((bm, D_MODEL), lambda i, vb, be_, nv, nu: (vb[i], 0)),
                      wspec((D_MODEL, D_EXPERT)), wspec((D_MODEL, D_EXPERT)), wspec((D_EXPERT, D_MODEL))],
            out_specs=pl.BlockSpec((bm, D_MODEL), lambda i, vb, be_, nv, nu: (vb[i], 0)),
            scratch_shapes=[pltpu.VMEM((D_MODEL, D_EXPERT), BF16), pltpu.VMEM((D_MODEL, D_EXPERT), BF16),
                            pltpu.VMEM((D_EXPERT, D_MODEL), BF16)]),
        out_shape=jax.ShapeDtypeStruct((n_blocks * bm, D_MODEL), F32),
        compiler_params=pltpu.CompilerParams(dimension_semantics=("arbitrary",)),
        name="experts",
    )(vblock, be, nvalid, n_used.astype(I32), xs, w_eg, w_eu, w_ed)


def _combine_kernel(pos_ref, ys_hbm, x1_ref, route_ref, g2_ref, fg_ref, out_ref, buf, sem):
    tc = x1_ref.shape[0]
    i = pl.program_id(0)
    n = pl.num_programs(0)
    slot = i % 2

    def issue(tile, s):
        def body(row, carry):
            for k in range(2):
                p = pos_ref[(tile * tc + row) * 2 + k]
                pltpu.make_async_copy(ys_hbm.at[pl.ds(p, 1)], buf.at[s, k, pl.ds(row, 1)], sem.at[s]).start()
            return carry
        lax.fori_loop(0, tc, body, 0, unroll=DMA_UNROLL)

    @pl.when(i == 0)
    def _():
        issue(0, 0)

    @pl.when(i + 1 < n)
    def _():
        issue(i + 1, 1 - slot)

    for k in range(2):
        pltpu.make_async_copy(ys_hbm.at[pl.ds(0, tc)], buf.at[slot, k], sem.at[slot]).wait()
    x2 = _combine_rows(x1_ref[...], route_ref[...], g2_ref[...].reshape(-1, D_MODEL), buf.at[slot])
    out_ref[...] = _rms(x2, fg_ref[...])


def _combine(ys, pos, x1, route, g2, fg):
    t = x1.shape[0]
    tc = min(t, COMB_TC)
    if g2.ndim == 3:
        seq = t // g2.shape[0]
        g2spec = pl.BlockSpec((1, 1, D_MODEL), lambda i, pos: (i * tc // seq, 0, 0))
    else:
        g2spec = pl.BlockSpec((tc, D_MODEL), lambda i, pos: (i, 0))
    tokspec = lambda last: pl.BlockSpec((tc, last), lambda i, pos: (i, 0))
    return pl.pallas_call(
        _combine_kernel,
        grid_spec=pltpu.PrefetchScalarGridSpec(
            num_scalar_prefetch=1,
            grid=(t // tc,),
            in_specs=[pl.BlockSpec(memory_space=pl.ANY), tokspec(D_MODEL), tokspec(LANES), g2spec,
                      pl.BlockSpec((1, D_MODEL), lambda i, pos: (0, 0))],
            out_specs=tokspec(D_MODEL),
            scratch_shapes=[pltpu.VMEM((2, 2, tc, D_MODEL), F32), pltpu.SemaphoreType.DMA((2,))]),
        out_shape=jax.ShapeDtypeStruct((t, D_MODEL), F32),
        compiler_params=pltpu.CompilerParams(dimension_semantics=("arbitrary",), disable_bounds_checks=True),
        name="combine",
    )(pos, ys, x1, route, g2, fg)


def _layer_weights(li, norm1_g, norm2_g, w_in, a_ln_g, a_ln_b, a_ws, a_bs, conv_b_w, pool_w, pool_scale,
                   conv_d_w, conv_d_b, d_ln_g, d_ln_b, w_branch, w_out, w_rg, b_rg, w_re, b_re):
    row = lambda a: a[li].reshape(1, -1)
    cg = MIX_W // A_GROUPS
    pad = LANES - N_GROUPS_MOE - N_EXPERTS
    return {
        "n1g": row(norm1_g), "n2g": row(norm2_g), "w_in": w_in[li].astype(BF16),
        "a_ln_g": row(a_ln_g), "a_ln_b": row(a_ln_b), "a_ws": a_ws[li],
        "a_bs_full": jnp.repeat(a_bs[li].T, cg, axis=1),
        "a_w0": jnp.repeat(a_ws[li, :, 0, 0], cg).reshape(1, MIX_W),
        "a_b0": jnp.repeat(a_bs[li, :, 0], cg).reshape(1, MIX_W),
        "conv_b_w": conv_b_w[li], "pool_w": pool_w[li], "pool_scale": row(pool_scale),
        "conv_b_w8": jnp.broadcast_to(conv_b_w[li][:, None, :], (CONV_B, SUBLANES, MIX_W)),
        "conv_d_w8": jnp.broadcast_to(conv_d_w[li][:, None, :], (CONV_D, SUBLANES, MIX_W)),
        "conv_d_w": conv_d_w[li], "conv_d_b": row(conv_d_b), "d_ln_g": row(d_ln_g), "d_ln_b": row(d_ln_b),
        "w_branch": w_branch[li].astype(BF16), "w_out": w_out[li].astype(BF16),
        "w_r": jnp.concatenate([w_rg[li], w_re[li], jnp.zeros((D_MODEL, pad), F32)], axis=1),
        "b_r": jnp.concatenate([b_rg[li], b_re[li], jnp.zeros((pad,), F32)]).reshape(1, LANES),
    }


def _slots(route):
    return route[:, ROUTE_SLOT:ROUTE_SLOT + 2].astype(I32).reshape(-1)


def kernel(x_prompt, x_sample, state_conv_b, state_pool, state_conv_d, c_prompt, c_sample, w_mod, b_mod, norm1_g, norm2_g, w_in, a_ln_g, a_ln_b, a_ws, a_bs, conv_b_w, pool_w, pool_scale, conv_d_w, conv_d_b, d_ln_g, d_ln_b, w_branch, w_out, w_rg, b_rg, w_re, b_re, w_eg, w_eu, w_ed, final_g):
    depth = w_mod.shape[0]
    nb, seq, _ = x_prompt.shape
    ns = x_sample.shape[0]
    t = nb * seq
    cap = -(-(t + ns) // MOE_BM) * MOE_BM
    mod_all = _modulation(jnp.concatenate([c_prompt, c_sample], axis=0), w_mod, b_mod)
    fg = final_g.reshape(1, D_MODEL)

    xp = x_prompt
    xs_tok = x_sample.reshape(ns, D_MODEL)
    prev_p = prev_s = None
    outs = {k: [] for k in ("bp", "pp", "dp", "bs", "ps", "ds", "vs")}
    for li in range(depth):
        lw = _layer_weights(li, norm1_g, norm2_g, w_in, a_ln_g, a_ln_b, a_ws, a_bs, conv_b_w, pool_w, pool_scale,
                            conv_d_w, conv_d_b, d_ln_g, d_ln_b, w_branch, w_out, w_rg, b_rg, w_re, b_re)
        mod_p = mod_all[li, :nb].reshape(nb, 1, 6 * D_MODEL)
        mod_s = mod_all[li, nb:]

        pm = _prompt_mixer(xp, mod_p, lw, cap, prev_p)
        outs["bp"].append(pm["new_b"])
        outs["pp"].append(pm["new_p"])
        outs["dp"].append(pm["new_d"])

        st_b, st_p, st_d = state_conv_b[li], state_pool[li], state_conv_d[li]
        sm = _sample_mixer(xs_tok, mod_s, st_b.reshape(ns, -1), st_p.reshape(ns, -1), st_d.reshape(ns, -1), lw, cap,
                           pm["carry_out"], pm["xs"], prev_s)
        outs["bs"].append(jnp.concatenate([st_b[:, 1:], sm["cx"][:, None]], axis=1))
        outs["ps"].append(jnp.concatenate([st_p[:, 1:], sm["pin"][:, None]], axis=1))
        outs["ds"].append(jnp.concatenate([st_d[:, 1:], sm["gl"][:, None]], axis=1))
        outs["vs"].append(sm["v"][:, None])

        counts = sm["carry_out"][0, ROUTE_LANE0:ROUTE_LANE0 + N_EXPERTS].astype(I32)
        ys = _experts(sm["xs"], counts, cap, li, w_eg, w_eu, w_ed)
        route_p = pm["route"]
        prev_p = (route_p, mod_p, ys, _slots(route_p.reshape(t, LANES)))
        prev_s = (sm["route"], mod_s, ys, _slots(sm["route"]))
        xp, xs_tok = pm["x1"], sm["x1"]

    route_p, mod_p, ys, slots_p = prev_p
    route_s, mod_s, _, slots_s = prev_s
    y_p = _combine(ys, slots_p, xp.reshape(t, D_MODEL), route_p.reshape(t, LANES), mod_p[:, :, 5 * D_MODEL:], fg)
    y_s = _combine(ys, slots_s, xs_tok, route_s, mod_s[:, 5 * D_MODEL:], fg)

    stack = lambda k: jnp.stack(outs[k])
    return (y_p.reshape(nb, seq, D_MODEL), y_s.reshape(ns, 1, D_MODEL), stack("bp"), stack("pp"), stack("dp"),
            stack("bs"), stack("ps"), stack("ds"), stack("vs"))
```

```python
import functools

import jax
import jax.numpy as jnp
from jax import lax
from jax.experimental import pallas as pl
from jax.experimental.pallas import tpu as pltpu

F32 = jnp.float32
BF16 = jnp.bfloat16
I32 = jnp.int32

D_MODEL = 1024
MIX_W = D_MODEL // 2
CHUNK = 128
A_GROUPS = 4
CONV_B = 3
POOL_WINDOWS = (2, 4, 8, 16)
POOL_CTX = 15
CONV_D = 31
N_BRANCH = 4
N_GROUPS_MOE = 4
EXPERTS_PER_GROUP = 8
N_EXPERTS = N_GROUPS_MOE * EXPERTS_PER_GROUP
D_EXPERT = D_MODEL // 4
EPS = 1e-6
D_MIX_IN = 8 * MIX_W
D_IN = D_MIX_IN + N_BRANCH * D_MODEL
PAST_LEN = 16384

LANES = 128
SUBLANES = 8
ROUTE_LANE0 = N_GROUPS_MOE
ROUTE_E, ROUTE_SLOT, ROUTE_W = 0, 2, 4

MIX_TM = 512
CONV_ROWS = 32
TAIL_ROWS = 256
MXU_COLS = 256
CTX_B = 8
CTX_P = 32
CTX_D = 32
POOL_TMP_SLOTS = sum(w.bit_length() - 2 for w in POOL_WINDOWS)
MOE_BM = 256
DMA_UNROLL = 8
MIXER_VMEM_BYTES = 56 * 1024 * 1024


def _rms(x, g):
    return x * lax.rsqrt(jnp.mean(x * x, axis=-1, keepdims=True) + EPS) * g


def _ln(x, g, b):
    mu = jnp.mean(x, axis=-1, keepdims=True)
    xc = x - mu
    return xc * lax.rsqrt(jnp.mean(xc * xc, axis=-1, keepdims=True) + EPS) * g + b


def _conv_in(x):
    return x.astype(BF16).astype(F32)


def _bdot(a, b):
    return jnp.dot(a.astype(BF16), b.astype(BF16), preferred_element_type=F32)


def _mod_kernel(c_ref, w_ref, b_ref, o_ref):
    o_ref[0] = _bdot(c_ref[...], w_ref[0]) + b_ref[0]


def _modulation(c_all, w_mod, b_mod):
    depth = w_mod.shape[0]
    rows = c_all.shape[0]
    tn = 1536
    return pl.pallas_call(
        _mod_kernel,
        grid=(depth, 6 * D_MODEL // tn),
        in_specs=[pl.BlockSpec((rows, D_MODEL), lambda l, n: (0, 0)),
                  pl.BlockSpec((1, D_MODEL, tn), lambda l, n: (l, 0, n)),
                  pl.BlockSpec((1, 1, tn), lambda l, n: (l, 0, n))],
        out_specs=pl.BlockSpec((1, rows, tn), lambda l, n: (l, 0, n)),
        out_shape=jax.ShapeDtypeStruct((depth, rows, 6 * D_MODEL), F32),
        name="modulation",
    )(c_all, w_mod, b_mod.reshape(depth, 1, 6 * D_MODEL))


def _project(h, win_ref, hb_ref, proj_ref, between=None):
    hb_ref[...] = h.astype(BF16)
    for i, c in enumerate(range(0, D_MIX_IN, D_MODEL)):
        proj_ref[:, c:c + D_MODEL] = jnp.dot(hb_ref[...], win_ref[:, c:c + D_MODEL], preferred_element_type=F32)
        if between is not None:
            between(i)


def _merge_branch(i, hb_ref, win_ref, ys_ref, wbr_ref, m_ref):
    lo = D_MIX_IN + i * D_MODEL
    rows = hb_ref.shape[0]
    for r0 in range(0, rows, TAIL_ROWS):
        rs = slice(r0, min(r0 + TAIL_ROWS, rows))
        gates = jnp.dot(hb_ref[rs, :], win_ref[:, lo:lo + D_MODEL], preferred_element_type=F32)
        br = jnp.dot(ys_ref[i, rs, :], wbr_ref[i], preferred_element_type=F32)
        t = jax.nn.sigmoid(gates) * br
        m_ref[rs, :] = t if i == 0 else m_ref[rs, :] + t


def _merge_tail(x_ref, g1, sh2, sc2, n2g, m_rows, wout_ref, wr_ref, br_ref, x1_ref, h2_ref, lg_ref):
    rows = x1_ref.shape[0]
    for r0 in range(0, rows, TAIL_ROWS):
        rs = slice(r0, min(r0 + TAIL_ROWS, rows))
        per_row = lambda a: a if a.shape[0] == 1 else a[rs]
        out = jnp.dot(m_rows(rs).astype(BF16), wout_ref[...], preferred_element_type=F32)
        x1 = x_ref[rs, :] + per_row(g1) * out
        h2 = _rms(x1, n2g) * (1.0 + per_row(sc2)) + per_row(sh2)
        x1_ref[rs, :] = x1
        h2_ref[rs, :] = h2
        lg_ref[rs, :] = _bdot(h2, wr_ref[...]) + br_ref[...]


def _route_rows(lg, carry_ref, cap):
    tr = lg.shape[0]
    lane = lax.broadcasted_iota(I32, (tr, LANES), 1)
    lanef = lane.astype(F32)

    def first_argmax(vals):
        mx = jnp.max(vals, axis=-1, keepdims=True)
        return mx, jnp.min(jnp.where(vals == mx, lanef, float(LANES)), axis=-1, keepdims=True)

    neg = -jnp.inf
    gl = jnp.where(lane < N_GROUPS_MOE, lg, neg)
    gmax, gsel = first_argmax(gl)
    p_g = 1.0 / jnp.sum(jnp.exp(gl - gmax), axis=-1, keepdims=True)
    e_lo = ROUTE_LANE0 + EXPERTS_PER_GROUP * gsel
    emask = jnp.logical_and(lanef >= e_lo, lanef < e_lo + EXPERTS_PER_GROUP)
    el = jnp.where(emask, lg, neg)
    emax = jnp.max(el, axis=-1, keepdims=True)
    ex = jnp.exp(el - emax)
    prob = jnp.where(emask, ex / jnp.sum(ex, axis=-1, keepdims=True), -1.0)
    p1, i1 = first_argmax(prob)
    p2, i2 = first_argmax(jnp.where(lanef == i1, -1.0, prob))
    w1 = p_g * p1 / (p1 + p2)
    w2 = p_g * p2 / (p1 + p2)

    hot1 = lanef == i1
    hot2 = lanef == i2
    hot = jnp.logical_or(hot1, hot2).astype(F32)
    rr = lax.broadcasted_iota(I32, (tr, tr), 0)
    cc = lax.broadcasted_iota(I32, (tr, tr), 1)
    before = jnp.dot((rr > cc).astype(BF16), hot.astype(BF16), preferred_element_type=F32) + carry_ref[...]
    rank1 = jnp.sum(jnp.where(hot1, before, 0.0), axis=-1, keepdims=True)
    rank2 = jnp.sum(jnp.where(hot2, before, 0.0), axis=-1, keepdims=True)
    carry_ref[...] = carry_ref[...] + jnp.sum(hot, axis=0, keepdims=True)

    e1 = i1 - ROUTE_LANE0
    e2 = i2 - ROUTE_LANE0
    fields = (e1, e2, e1 * float(cap) + rank1, e2 * float(cap) + rank2, w1, w2)
    out = jnp.zeros((tr, LANES), F32)
    for k, f in enumerate(fields):
        out = jnp.where(lane == k, f, out)
    return out


def _slots_to_smem(route, slot_v, slot_s, sem):
    slot_v[0:2, :] = jnp.transpose(route)[ROUTE_SLOT:ROUTE_SLOT + 2, :].astype(I32)
    pltpu.make_async_copy(slot_v.at[pl.ds(0, 2)], slot_s, sem).start()


def _combine_rows(x1_prev, route_prev, g2, gbuf):
    return x1_prev + g2 * (route_prev[:, ROUTE_W:ROUTE_W + 1] * gbuf[0]
                           + route_prev[:, ROUTE_W + 1:ROUTE_W + 2] * gbuf[1])


MIXER_COL_GROUPS = tuple(range(lo * MIX_W // MXU_COLS, hi * MIX_W // MXU_COLS) for lo, hi in ((0, 2), (2, 5), (5, 6), (6, 8)))
PROJ_SLABS = max(len(g) for g in MIXER_COL_GROUPS)
PROJ_SLAB_OF = {g: g - groups[0] for groups in MIXER_COL_GROUPS for g in groups}


class _ProjCols:
    def __init__(self, ref):
        self.ref = ref

    def slab(self, col):
        return self.ref.at[PROJ_SLAB_OF[col // MXU_COLS]]

    def __getitem__(self, idx):
        rows, cols = idx
        lo, hi = cols.start, cols.stop
        parts = []
        while lo < hi:
            g = lo // MXU_COLS
            end = min(hi, (g + 1) * MXU_COLS)
            parts.append(self.slab(lo)[rows, lo - g * MXU_COLS:end - g * MXU_COLS])
            lo = end
        return parts[0] if len(parts) == 1 else jnp.concatenate(parts, axis=1)


def _tap(w8_ref, k, x):
    rows, c = x.shape
    return (x.reshape(rows // SUBLANES, SUBLANES, c) * w8_ref[k][None]).reshape(rows, c)


PROMPT_CONSTS = ("n1g", "n2g", "w_in", "a_ln_g", "a_ln_b", "a_ws", "a_bs_full", "conv_b_w8", "pool_w", "pool_scale",
                 "conv_d_w8", "conv_d_b", "d_ln_g", "d_ln_b", "w_branch", "w_out", "w_r", "b_r")


def _prompt_mixer_kernel(slots_prev, *refs, names, first, cap):
    r = dict(zip(names, refs))
    tm = r["x1"].shape[1]
    hb_ref, proj_ref, ys_ref = r["hb"], _ProjCols(r["proj"]), r["ys"]
    m_ref = r["x1"].at[0]
    cx_ext, p_ext, gl_ext, pool_tmp, conv_tmp = r["cx_ext"], r["p_ext"], r["gl_ext"], r["pool_tmp"], r["conv_tmp"]
    h2buf, slot_v, slot_s, carry, sem = r["h2buf"], r["slot_v"], r["slot_s"], r["carry"], r["sem"]
    xs_hbm = r["xs"]
    win_ref, wbr_ref = r["w_in"], r["w_branch"]
    j = pl.program_id(1)
    nj = pl.num_programs(1)
    step = pl.program_id(0) * nj + j
    nsteps = pl.num_programs(0) * nj
    trash = N_EXPERTS * cap
    sem_g, sem_d, sem_s = sem.at[0], sem.at[1], sem.at[2]

    def gather_row(tile, row):
        for k in range(2):
            p = slots_prev[(tile * 2 + k) * tm + row]
            pltpu.make_async_copy(r["ys_prev"].at[pl.ds(p, 1)], r["gbuf"].at[k, pl.ds(row, 1)], sem_g).start()

    def dispatch_row(row):
        for k in range(2):
            pltpu.make_async_copy(h2buf.at[pl.ds(row, 1)], xs_hbm.at[pl.ds(slot_s[k, row], 1)], sem_d).start()

    def wait_rows(src, dst, sem_, times):
        for _ in range(times):
            pltpu.make_async_copy(src, dst, sem_).wait()

    @pl.when(step == 0)
    def _():
        carry[...] = r["carry_in"][...]
        h2buf[...] = jnp.zeros_like(h2buf)
        slot_v[...] = (trash + tm * lax.broadcasted_iota(I32, slot_v.shape, 0)
                       + lax.broadcasted_iota(I32, slot_v.shape, 1))
        pltpu.make_async_copy(slot_v.at[pl.ds(0, 2)], slot_s, sem_s).start()
        if not first:
            lax.fori_loop(0, tm, lambda i, c: (gather_row(0, i), c)[1], 0, unroll=DMA_UNROLL)

    @pl.when(j == 0)
    def _():
        cx_ext[0:CTX_B, :] = jnp.zeros((CTX_B, MIX_W), F32)
        p_ext[0:CTX_P, :] = jnp.zeros((CTX_P, MIX_W), F32)
        gl_ext[0:CTX_D, :] = jnp.zeros((CTX_D, MIX_W), F32)

    mod = r["mod"][0]
    sh1, sc1, g1, sh2, sc2 = (mod[:, i * D_MODEL:(i + 1) * D_MODEL] for i in range(5))
    if first:
        x_src = r["x"].at[0]
    else:
        wait_rows(r["ys_prev"].at[pl.ds(0, tm)], r["gbuf"].at[0], sem_g, 2)
        x_src = r["xin"]
        x_src[...] = _combine_rows(r["x"][0], r["route_prev"][0], r["mod_prev"][0][:, 5 * D_MODEL:], r["gbuf"])
    h = _rms(x_src[...], r["n1g"][...]) * (1.0 + sc1) + sh1

    pltpu.make_async_copy(slot_v.at[pl.ds(0, 2)], slot_s, sem_s).wait()

    def proj_piece(c):
        def f():
            proj_ref.slab(c)[...] = jnp.dot(hb_ref[...], win_ref[:, c:c + MXU_COLS], preferred_element_type=F32)
        return f

    def proj_pieces(mixer):
        return [proj_piece(g * MXU_COLS) for g in MIXER_COL_GROUPS[mixer]]

    def merge_piece(i, c):
        def f():
            lo = D_MIX_IN + i * D_MODEL + c
            gates = jnp.dot(hb_ref[...], win_ref[:, lo:lo + MXU_COLS], preferred_element_type=F32)
            br = jnp.dot(ys_ref[i], wbr_ref[i, :, c:c + MXU_COLS], preferred_element_type=F32)
            t = jax.nn.sigmoid(gates) * br
            m_ref[:, c:c + MXU_COLS] = t if i == 0 else m_ref[:, c:c + MXU_COLS] + t
        return f

    def merge_pieces(i):
        return [merge_piece(i, c) for c in range(0, D_MODEL, MXU_COLS)]

    def emit(vector_pieces, matrix_pieces):
        nv, nm = len(vector_pieces), len(matrix_pieces)
        for i, vp in enumerate(vector_pieces):
            vp()
            for mp in matrix_pieces[i * nm // nv:(i + 1) * nm // nv]:
                mp()

    def dispatch_chunk(i, n):
        def f():
            for row in range(i * tm // n, (i + 1) * tm // n):
                dispatch_row(row)
        return f

    hb_ref[...] = h.astype(BF16)
    n_a = len(MIXER_COL_GROUPS[0])
    emit(proj_pieces(0), [dispatch_chunk(i, n_a) for i in range(n_a)])

    rr = lax.broadcasted_iota(I32, (CHUNK, CHUNK), 0)
    cc = lax.broadcasted_iota(I32, (CHUNK, CHUNK), 1)
    wtril = [jnp.where(rr >= cc, r["a_ws"][g], 0.0).astype(BF16) for g in range(A_GROUPS)]
    cg = MIX_W // A_GROUPS

    def a_chunk(c0):
        def f():
            u = jax.nn.gelu(proj_ref[c0:c0 + CHUNK, 0:MIX_W])
            v = _ln(jax.nn.gelu(proj_ref[c0:c0 + CHUNK, MIX_W:2 * MIX_W]), r["a_ln_g"][...], r["a_ln_b"][...])
            vb = v.astype(BF16)
            mixed = jnp.concatenate(
                [jnp.dot(wtril[g], vb[:, g * cg:(g + 1) * cg], preferred_element_type=F32)
                 for g in range(A_GROUPS)], axis=1) + r["a_bs_full"][...]
            ys_ref[0, c0:c0 + CHUNK, :] = (u * mixed).astype(BF16)
        return f

    emit([a_chunk(c0) for c0 in range(0, tm, CHUNK)], [])
    emit(proj_pieces(1), [])

    cbw_ref = r["conv_b_w8"]

    def b_prep(c0):
        def f():
            cx_ext[CTX_B + c0:CTX_B + c0 + CHUNK, :] = _conv_in(
                proj_ref[c0:c0 + CHUNK, 3 * MIX_W:4 * MIX_W] * proj_ref[c0:c0 + CHUNK, 4 * MIX_W:5 * MIX_W])
        return f

    def b_block(r0):
        def f():
            acc = None
            for k in range(CONV_B):
                t = _tap(cbw_ref, k, cx_ext[pl.ds(r0 + CTX_B - (CONV_B - 1) + k, CONV_ROWS), :])
                acc = t if acc is None else acc + t
            ys_ref[1, r0:r0 + CONV_ROWS, :] = (proj_ref[r0:r0 + CONV_ROWS, 2 * MIX_W:3 * MIX_W] * acc).astype(BF16)
        return f

    def b_end():
        tail = pl.ds(tm - (CONV_B - 1), CONV_B - 1)
        r["new_b"][0] = proj_ref[tail, 3 * MIX_W:4 * MIX_W] * proj_ref[tail, 4 * MIX_W:5 * MIX_W]
        cx_ext[0:CTX_B, :] = cx_ext[tm:tm + CTX_B, :]

    emit([b_prep(c0) for c0 in range(0, tm, CHUNK)] + [b_block(r0) for r0 in range(0, tm, CONV_ROWS)] + [b_end],
         merge_pieces(0))
    emit(proj_pieces(2), [])

    pos = j * tm + lax.broadcasted_iota(I32, (tm, 1), 0)
    pg = MIX_W // len(POOL_WINDOWS)

    def c_prep():
        p_ext[CTX_P:CTX_P + tm, :] = proj_ref[:, 5 * MIX_W:6 * MIX_W]

    def c_group(gi, w):
        def f():
            lo = gi * pg
            levels = w.bit_length() - 1
            for lv in range(levels):
                span = 1 << lv
                start = CTX_P if lv == levels - 1 else SUBLANES * (lv + 1)
                rows = CTX_P + tm - start
                if lv == 0:
                    a = p_ext[start:start + rows, lo:lo + pg] + p_ext[pl.ds(start - span, rows), lo:lo + pg]
                else:
                    prev = pool_tmp.at[gi * (gi - 1) // 2 + lv - 1]
                    a = prev[start:start + rows, :] + prev[pl.ds(start - span, rows), :]
                if lv == levels - 1:
                    s = a
                else:
                    pool_tmp[gi * (gi - 1) // 2 + lv, start:start + rows, :] = a
            cnt = jnp.minimum(pos + 1, w).astype(F32)
            pooled = s / cnt - p_ext[CTX_P:CTX_P + tm, lo:lo + pg]
            ys_ref[2, :, lo:lo + pg] = (_bdot(pooled, r["pool_w"][gi]) * r["pool_scale"][:, lo:lo + pg]).astype(BF16)
        return f

    def c_end():
        r["new_p"][0] = p_ext[pl.ds(CTX_P + tm - POOL_CTX, POOL_CTX), :]
        p_ext[0:CTX_P, :] = p_ext[tm:tm + CTX_P, :]

    emit([c_prep] + proj_pieces(3), [])
    emit([c_group(gi, w) for gi, w in enumerate(POOL_WINDOWS)] + [c_end], merge_pieces(1))

    cdw_ref = r["conv_d_w8"]
    off0 = CTX_D - (CONV_D - 1)
    next_tile = jnp.minimum(step + 1, nsteps - 1)

    def d_prep(c0):
        def f():
            gl_ext[CTX_D + c0:CTX_D + c0 + CHUNK, :] = _conv_in(
                proj_ref[c0:c0 + CHUNK, 6 * MIX_W:7 * MIX_W]
                * jax.nn.sigmoid(proj_ref[c0:c0 + CHUNK, 7 * MIX_W:8 * MIX_W]))
        return f

    def d_block(r0):
        def f():
            acc = None
            for rr_ in range(SUBLANES):
                rows = CONV_ROWS if rr_ == 0 else CONV_ROWS + SUBLANES
                y = None
                for q in range(-(-(off0 + CONV_D) // SUBLANES)):
                    k = SUBLANES * q + rr_ - off0
                    if 0 <= k < CONV_D:
                        t = _tap(cdw_ref, k, gl_ext[r0 + SUBLANES * q:r0 + SUBLANES * q + rows, :])
                        y = t if y is None else y + t
                if rr_ > 0:
                    conv_tmp[rr_ - 1] = y
                    y = conv_tmp[rr_ - 1, pl.ds(rr_, CONV_ROWS), :]
                acc = y if acc is None else acc + y
            dc = acc + r["conv_d_b"][...]
            ys_ref[3, r0:r0 + CONV_ROWS, :] = jax.nn.silu(_ln(dc, r["d_ln_g"][...], r["d_ln_b"][...])).astype(BF16)
            if not first:
                for row in range(r0, r0 + CONV_ROWS):
                    gather_row(next_tile, row)
        return f

    def d_end():
        tail = pl.ds(tm - (CONV_D - 1), CONV_D - 1)
        r["new_d"][0] = proj_ref[tail, 6 * MIX_W:7 * MIX_W] * jax.nn.sigmoid(proj_ref[tail, 7 * MIX_W:8 * MIX_W])
        gl_ext[0:CTX_D, :] = gl_ext[tm:tm + CTX_D, :]

    emit([d_prep(c0) for c0 in range(0, tm, CHUNK)] + [d_block(r0) for r0 in range(0, tm, CONV_ROWS)] + [d_end],
         merge_pieces(2))
    for mp in merge_pieces(3):
        mp()

    wait_rows(h2buf, xs_hbm.at[pl.ds(0, tm)], sem_d, 2)
    _merge_tail(x_src, g1, sh2, sc2, r["n2g"][...], lambda rs: m_ref[rs, :], r["w_out"], r["w_r"], r["b_r"],
                r["x1"].at[0], h2buf, r["lg"])
    route = _route_rows(r["lg"][...], carry, cap)
    r["route"][0] = route
    r["carry_out"][...] = carry[...]
    _slots_to_smem(route, slot_v, slot_s, sem_s)
    r["slots"][...] = slot_v[...]

    @pl.when(step == nsteps - 1)
    def _():
        pltpu.make_async_copy(slot_v.at[pl.ds(0, 2)], slot_s, sem_s).wait()
        lax.fori_loop(0, tm, lambda i, c: (dispatch_row(i), c)[1], 0, unroll=DMA_UNROLL)
        wait_rows(h2buf, xs_hbm.at[pl.ds(0, tm)], sem_d, 2)
        if not first:
            wait_rows(r["ys_prev"].at[pl.ds(0, tm)], r["gbuf"].at[0], sem_g, 2)


def _const_spec(shape):
    nd = len(shape)
    return pl.BlockSpec(shape, lambda *_: (0,) * nd, pipeline_mode=pl.Buffered(1))


def _prompt_mixer(x, mod, lw, cap, prev=None):
    n, l, _ = x.shape
    tm = MIX_TM
    first = prev is None
    tok = lambda last: pl.BlockSpec((1, tm, last), lambda b, j, sp: (b, j, 0))
    per_seq = lambda rows, last: pl.BlockSpec((1, rows, last), lambda b, j, sp: (b, 0, 0))
    anyspec = pl.BlockSpec(memory_space=pl.ANY)
    names, ins, in_specs = ["x", "mod", "carry_in"], [x, mod, jnp.zeros((1, LANES), F32)], \
        [tok(D_MODEL), per_seq(1, 6 * D_MODEL), _const_spec((1, LANES))]
    if first:
        slots_prev = jnp.zeros((SUBLANES,), I32)
    else:
        route_prev, mod_prev, ys_prev, slots_prev = prev
        names += ["route_prev", "mod_prev", "ys_prev"]
        ins += [route_prev, mod_prev, ys_prev]
        in_specs += [tok(LANES), per_seq(1, 6 * D_MODEL), anyspec]
    names += list(PROMPT_CONSTS)
    ins += [lw[k] for k in PROMPT_CONSTS]
    in_specs += [_const_spec(lw[k].shape) for k in PROMPT_CONSTS]

    xs_rows = N_EXPERTS * cap + 2 * tm
    out_names = ["x1", "route", "carry_out", "xs", "slots", "new_b", "new_p", "new_d"]
    out_specs = [tok(D_MODEL), tok(LANES), pl.BlockSpec((1, LANES), lambda b, j, sp: (0, 0)), anyspec,
                 pl.BlockSpec((SUBLANES, tm), lambda b, j, sp: (b * (l // tm) + j, 0)),
                 per_seq(CONV_B - 1, MIX_W), per_seq(POOL_CTX, MIX_W), per_seq(CONV_D - 1, MIX_W)]
    out_shape = [jax.ShapeDtypeStruct((n, l, D_MODEL), F32), jax.ShapeDtypeStruct((n, l, LANES), F32),
                 jax.ShapeDtypeStruct((1, LANES), F32), jax.ShapeDtypeStruct((xs_rows, D_MODEL), F32),
                 jax.ShapeDtypeStruct((n * (l // tm) * SUBLANES, tm), I32),
                 jax.ShapeDtypeStruct((n, CONV_B - 1, MIX_W), F32), jax.ShapeDtypeStruct((n, POOL_CTX, MIX_W), F32),
                 jax.ShapeDtypeStruct((n, CONV_D - 1, MIX_W), F32)]
    scratch = {"hb": pltpu.VMEM((tm, D_MODEL), BF16), "proj": pltpu.VMEM((PROJ_SLABS, tm, MXU_COLS), F32),
               "ys": pltpu.VMEM((N_BRANCH, tm, MIX_W), BF16),
               "cx_ext": pltpu.VMEM((CTX_B + tm, MIX_W), F32), "p_ext": pltpu.VMEM((CTX_P + tm, MIX_W), F32),
               "gl_ext": pltpu.VMEM((CTX_D + tm, MIX_W), F32),
               "pool_tmp": pltpu.VMEM((POOL_TMP_SLOTS, CTX_P + tm, MIX_W // len(POOL_WINDOWS)), F32),
               "conv_tmp": pltpu.VMEM((SUBLANES - 1, CONV_ROWS + SUBLANES, MIX_W), F32),
               "h2buf": pltpu.VMEM((tm, D_MODEL), F32), "lg": pltpu.VMEM((tm, LANES), F32),
               "slot_v": pltpu.VMEM((SUBLANES, tm), I32), "slot_s": pltpu.SMEM((2, tm), I32),
               "carry": pltpu.VMEM((1, LANES), F32), "sem": pltpu.SemaphoreType.DMA((3,))}
    if not first:
        scratch["xin"] = pltpu.VMEM((tm, D_MODEL), F32)
        scratch["gbuf"] = pltpu.VMEM((2, tm, D_MODEL), F32)
    names = tuple(names + out_names + list(scratch))
    outs = pl.pallas_call(
        functools.partial(_prompt_mixer_kernel, names=names, first=first, cap=cap),
        grid_spec=pltpu.PrefetchScalarGridSpec(
            num_scalar_prefetch=1, grid=(n, l // tm), in_specs=in_specs, out_specs=out_specs,
            scratch_shapes=list(scratch.values())),
        out_shape=out_shape,
        compiler_params=pltpu.CompilerParams(dimension_semantics=("arbitrary", "arbitrary"),
                                             vmem_limit_bytes=MIXER_VMEM_BYTES, disable_bounds_checks=True),
        name="prompt_mixer",
    )(slots_prev, *ins)
    return dict(zip(out_names, outs))


SAMPLE_CONSTS = ("n1g", "n2g", "w_in", "a_ln_g", "a_ln_b", "a_w0", "a_b0", "conv_b_w", "pool_w", "pool_scale",
                 "conv_d_w", "conv_d_b", "d_ln_g", "d_ln_b", "w_branch", "w_out", "w_r", "b_r")


def _sample_mixer_kernel(slots_prev, *refs, names, first, cap):
    r = dict(zip(names, refs))
    rows = r["x1"].shape[0]
    hb_ref, proj_ref, ys_ref, m_ref = r["hb"], r["proj"], r["ys"], r["m"]
    h2buf, slot_v, slot_s, carry, sem = r["h2buf"], r["slot_v"], r["slot_s"], r["carry"], r["sem"]
    xs_hbm = r["xs"]
    win_ref, wbr_ref = r["w_in"], r["w_branch"]
    cbw_ref, cdw_ref, stb_ref, stp_ref, std_ref = r["conv_b_w"], r["conv_d_w"], r["st_b"], r["st_p"], r["st_d"]
    sem_g, sem_d, sem_s = sem.at[0], sem.at[1], sem.at[2]

    mod = r["mod"][...]
    sh1, sc1, g1, sh2, sc2 = (mod[:, i * D_MODEL:(i + 1) * D_MODEL] for i in range(5))
    if first:
        x_src = r["x"]
    else:
        def gather_row(i, c):
            for k in range(2):
                p = slots_prev[k * rows + i]
                pltpu.make_async_copy(r["ys_prev"].at[pl.ds(p, 1)], r["gbuf"].at[k, pl.ds(i, 1)], sem_g).start()
            return c
        lax.fori_loop(0, rows, gather_row, 0, unroll=DMA_UNROLL)
        for _ in range(2):
            pltpu.make_async_copy(r["ys_prev"].at[pl.ds(0, rows)], r["gbuf"].at[0], sem_g).wait()
        x_src = r["xin"]
        x_src[...] = _combine_rows(r["x"][...], r["route_prev"][...], r["mod_prev"][...][:, 5 * D_MODEL:], r["gbuf"])
    h = _rms(x_src[...], r["n1g"][...]) * (1.0 + sc1) + sh1
    _project(h, win_ref, hb_ref, proj_ref)

    u = jax.nn.gelu(proj_ref[:, 0:MIX_W])
    v = _ln(jax.nn.gelu(proj_ref[:, MIX_W:2 * MIX_W]), r["a_ln_g"][...], r["a_ln_b"][...])
    r["v"][...] = v
    ys_ref[0] = (u * (r["a_w0"][...] * v + r["a_b0"][...])).astype(BF16)

    cx = proj_ref[:, 3 * MIX_W:4 * MIX_W] * proj_ref[:, 4 * MIX_W:5 * MIX_W]
    r["cx"][...] = cx
    acc = cbw_ref[CONV_B - 1:CONV_B, :] * _conv_in(cx)
    for k in range(CONV_B - 1):
        acc = acc + cbw_ref[k:k + 1, :] * _conv_in(stb_ref[:, k * MIX_W:(k + 1) * MIX_W])
    ys_ref[1] = (proj_ref[:, 2 * MIX_W:3 * MIX_W] * acc).astype(BF16)

    pin = proj_ref[:, 5 * MIX_W:6 * MIX_W]
    r["pin"][...] = pin
    pg = MIX_W // len(POOL_WINDOWS)
    mixed = []
    for gi, w in enumerate(POOL_WINDOWS):
        lo = gi * pg
        s = pin[:, lo:lo + pg]
        for i in range(1, w):
            row = POOL_CTX - i
            s = s + stp_ref[:, row * MIX_W + lo:row * MIX_W + lo + pg]
        pooled = s / float(min(PAST_LEN + 1, w)) - pin[:, lo:lo + pg]
        mixed.append(_bdot(pooled, r["pool_w"][gi]))
    ys_ref[2] = (jnp.concatenate(mixed, axis=1) * r["pool_scale"][...]).astype(BF16)

    gl = proj_ref[:, 6 * MIX_W:7 * MIX_W] * jax.nn.sigmoid(proj_ref[:, 7 * MIX_W:8 * MIX_W])
    r["gl"][...] = gl
    acc = cdw_ref[CONV_D - 1:CONV_D, :] * _conv_in(gl)
    for k in range(CONV_D - 1):
        acc = acc + cdw_ref[k:k + 1, :] * _conv_in(std_ref[:, k * MIX_W:(k + 1) * MIX_W])
    dc = acc + r["conv_d_b"][...]
    ys_ref[3] = jax.nn.silu(_ln(dc, r["d_ln_g"][...], r["d_ln_b"][...])).astype(BF16)

    for i in range(N_BRANCH):
        _merge_branch(i, hb_ref, win_ref, ys_ref, wbr_ref, m_ref)
    _merge_tail(x_src, g1, sh2, sc2, r["n2g"][...], lambda rs: m_ref[rs, :], r["w_out"], r["w_r"], r["b_r"],
                r["x1"], h2buf, r["lg"])

    carry[...] = r["carry_in"][...]
    route = _route_rows(r["lg"][...], carry, cap)
    r["route"][...] = route
    r["carry_out"][...] = carry[...]
    slot_v[...] = jnp.zeros_like(slot_v)
    _slots_to_smem(route, slot_v, slot_s, sem_s)
    r["slots"][...] = slot_v[...]
    pltpu.make_async_copy(slot_v.at[pl.ds(0, 2)], slot_s, sem_s).wait()

    def dispatch_row(i, c):
        for k in range(2):
            pltpu.make_async_copy(h2buf.at[pl.ds(i, 1)], xs_hbm.at[pl.ds(slot_s[k, i], 1)], sem_d).start()
        return c
    lax.fori_loop(0, rows, dispatch_row, 0, unroll=DMA_UNROLL)
    for _ in range(2):
        pltpu.make_async_copy(h2buf, xs_hbm.at[pl.ds(0, rows)], sem_d).wait()


def _sample_mixer(x, mod, st_b, st_p, st_d, lw, cap, carry_in, xs, prev=None):
    rows = x.shape[0]
    first = prev is None
    full = lambda a: pl.BlockSpec(a.shape, lambda i, sp: (0,) * a.ndim)
    anyspec = pl.BlockSpec(memory_space=pl.ANY)
    names, ins = ["x", "mod", "carry_in", "xs_in", "st_b", "st_p", "st_d"], [x, mod, carry_in, xs, st_b, st_p, st_d]
    in_specs = [full(x), full(mod), full(carry_in), anyspec, full(st_b), full(st_p), full(st_d)]
    if first:
        slots_prev = jnp.zeros((SUBLANES,), I32)
    else:
        route_prev, mod_prev, ys_prev, slots_prev = prev
        names += ["route_prev", "mod_prev", "ys_prev"]
        ins += [route_prev, mod_prev, ys_prev]
        in_specs += [full(route_prev), full(mod_prev), anyspec]
    names += list(SAMPLE_CONSTS)
    ins += [lw[k] for k in SAMPLE_CONSTS]
    in_specs += [full(lw[k]) for k in SAMPLE_CONSTS]

    mk = lambda last: jax.ShapeDtypeStruct((rows, last), F32)
    out_names = ["x1", "route", "carry_out", "xs", "slots", "cx", "pin", "gl", "v"]
    out_shape = [mk(D_MODEL), mk(LANES), jax.ShapeDtypeStruct((1, LANES), F32),
                 jax.ShapeDtypeStruct(xs.shape, xs.dtype), jax.ShapeDtypeStruct((SUBLANES, rows), I32),
                 mk(MIX_W), mk(MIX_W), mk(MIX_W), mk(MIX_W)]
    out_specs = [pl.BlockSpec(s.shape, lambda i, sp, nd=len(s.shape): (0,) * nd) for s in out_shape]
    out_specs[3] = anyspec
    scratch = {"hb": pltpu.VMEM((rows, D_MODEL), BF16), "proj": pltpu.VMEM((rows, D_MIX_IN), F32),
               "ys": pltpu.VMEM((N_BRANCH, rows, MIX_W), BF16), "m": pltpu.VMEM((rows, D_MODEL), F32),
               "h2buf": pltpu.VMEM((rows, D_MODEL), F32), "lg": pltpu.VMEM((rows, LANES), F32),
               "slot_v": pltpu.VMEM((SUBLANES, rows), I32), "slot_s": pltpu.SMEM((2, rows), I32),
               "carry": pltpu.VMEM((1, LANES), F32), "sem": pltpu.SemaphoreType.DMA((3,))}
    if not first:
        scratch["xin"] = pltpu.VMEM((rows, D_MODEL), F32)
        scratch["gbuf"] = pltpu.VMEM((2, rows, D_MODEL), F32)
    names = tuple(names + out_names + list(scratch))
    outs = pl.pallas_call(
        functools.partial(_sample_mixer_kernel, names=names, first=first, cap=cap),
        grid_spec=pltpu.PrefetchScalarGridSpec(
            num_scalar_prefetch=1, grid=(1,), in_specs=in_specs, out_specs=out_specs,
            scratch_shapes=list(scratch.values())),
        out_shape=out_shape,
        input_output_aliases={1 + names.index("xs_in"): out_names.index("xs")},
        compiler_params=pltpu.CompilerParams(dimension_semantics=("arbitrary",), vmem_limit_bytes=MIXER_VMEM_BYTES,
                                             disable_bounds_checks=True),
        name="sample_mixer",
    )(slots_prev, *ins)
    return dict(zip(out_names, outs))


def _expert_kernel(vb_ref, be_ref, nv_ref, nu_ref, xs_ref, wg_ref, wu_ref, wd_ref, ys_ref, wgb, wub, wdb):
    b = pl.program_id(0)

    @pl.when(b < nu_ref[0])
    def _():
        @pl.when(jnp.logical_or(b == 0, be_ref[b] != be_ref[jnp.maximum(b - 1, 0)]))
        def _():
            wgb[...] = wg_ref[0, 0].astype(BF16)
            wub[...] = wu_ref[0, 0].astype(BF16)
            wdb[...] = wd_ref[0, 0].astype(BF16)

        valid = lax.broadcasted_iota(I32, (xs_ref.shape[0], 1), 0) < nv_ref[b]
        xb = jnp.where(valid, xs_ref[...], 0.0).astype(BF16)
        g = jnp.dot(xb, wgb[...], preferred_element_type=F32)
        u = jnp.dot(xb, wub[...], preferred_element_type=F32)
        ys_ref[...] = jnp.dot((jax.nn.silu(g) * u).astype(BF16), wdb[...], preferred_element_type=F32)


def _experts(xs, counts, cap, li, w_eg, w_eu, w_ed):
    bm = MOE_BM
    n_blocks = -(-xs.shape[0] // bm)
    max_used = min(N_EXPERTS * (cap // bm), -(-(cap * 2) // bm) + N_EXPERTS)
    nblk = (counts + bm - 1) // bm
    ends = jnp.cumsum(nblk)
    n_used = ends[-1:]
    b = jnp.arange(max_used, dtype=I32)
    bc = jnp.minimum(b, n_used[0] - 1)
    be = jnp.minimum(jnp.sum(ends[None, :] <= bc[:, None], axis=1), N_EXPERTS - 1).astype(I32)
    local = bc - (ends - nblk)[be]
    vblock = (be * (cap // bm) + local).astype(I32)
    nvalid = jnp.clip(counts[be] - local * bm, 0, bm).astype(I32)
    wspec = lambda shape: pl.BlockSpec((1, 1) + shape, lambda i, vb, be_, nv, nu: (li, be_[i], 0, 0))
    return pl.pallas_call(
        _expert_kernel,
        grid_spec=pltpu.PrefetchScalarGridSpec(
            num_scalar_prefetch=4,
            grid=(max_used,),
            in_specs=[pl.BlockSpec((bm, D_MODEL), lambda i, vb, be_, nv, nu: (vb[i], 0)),
                      wspec((D_MODEL, D_EXPERT)), wspec((D_MODEL, D_EXPERT)), wspec((D_EXPERT, D_MODEL))],
            out_specs=pl.BlockSpec((bm, D_MODEL), lambda i, vb, be_, nv, nu: (vb[i], 0)),
            scratch_shapes=[pltpu.VMEM((D_MODEL, D_EXPERT), BF16), pltpu.VMEM((D_MODEL, D_EXPERT), BF16),
                            pltpu.VMEM((D_EXPERT, D_MODEL), BF16)]),
        out_shape=jax.ShapeDtypeStruct((n_blocks * bm, D_MODEL), F32),
        compiler_params=pltpu.CompilerParams(dimension_semantics=("arbitrary",)),
        name="experts",
    )(vblock, be, nvalid, n_used.astype(I32), xs, w_eg, w_eu, w_ed)


def _combine_kernel(pos_ref, ys_hbm, x1_ref, route_ref, g2_ref, fg_ref, out_ref, buf, sem):
    tc = x1_ref.shape[0]
    i = pl.program_id(0)
    n = pl.num_programs(0)
    slot = i % 2

    def issue(tile, s):
        def body(row, carry):
            for k in range(2):
                p = pos_ref[(tile * 2 + k) * tc + row]
                pltpu.make_async_copy(ys_hbm.at[pl.ds(p, 1)], buf.at[s, k, pl.ds(row, 1)], sem.at[s]).start()
            return carry
        lax.fori_loop(0, tc, body, 0, unroll=DMA_UNROLL)

    @pl.when(i == 0)
    def _():
        issue(0, 0)

    @pl.when(i + 1 < n)
    def _():
        issue(i + 1, 1 - slot)

    for k in range(2):
        pltpu.make_async_copy(ys_hbm.at[pl.ds(0, tc)], buf.at[slot, k], sem.at[slot]).wait()
    x2 = _combine_rows(x1_ref[...], route_ref[...], g2_ref[...].reshape(-1, D_MODEL), buf.at[slot])
    out_ref[...] = _rms(x2, fg_ref[...])


def _combine(ys, pos, tc, x1, route, g2, fg):
    t = x1.shape[0]
    if g2.ndim == 3:
        seq = t // g2.shape[0]
        g2spec = pl.BlockSpec((1, 1, D_MODEL), lambda i, pos: (i * tc // seq, 0, 0))
    else:
        g2spec = pl.BlockSpec((tc, D_MODEL), lambda i, pos: (i, 0))
    tokspec = lambda last: pl.BlockSpec((tc, last), lambda i, pos: (i, 0))
    return pl.pallas_call(
        _combine_kernel,
        grid_spec=pltpu.PrefetchScalarGridSpec(
            num_scalar_prefetch=1,
            grid=(t // tc,),
            in_specs=[pl.BlockSpec(memory_space=pl.ANY), tokspec(D_MODEL), tokspec(LANES), g2spec,
                      pl.BlockSpec((1, D_MODEL), lambda i, pos: (0, 0))],
            out_specs=tokspec(D_MODEL),
            scratch_shapes=[pltpu.VMEM((2, 2, tc, D_MODEL), F32), pltpu.SemaphoreType.DMA((2,))]),
        out_shape=jax.ShapeDtypeStruct((t, D_MODEL), F32),
        compiler_params=pltpu.CompilerParams(dimension_semantics=("arbitrary",), disable_bounds_checks=True,
                                             vmem_limit_bytes=MIXER_VMEM_BYTES),
        name="combine",
    )(pos, ys, x1, route, g2, fg)


def _layer_weights(li, norm1_g, norm2_g, w_in, a_ln_g, a_ln_b, a_ws, a_bs, conv_b_w, pool_w, pool_scale,
                   conv_d_w, conv_d_b, d_ln_g, d_ln_b, w_branch, w_out, w_rg, b_rg, w_re, b_re):
    row = lambda a: a[li].reshape(1, -1)
    cg = MIX_W // A_GROUPS
    pad = LANES - N_GROUPS_MOE - N_EXPERTS
    return {
        "n1g": row(norm1_g), "n2g": row(norm2_g), "w_in": w_in[li].astype(BF16),
        "a_ln_g": row(a_ln_g), "a_ln_b": row(a_ln_b), "a_ws": a_ws[li],
        "a_bs_full": jnp.repeat(a_bs[li].T, cg, axis=1),
        "a_w0": jnp.repeat(a_ws[li, :, 0, 0], cg).reshape(1, MIX_W),
        "a_b0": jnp.repeat(a_bs[li, :, 0], cg).reshape(1, MIX_W),
        "conv_b_w": conv_b_w[li], "pool_w": pool_w[li], "pool_scale": row(pool_scale),
        "conv_b_w8": jnp.broadcast_to(conv_b_w[li][:, None, :], (CONV_B, SUBLANES, MIX_W)),
        "conv_d_w8": jnp.broadcast_to(conv_d_w[li][:, None, :], (CONV_D, SUBLANES, MIX_W)),
        "conv_d_w": conv_d_w[li], "conv_d_b": row(conv_d_b), "d_ln_g": row(d_ln_g), "d_ln_b": row(d_ln_b),
        "w_branch": w_branch[li].astype(BF16), "w_out": w_out[li].astype(BF16),
        "w_r": jnp.concatenate([w_rg[li], w_re[li], jnp.zeros((D_MODEL, pad), F32)], axis=1),
        "b_r": jnp.concatenate([b_rg[li], b_re[li], jnp.zeros((pad,), F32)]).reshape(1, LANES),
    }


def _slots(slot_rows):
    return slot_rows.reshape(-1, SUBLANES, slot_rows.shape[-1])[:, :2].reshape(-1)


def kernel(x_prompt, x_sample, state_conv_b, state_pool, state_conv_d, c_prompt, c_sample, w_mod, b_mod, norm1_g, norm2_g, w_in, a_ln_g, a_ln_b, a_ws, a_bs, conv_b_w, pool_w, pool_scale, conv_d_w, conv_d_b, d_ln_g, d_ln_b, w_branch, w_out, w_rg, b_rg, w_re, b_re, w_eg, w_eu, w_ed, final_g):
    depth = w_mod.shape[0]
    nb, seq, _ = x_prompt.shape
    ns = x_sample.shape[0]
    t = nb * seq
    cap = -(-(t + ns) // MOE_BM) * MOE_BM
    mod_all = _modulation(jnp.concatenate([c_prompt, c_sample], axis=0), w_mod, b_mod)
    fg = final_g.reshape(1, D_MODEL)

    xp = x_prompt
    xs_tok = x_sample.reshape(ns, D_MODEL)
    prev_p = prev_s = None
    outs = {k: [] for k in ("bp", "pp", "dp", "bs", "ps", "ds", "vs")}
    for li in range(depth):
        lw = _layer_weights(li, norm1_g, norm2_g, w_in, a_ln_g, a_ln_b, a_ws, a_bs, conv_b_w, pool_w, pool_scale,
                            conv_d_w, conv_d_b, d_ln_g, d_ln_b, w_branch, w_out, w_rg, b_rg, w_re, b_re)
        mod_p = mod_all[li, :nb].reshape(nb, 1, 6 * D_MODEL)
        mod_s = mod_all[li, nb:]

        pm = _prompt_mixer(xp, mod_p, lw, cap, prev_p)
        outs["bp"].append(pm["new_b"])
        outs["pp"].append(pm["new_p"])
        outs["dp"].append(pm["new_d"])

        st_b, st_p, st_d = state_conv_b[li], state_pool[li], state_conv_d[li]
        sm = _sample_mixer(xs_tok, mod_s, st_b.reshape(ns, -1), st_p.reshape(ns, -1), st_d.reshape(ns, -1), lw, cap,
                           pm["carry_out"], pm["xs"], prev_s)
        outs["bs"].append(jnp.concatenate([st_b[:, 1:], sm["cx"][:, None]], axis=1))
        outs["ps"].append(jnp.concatenate([st_p[:, 1:], sm["pin"][:, None]], axis=1))
        outs["ds"].append(jnp.concatenate([st_d[:, 1:], sm["gl"][:, None]], axis=1))
        outs["vs"].append(sm["v"][:, None])

        counts = sm["carry_out"][0, ROUTE_LANE0:ROUTE_LANE0 + N_EXPERTS].astype(I32)
        ys = _experts(sm["xs"], counts, cap, li, w_eg, w_eu, w_ed)
        route_p = pm["route"]
        prev_p = (route_p, mod_p, ys, _slots(pm["slots"]))
        prev_s = (sm["route"], mod_s, ys, _slots(sm["slots"]))
        xp, xs_tok = pm["x1"], sm["x1"]

    route_p, mod_p, ys, slots_p = prev_p
    route_s, mod_s, _, slots_s = prev_s
    y_p = _combine(ys, slots_p, MIX_TM, xp.reshape(t, D_MODEL), route_p.reshape(t, LANES), mod_p[:, :, 5 * D_MODEL:], fg)
    y_s = _combine(ys, slots_s, ns, xs_tok, route_s, mod_s[:, 5 * D_MODEL:], fg)

    stack = lambda k: jnp.stack(outs[k])
    return (y_p.reshape(nb, seq, D_MODEL), y_s.reshape(ns, 1, D_MODEL), stack("bp"), stack("pp"), stack("dp"),
            stack("bs"), stack("ps"), stack("ds"), stack("vs"))
```

```python
import functools

import jax
import jax.numpy as jnp
from jax import lax
from jax.experimental import pallas as pl
from jax.experimental.pallas import tpu as pltpu

F32 = jnp.float32
BF16 = jnp.bfloat16
I32 = jnp.int32

D_MODEL = 1024
MIX_W = D_MODEL // 2
CHUNK = 128
A_GROUPS = 4
CONV_B = 3
POOL_WINDOWS = (2, 4, 8, 16)
POOL_CTX = 15
CONV_D = 31
N_BRANCH = 4
N_GROUPS_MOE = 4
EXPERTS_PER_GROUP = 8
N_EXPERTS = N_GROUPS_MOE * EXPERTS_PER_GROUP
D_EXPERT = D_MODEL // 4
EPS = 1e-6
D_MIX_IN = 8 * MIX_W
D_IN = D_MIX_IN + N_BRANCH * D_MODEL
PAST_LEN = 16384

LANES = 128
SUBLANES = 8
ROUTE_LANE0 = N_GROUPS_MOE
ROUTE_E, ROUTE_SLOT, ROUTE_W = 0, 2, 4

MIX_TM = 512
CONV_ROWS = 32
TAIL_ROWS = 256
MXU_COLS = 256
CTX_B = 8
CTX_P = 32
CTX_D = 32
POOL_TMP_SLOTS = sum(w.bit_length() - 2 for w in POOL_WINDOWS)
MOE_BM = 256
DMA_UNROLL = 8
MIXER_VMEM_BYTES = 56 * 1024 * 1024


def _rms(x, g):
    return x * lax.rsqrt(jnp.mean(x * x, axis=-1, keepdims=True) + EPS) * g


def _ln(x, g, b):
    mu = jnp.mean(x, axis=-1, keepdims=True)
    xc = x - mu
    return xc * lax.rsqrt(jnp.mean(xc * xc, axis=-1, keepdims=True) + EPS) * g + b


def _conv_in(x):
    return x.astype(BF16).astype(F32)


def _bdot(a, b):
    return jnp.dot(a.astype(BF16), b.astype(BF16), preferred_element_type=F32)


def _mod_kernel(c_ref, w_ref, b_ref, o_ref):
    o_ref[0] = _bdot(c_ref[...], w_ref[0]) + b_ref[0]


def _modulation(c_all, w_mod, b_mod):
    depth = w_mod.shape[0]
    rows = c_all.shape[0]
    tn = 1536
    return pl.pallas_call(
        _mod_kernel,
        grid=(depth, 6 * D_MODEL // tn),
        in_specs=[pl.BlockSpec((rows, D_MODEL), lambda l, n: (0, 0)),
                  pl.BlockSpec((1, D_MODEL, tn), lambda l, n: (l, 0, n)),
                  pl.BlockSpec((1, 1, tn), lambda l, n: (l, 0, n))],
        out_specs=pl.BlockSpec((1, rows, tn), lambda l, n: (l, 0, n)),
        out_shape=jax.ShapeDtypeStruct((depth, rows, 6 * D_MODEL), F32),
        name="modulation",
    )(c_all, w_mod, b_mod.reshape(depth, 1, 6 * D_MODEL))


def _project(h, win_ref, hb_ref, proj_ref, between=None):
    hb_ref[...] = h.astype(BF16)
    for i, c in enumerate(range(0, D_MIX_IN, D_MODEL)):
        proj_ref[:, c:c + D_MODEL] = jnp.dot(hb_ref[...], win_ref[:, c:c + D_MODEL], preferred_element_type=F32)
        if between is not None:
            between(i)


def _merge_branch(i, hb_ref, win_ref, ys_ref, wbr_ref, m_ref):
    lo = D_MIX_IN + i * D_MODEL
    rows = hb_ref.shape[0]
    for r0 in range(0, rows, TAIL_ROWS):
        rs = slice(r0, min(r0 + TAIL_ROWS, rows))
        gates = jnp.dot(hb_ref[rs, :], win_ref[:, lo:lo + D_MODEL], preferred_element_type=F32)
        br = jnp.dot(ys_ref[i, rs, :], wbr_ref[i], preferred_element_type=F32)
        t = jax.nn.sigmoid(gates) * br
        m_ref[rs, :] = t if i == 0 else m_ref[rs, :] + t


def _merge_tail(x_ref, g1, sh2, sc2, n2g, m_rows, wout_ref, wr_ref, br_ref, x1_ref, h2_ref, lg_ref):
    rows = x1_ref.shape[0]
    for r0 in range(0, rows, TAIL_ROWS):
        rs = slice(r0, min(r0 + TAIL_ROWS, rows))
        per_row = lambda a: a if a.shape[0] == 1 else a[rs]
        out = jnp.dot(m_rows(rs).astype(BF16), wout_ref[...], preferred_element_type=F32)
        x1 = x_ref[rs, :] + per_row(g1) * out
        h2 = _rms(x1, n2g) * (1.0 + per_row(sc2)) + per_row(sh2)
        x1_ref[rs, :] = x1
        h2_ref[rs, :] = h2
        lg_ref[rs, :] = _bdot(h2, wr_ref[...]) + br_ref[...]


def _route_rows(lg, carry_ref, cap):
    tr = lg.shape[0]
    lane = lax.broadcasted_iota(I32, (tr, LANES), 1)
    lanef = lane.astype(F32)

    def first_argmax(vals):
        mx = jnp.max(vals, axis=-1, keepdims=True)
        return mx, jnp.min(jnp.where(vals == mx, lanef, float(LANES)), axis=-1, keepdims=True)

    neg = -jnp.inf
    gl = jnp.where(lane < N_GROUPS_MOE, lg, neg)
    gmax, gsel = first_argmax(gl)
    p_g = 1.0 / jnp.sum(jnp.exp(gl - gmax), axis=-1, keepdims=True)
    e_lo = ROUTE_LANE0 + EXPERTS_PER_GROUP * gsel
    emask = jnp.logical_and(lanef >= e_lo, lanef < e_lo + EXPERTS_PER_GROUP)
    el = jnp.where(emask, lg, neg)
    emax = jnp.max(el, axis=-1, keepdims=True)
    ex = jnp.exp(el - emax)
    prob = jnp.where(emask, ex / jnp.sum(ex, axis=-1, keepdims=True), -1.0)
    p1, i1 = first_argmax(prob)
    p2, i2 = first_argmax(jnp.where(lanef == i1, -1.0, prob))
    w1 = p_g * p1 / (p1 + p2)
    w2 = p_g * p2 / (p1 + p2)

    hot1 = lanef == i1
    hot2 = lanef == i2
    hot = jnp.logical_or(hot1, hot2).astype(F32)
    rr = lax.broadcasted_iota(I32, (tr, tr), 0)
    cc = lax.broadcasted_iota(I32, (tr, tr), 1)
    before = jnp.dot((rr > cc).astype(BF16), hot.astype(BF16), preferred_element_type=F32) + carry_ref[...]
    rank1 = jnp.sum(jnp.where(hot1, before, 0.0), axis=-1, keepdims=True)
    rank2 = jnp.sum(jnp.where(hot2, before, 0.0), axis=-1, keepdims=True)
    carry_ref[...] = carry_ref[...] + jnp.sum(hot, axis=0, keepdims=True)

    e1 = i1 - ROUTE_LANE0
    e2 = i2 - ROUTE_LANE0
    fields = (e1, e2, e1 * float(cap) + rank1, e2 * float(cap) + rank2, w1, w2)
    out = jnp.zeros((tr, LANES), F32)
    for k, f in enumerate(fields):
        out = jnp.where(lane == k, f, out)
    return out


def _slots_to_smem(route, slot_v, slot_s, sem):
    slot_v[0:2, :] = jnp.transpose(route)[ROUTE_SLOT:ROUTE_SLOT + 2, :].astype(I32)
    pltpu.make_async_copy(slot_v.at[pl.ds(0, 2)], slot_s, sem).start()


def _combine_rows(x1_prev, route_prev, g2, gbuf):
    return x1_prev + g2 * (route_prev[:, ROUTE_W:ROUTE_W + 1] * gbuf[0]
                           + route_prev[:, ROUTE_W + 1:ROUTE_W + 2] * gbuf[1])


MIXER_COL_GROUPS = tuple(range(lo * MIX_W // MXU_COLS, hi * MIX_W // MXU_COLS) for lo, hi in ((0, 2), (2, 5), (5, 6), (6, 8)))
PROJ_SLABS = max(len(g) for g in MIXER_COL_GROUPS)
PROJ_SLAB_OF = {g: g - groups[0] for groups in MIXER_COL_GROUPS for g in groups}


class _ProjCols:
    def __init__(self, ref):
        self.ref = ref

    def slab(self, col):
        return self.ref.at[PROJ_SLAB_OF[col // MXU_COLS]]

    def __getitem__(self, idx):
        rows, cols = idx
        lo, hi = cols.start, cols.stop
        parts = []
        while lo < hi:
            g = lo // MXU_COLS
            end = min(hi, (g + 1) * MXU_COLS)
            parts.append(self.slab(lo)[rows, lo - g * MXU_COLS:end - g * MXU_COLS])
            lo = end
        return parts[0] if len(parts) == 1 else jnp.concatenate(parts, axis=1)


def _tap(w8_ref, k, x):
    rows, c = x.shape
    return (x.reshape(rows // SUBLANES, SUBLANES, c) * w8_ref[k][None]).reshape(rows, c)


PROMPT_CONSTS = ("n1g", "n2g", "w_in", "a_ln_g", "a_ln_b", "a_ws", "a_bs_full", "conv_b_w8", "pool_w", "pool_scale",
                 "conv_d_w8", "conv_d_b", "d_ln_g", "d_ln_b", "w_branch", "w_out", "w_r", "b_r")


def _prompt_mixer_kernel(slots_prev, *refs, names, first, cap):
    r = dict(zip(names, refs))
    tm = r["x1"].shape[1]
    hb_ref, proj_ref, ys_ref = r["hb"], _ProjCols(r["proj"]), r["ys"]
    m_ref = r["x1"].at[0]
    cx_ext, p_ext, gl_ext, pool_tmp, conv_tmp = r["cx_ext"], r["p_ext"], r["gl_ext"], r["pool_tmp"], r["conv_tmp"]
    h2buf, slot_v, slot_s, carry, sem = r["h2buf"], r["slot_v"], r["slot_s"], r["carry"], r["sem"]
    xs_hbm = r["xs"]
    win_ref, wbr_ref, wout_ref = r["w_in"].at[0], r["w_branch"].at[0], r["w_out"].at[0]
    j = pl.program_id(1)
    nj = pl.num_programs(1)
    step = pl.program_id(0) * nj + j
    nsteps = pl.num_programs(0) * nj
    trash = N_EXPERTS * cap
    sem_g, sem_d, sem_s = sem.at[0], sem.at[1], sem.at[2]

    def gather_row(tile, row):
        for k in range(2):
            p = slots_prev[(tile * 2 + k) * tm + row]
            pltpu.make_async_copy(r["ys_prev"].at[pl.ds(p, 1)], r["gbuf"].at[k, pl.ds(row, 1)], sem_g).start()

    def dispatch_row(row):
        for k in range(2):
            pltpu.make_async_copy(h2buf.at[pl.ds(row, 1)], xs_hbm.at[pl.ds(slot_s[k, row], 1)], sem_d).start()

    def wait_rows(src, dst, sem_, times):
        for _ in range(times):
            pltpu.make_async_copy(src, dst, sem_).wait()

    @pl.when(step == 0)
    def _():
        carry[...] = r["carry_in"][...]
        h2buf[...] = jnp.zeros_like(h2buf)
        slot_v[...] = (trash + tm * lax.broadcasted_iota(I32, slot_v.shape, 0)
                       + lax.broadcasted_iota(I32, slot_v.shape, 1))
        pltpu.make_async_copy(slot_v.at[pl.ds(0, 2)], slot_s, sem_s).start()
        if not first:
            lax.fori_loop(0, tm, lambda i, c: (gather_row(0, i), c)[1], 0, unroll=DMA_UNROLL)

    @pl.when(j == 0)
    def _():
        cx_ext[0:CTX_B, :] = jnp.zeros((CTX_B, MIX_W), F32)
        p_ext[0:CTX_P, :] = jnp.zeros((CTX_P, MIX_W), F32)
        gl_ext[0:CTX_D, :] = jnp.zeros((CTX_D, MIX_W), F32)

    mod = r["mod"][0]
    sh1, sc1, g1, sh2, sc2 = (mod[:, i * D_MODEL:(i + 1) * D_MODEL] for i in range(5))
    if first:
        x_src = r["x"].at[0]
    else:
        wait_rows(r["ys_prev"].at[pl.ds(0, tm)], r["gbuf"].at[0], sem_g, 2)
        x_src = r["xin"]
        x_src[...] = _combine_rows(r["x"][0], r["route_prev"][0], r["mod_prev"][0][:, 5 * D_MODEL:], r["gbuf"])
    h = _rms(x_src[...], r["n1g"][...]) * (1.0 + sc1) + sh1

    pltpu.make_async_copy(slot_v.at[pl.ds(0, 2)], slot_s, sem_s).wait()

    def proj_piece(c):
        def f():
            proj_ref.slab(c)[...] = jnp.dot(hb_ref[...], win_ref[:, c:c + MXU_COLS], preferred_element_type=F32)
        return f

    def proj_pieces(mixer):
        return [proj_piece(g * MXU_COLS) for g in MIXER_COL_GROUPS[mixer]]

    def merge_piece(i, c):
        def f():
            lo = D_MIX_IN + i * D_MODEL + c
            gates = jnp.dot(hb_ref[...], win_ref[:, lo:lo + MXU_COLS], preferred_element_type=F32)
            br = jnp.dot(ys_ref[i], wbr_ref[i, :, c:c + MXU_COLS], preferred_element_type=F32)
            t = jax.nn.sigmoid(gates) * br
            m_ref[:, c:c + MXU_COLS] = t if i == 0 else m_ref[:, c:c + MXU_COLS] + t
        return f

    def merge_pieces(i):
        return [merge_piece(i, c) for c in range(0, D_MODEL, MXU_COLS)]

    def emit(vector_pieces, matrix_pieces):
        nv, nm = len(vector_pieces), len(matrix_pieces)
        for i, vp in enumerate(vector_pieces):
            vp()
            for mp in matrix_pieces[i * nm // nv:(i + 1) * nm // nv]:
                mp()

    def dispatch_chunk(i, n):
        def f():
            for row in range(i * tm // n, (i + 1) * tm // n):
                dispatch_row(row)
        return f

    hb_ref[...] = h.astype(BF16)
    n_a = len(MIXER_COL_GROUPS[0])
    emit(proj_pieces(0), [dispatch_chunk(i, n_a) for i in range(n_a)])

    rr = lax.broadcasted_iota(I32, (CHUNK, CHUNK), 0)
    cc = lax.broadcasted_iota(I32, (CHUNK, CHUNK), 1)
    wtril = [jnp.where(rr >= cc, r["a_ws"][g], 0.0).astype(BF16) for g in range(A_GROUPS)]
    cg = MIX_W // A_GROUPS

    def a_chunk(c0):
        def f():
            u = jax.nn.gelu(proj_ref[c0:c0 + CHUNK, 0:MIX_W])
            v = _ln(jax.nn.gelu(proj_ref[c0:c0 + CHUNK, MIX_W:2 * MIX_W]), r["a_ln_g"][...], r["a_ln_b"][...])
            vb = v.astype(BF16)
            mixed = jnp.concatenate(
                [jnp.dot(wtril[g], vb[:, g * cg:(g + 1) * cg], preferred_element_type=F32)
                 for g in range(A_GROUPS)], axis=1) + r["a_bs_full"][...]
            ys_ref[0, c0:c0 + CHUNK, :] = (u * mixed).astype(BF16)
        return f

    emit([a_chunk(c0) for c0 in range(0, tm, CHUNK)], [])
    emit(proj_pieces(1), [])

    cbw_ref = r["conv_b_w8"]

    def b_prep(c0):
        def f():
            cx_ext[CTX_B + c0:CTX_B + c0 + CHUNK, :] = _conv_in(
                proj_ref[c0:c0 + CHUNK, 3 * MIX_W:4 * MIX_W] * proj_ref[c0:c0 + CHUNK, 4 * MIX_W:5 * MIX_W])
        return f

    def b_block(r0):
        def f():
            acc = None
            for k in range(CONV_B):
                t = _tap(cbw_ref, k, cx_ext[pl.ds(r0 + CTX_B - (CONV_B - 1) + k, CONV_ROWS), :])
                acc = t if acc is None else acc + t
            ys_ref[1, r0:r0 + CONV_ROWS, :] = (proj_ref[r0:r0 + CONV_ROWS, 2 * MIX_W:3 * MIX_W] * acc).astype(BF16)
        return f

    def b_end():
        tail = pl.ds(tm - (CONV_B - 1), CONV_B - 1)
        r["new_b"][0] = proj_ref[tail, 3 * MIX_W:4 * MIX_W] * proj_ref[tail, 4 * MIX_W:5 * MIX_W]
        cx_ext[0:CTX_B, :] = cx_ext[tm:tm + CTX_B, :]

    emit([b_prep(c0) for c0 in range(0, tm, CHUNK)] + [b_block(r0) for r0 in range(0, tm, CONV_ROWS)] + [b_end],
         merge_pieces(0))
    emit(proj_pieces(2), [])

    pos = j * tm + lax.broadcasted_iota(I32, (tm, 1), 0)
    pg = MIX_W // len(POOL_WINDOWS)

    def c_prep():
        p_ext[CTX_P:CTX_P + tm, :] = proj_ref[:, 5 * MIX_W:6 * MIX_W]

    def c_group(gi, w):
        def f():
            lo = gi * pg
            levels = w.bit_length() - 1
            for lv in range(levels):
                span = 1 << lv
                start = CTX_P if lv == levels - 1 else SUBLANES * (lv + 1)
                rows = CTX_P + tm - start
                if lv == 0:
                    a = p_ext[start:start + rows, lo:lo + pg] + p_ext[pl.ds(start - span, rows), lo:lo + pg]
                else:
                    prev = pool_tmp.at[gi * (gi - 1) // 2 + lv - 1]
                    a = prev[start:start + rows, :] + prev[pl.ds(start - span, rows), :]
                if lv == levels - 1:
                    s = a
                else:
                    pool_tmp[gi * (gi - 1) // 2 + lv, start:start + rows, :] = a
            cnt = jnp.minimum(pos + 1, w).astype(F32)
            pooled = s / cnt - p_ext[CTX_P:CTX_P + tm, lo:lo + pg]
            ys_ref[2, :, lo:lo + pg] = (_bdot(pooled, r["pool_w"][gi]) * r["pool_scale"][:, lo:lo + pg]).astype(BF16)
        return f

    def c_end():
        r["new_p"][0] = p_ext[pl.ds(CTX_P + tm - POOL_CTX, POOL_CTX), :]
        p_ext[0:CTX_P, :] = p_ext[tm:tm + CTX_P, :]

    emit([c_prep] + proj_pieces(3), [])
    emit([c_group(gi, w) for gi, w in enumerate(POOL_WINDOWS)] + [c_end], merge_pieces(1))

    cdw_ref = r["conv_d_w8"]
    off0 = CTX_D - (CONV_D - 1)
    next_tile = jnp.minimum(step + 1, nsteps - 1)

    def d_prep(c0):
        def f():
            gl_ext[CTX_D + c0:CTX_D + c0 + CHUNK, :] = _conv_in(
                proj_ref[c0:c0 + CHUNK, 6 * MIX_W:7 * MIX_W]
                * jax.nn.sigmoid(proj_ref[c0:c0 + CHUNK, 7 * MIX_W:8 * MIX_W]))
        return f

    def d_block(r0):
        def f():
            acc = None
            for rr_ in range(SUBLANES):
                rows = CONV_ROWS if rr_ == 0 else CONV_ROWS + SUBLANES
                y = None
                for q in range(-(-(off0 + CONV_D) // SUBLANES)):
                    k = SUBLANES * q + rr_ - off0
                    if 0 <= k < CONV_D:
                        t = _tap(cdw_ref, k, gl_ext[r0 + SUBLANES * q:r0 + SUBLANES * q + rows, :])
                        y = t if y is None else y + t
                if rr_ > 0:
                    conv_tmp[rr_ - 1] = y
                    y = conv_tmp[rr_ - 1, pl.ds(rr_, CONV_ROWS), :]
                acc = y if acc is None else acc + y
            dc = acc + r["conv_d_b"][...]
            ys_ref[3, r0:r0 + CONV_ROWS, :] = jax.nn.silu(_ln(dc, r["d_ln_g"][...], r["d_ln_b"][...])).astype(BF16)
            if not first:
                for row in range(r0, r0 + CONV_ROWS):
                    gather_row(next_tile, row)
        return f

    def d_end():
        tail = pl.ds(tm - (CONV_D - 1), CONV_D - 1)
        r["new_d"][0] = proj_ref[tail, 6 * MIX_W:7 * MIX_W] * jax.nn.sigmoid(proj_ref[tail, 7 * MIX_W:8 * MIX_W])
        gl_ext[0:CTX_D, :] = gl_ext[tm:tm + CTX_D, :]

    emit([d_prep(c0) for c0 in range(0, tm, CHUNK)] + [d_block(r0) for r0 in range(0, tm, CONV_ROWS)] + [d_end],
         merge_pieces(2))
    for mp in merge_pieces(3):
        mp()

    wait_rows(h2buf, xs_hbm.at[pl.ds(0, tm)], sem_d, 2)
    _merge_tail(x_src, g1, sh2, sc2, r["n2g"][...], lambda rs: m_ref[rs, :], wout_ref, r["w_r"], r["b_r"],
                r["x1"].at[0], h2buf, r["lg"])
    route = _route_rows(r["lg"][...], carry, cap)
    r["route"][0] = route
    r["carry_out"][...] = carry[...]
    _slots_to_smem(route, slot_v, slot_s, sem_s)
    r["slots"][...] = slot_v[...]

    @pl.when(step == nsteps - 1)
    def _():
        pltpu.make_async_copy(slot_v.at[pl.ds(0, 2)], slot_s, sem_s).wait()
        lax.fori_loop(0, tm, lambda i, c: (dispatch_row(i), c)[1], 0, unroll=DMA_UNROLL)
        wait_rows(h2buf, xs_hbm.at[pl.ds(0, tm)], sem_d, 2)
        if not first:
            wait_rows(r["ys_prev"].at[pl.ds(0, tm)], r["gbuf"].at[0], sem_g, 2)


def _const_spec(shape):
    nd = len(shape)
    return pl.BlockSpec(shape, lambda *_: (0,) * nd, pipeline_mode=pl.Buffered(1))


PER_LAYER = ("w_in", "w_branch", "w_out")


def _layer_spec(shape, li):
    nd = len(shape)
    return pl.BlockSpec((1,) + tuple(shape[1:]), lambda *_: (li,) + (0,) * (nd - 1), pipeline_mode=pl.Buffered(1))


def _prompt_mixer(x, mod, lw, li, cap, prev=None):
    n, l, _ = x.shape
    tm = MIX_TM
    first = prev is None
    tok = lambda last: pl.BlockSpec((1, tm, last), lambda b, j, sp: (b, j, 0))
    per_seq = lambda rows, last: pl.BlockSpec((1, rows, last), lambda b, j, sp: (b, 0, 0))
    anyspec = pl.BlockSpec(memory_space=pl.ANY)
    names, ins, in_specs = ["x", "mod", "carry_in"], [x, mod, jnp.zeros((1, LANES), F32)], \
        [tok(D_MODEL), per_seq(1, 6 * D_MODEL), _const_spec((1, LANES))]
    if first:
        slots_prev = jnp.zeros((SUBLANES,), I32)
    else:
        route_prev, mod_prev, ys_prev, slots_prev = prev
        names += ["route_prev", "mod_prev", "ys_prev"]
        ins += [route_prev, mod_prev, ys_prev]
        in_specs += [tok(LANES), per_seq(1, 6 * D_MODEL), anyspec]
    names += list(PROMPT_CONSTS)
    ins += [lw[k] for k in PROMPT_CONSTS]
    in_specs += [_layer_spec(lw[k].shape, li) if k in PER_LAYER else _const_spec(lw[k].shape) for k in PROMPT_CONSTS]

    xs_rows = N_EXPERTS * cap + 2 * tm
    out_names = ["x1", "route", "carry_out", "xs", "slots", "new_b", "new_p", "new_d"]
    out_specs = [tok(D_MODEL), tok(LANES), pl.BlockSpec((1, LANES), lambda b, j, sp: (0, 0)), anyspec,
                 pl.BlockSpec((SUBLANES, tm), lambda b, j, sp: (b * (l // tm) + j, 0)),
                 per_seq(CONV_B - 1, MIX_W), per_seq(POOL_CTX, MIX_W), per_seq(CONV_D - 1, MIX_W)]
    out_shape = [jax.ShapeDtypeStruct((n, l, D_MODEL), F32), jax.ShapeDtypeStruct((n, l, LANES), F32),
                 jax.ShapeDtypeStruct((1, LANES), F32), jax.ShapeDtypeStruct((xs_rows, D_MODEL), F32),
                 jax.ShapeDtypeStruct((n * (l // tm) * SUBLANES, tm), I32),
                 jax.ShapeDtypeStruct((n, CONV_B - 1, MIX_W), F32), jax.ShapeDtypeStruct((n, POOL_CTX, MIX_W), F32),
                 jax.ShapeDtypeStruct((n, CONV_D - 1, MIX_W), F32)]
    scratch = {"hb": pltpu.VMEM((tm, D_MODEL), BF16), "proj": pltpu.VMEM((PROJ_SLABS, tm, MXU_COLS), F32),
               "ys": pltpu.VMEM((N_BRANCH, tm, MIX_W), BF16),
               "cx_ext": pltpu.VMEM((CTX_B + tm, MIX_W), F32), "p_ext": pltpu.VMEM((CTX_P + tm, MIX_W), F32),
               "gl_ext": pltpu.VMEM((CTX_D + tm, MIX_W), F32),
               "pool_tmp": pltpu.VMEM((POOL_TMP_SLOTS, CTX_P + tm, MIX_W // len(POOL_WINDOWS)), F32),
               "conv_tmp": pltpu.VMEM((SUBLANES - 1, CONV_ROWS + SUBLANES, MIX_W), F32),
               "h2buf": pltpu.VMEM((tm, D_MODEL), F32), "lg": pltpu.VMEM((tm, LANES), F32),
               "slot_v": pltpu.VMEM((SUBLANES, tm), I32), "slot_s": pltpu.SMEM((2, tm), I32),
               "carry": pltpu.VMEM((1, LANES), F32), "sem": pltpu.SemaphoreType.DMA((3,))}
    if not first:
        scratch["xin"] = pltpu.VMEM((tm, D_MODEL), F32)
        scratch["gbuf"] = pltpu.VMEM((2, tm, D_MODEL), F32)
    names = tuple(names + out_names + list(scratch))
    outs = pl.pallas_call(
        functools.partial(_prompt_mixer_kernel, names=names, first=first, cap=cap),
        grid_spec=pltpu.PrefetchScalarGridSpec(
            num_scalar_prefetch=1, grid=(n, l // tm), in_specs=in_specs, out_specs=out_specs,
            scratch_shapes=list(scratch.values())),
        out_shape=out_shape,
        compiler_params=pltpu.CompilerParams(dimension_semantics=("arbitrary", "arbitrary"),
                                             vmem_limit_bytes=MIXER_VMEM_BYTES, disable_bounds_checks=True),
        name="prompt_mixer",
    )(slots_prev, *ins)
    return dict(zip(out_names, outs))


SAMPLE_CONSTS = ("n1g", "n2g", "w_in", "a_ln_g", "a_ln_b", "a_w0", "a_b0", "conv_b_w", "pool_w", "pool_scale",
                 "conv_d_w", "conv_d_b", "d_ln_g", "d_ln_b", "w_branch", "w_out", "w_r", "b_r")


def _sample_mixer_kernel(slots_prev, *refs, names, first, cap):
    r = dict(zip(names, refs))
    rows = r["x1"].shape[0]
    hb_ref, proj_ref, ys_ref, m_ref = r["hb"], r["proj"], r["ys"], r["m"]
    h2buf, slot_v, slot_s, carry, sem = r["h2buf"], r["slot_v"], r["slot_s"], r["carry"], r["sem"]
    xs_hbm = r["xs"]
    win_ref, wbr_ref, wout_ref = r["w_in"].at[0], r["w_branch"].at[0], r["w_out"].at[0]
    cbw_ref, cdw_ref, stb_ref, stp_ref, std_ref = r["conv_b_w"], r["conv_d_w"], r["st_b"], r["st_p"], r["st_d"]
    sem_g, sem_d, sem_s = sem.at[0], sem.at[1], sem.at[2]

    mod = r["mod"][...]
    sh1, sc1, g1, sh2, sc2 = (mod[:, i * D_MODEL:(i + 1) * D_MODEL] for i in range(5))
    if first:
        x_src = r["x"]
    else:
        def gather_row(i, c):
            for k in range(2):
                p = slots_prev[k * rows + i]
                pltpu.make_async_copy(r["ys_prev"].at[pl.ds(p, 1)], r["gbuf"].at[k, pl.ds(i, 1)], sem_g).start()
            return c
        lax.fori_loop(0, rows, gather_row, 0, unroll=DMA_UNROLL)
        for _ in range(2):
            pltpu.make_async_copy(r["ys_prev"].at[pl.ds(0, rows)], r["gbuf"].at[0], sem_g).wait()
        x_src = r["xin"]
        x_src[...] = _combine_rows(r["x"][...], r["route_prev"][...], r["mod_prev"][...][:, 5 * D_MODEL:], r["gbuf"])
    h = _rms(x_src[...], r["n1g"][...]) * (1.0 + sc1) + sh1
    _project(h, win_ref, hb_ref, proj_ref)

    u = jax.nn.gelu(proj_ref[:, 0:MIX_W])
    v = _ln(jax.nn.gelu(proj_ref[:, MIX_W:2 * MIX_W]), r["a_ln_g"][...], r["a_ln_b"][...])
    r["v"][...] = v
    ys_ref[0] = (u * (r["a_w0"][...] * v + r["a_b0"][...])).astype(BF16)

    cx = proj_ref[:, 3 * MIX_W:4 * MIX_W] * proj_ref[:, 4 * MIX_W:5 * MIX_W]
    r["cx"][...] = cx
    acc = cbw_ref[CONV_B - 1:CONV_B, :] * _conv_in(cx)
    for k in range(CONV_B - 1):
        acc = acc + cbw_ref[k:k + 1, :] * _conv_in(stb_ref[:, k, :])
    ys_ref[1] = (proj_ref[:, 2 * MIX_W:3 * MIX_W] * acc).astype(BF16)

    pin = proj_ref[:, 5 * MIX_W:6 * MIX_W]
    r["pin"][...] = pin
    pg = MIX_W // len(POOL_WINDOWS)
    mixed = []
    for gi, w in enumerate(POOL_WINDOWS):
        lo = gi * pg
        s = pin[:, lo:lo + pg]
        for i in range(1, w):
            row = POOL_CTX - i
            s = s + stp_ref[:, row, lo:lo + pg]
        pooled = s / float(min(PAST_LEN + 1, w)) - pin[:, lo:lo + pg]
        mixed.append(_bdot(pooled, r["pool_w"][gi]))
    ys_ref[2] = (jnp.concatenate(mixed, axis=1) * r["pool_scale"][...]).astype(BF16)

    gl = proj_ref[:, 6 * MIX_W:7 * MIX_W] * jax.nn.sigmoid(proj_ref[:, 7 * MIX_W:8 * MIX_W])
    r["gl"][...] = gl
    acc = cdw_ref[CONV_D - 1:CONV_D, :] * _conv_in(gl)
    for k in range(CONV_D - 1):
        acc = acc + cdw_ref[k:k + 1, :] * _conv_in(std_ref[:, k, :])
    dc = acc + r["conv_d_b"][...]
    ys_ref[3] = jax.nn.silu(_ln(dc, r["d_ln_g"][...], r["d_ln_b"][...])).astype(BF16)

    for i in range(N_BRANCH):
        _merge_branch(i, hb_ref, win_ref, ys_ref, wbr_ref, m_ref)
    _merge_tail(x_src, g1, sh2, sc2, r["n2g"][...], lambda rs: m_ref[rs, :], wout_ref, r["w_r"], r["b_r"],
                r["x1"], h2buf, r["lg"])

    carry[...] = r["carry_in"][...]
    route = _route_rows(r["lg"][...], carry, cap)
    r["route"][...] = route
    r["carry_out"][...] = carry[...]
    slot_v[...] = jnp.zeros_like(slot_v)
    _slots_to_smem(route, slot_v, slot_s, sem_s)
    r["slots"][...] = slot_v[...]
    pltpu.make_async_copy(slot_v.at[pl.ds(0, 2)], slot_s, sem_s).wait()

    def dispatch_row(i, c):
        for k in range(2):
            pltpu.make_async_copy(h2buf.at[pl.ds(i, 1)], xs_hbm.at[pl.ds(slot_s[k, i], 1)], sem_d).start()
        return c
    lax.fori_loop(0, rows, dispatch_row, 0, unroll=DMA_UNROLL)
    for _ in range(2):
        pltpu.make_async_copy(h2buf, xs_hbm.at[pl.ds(0, rows)], sem_d).wait()


def _sample_mixer(x, mod, st_b, st_p, st_d, lw, li, cap, carry_in, xs, prev=None):
    rows = x.shape[0]
    first = prev is None
    full = lambda a: pl.BlockSpec(a.shape, lambda i, sp: (0,) * a.ndim)
    anyspec = pl.BlockSpec(memory_space=pl.ANY)
    names, ins = ["x", "mod", "carry_in", "xs_in", "st_b", "st_p", "st_d"], [x, mod, carry_in, xs, st_b, st_p, st_d]
    in_specs = [full(x), full(mod), full(carry_in), anyspec, full(st_b), full(st_p), full(st_d)]
    if first:
        slots_prev = jnp.zeros((SUBLANES,), I32)
    else:
        route_prev, mod_prev, ys_prev, slots_prev = prev
        names += ["route_prev", "mod_prev", "ys_prev"]
        ins += [route_prev, mod_prev, ys_prev]
        in_specs += [full(route_prev), full(mod_prev), anyspec]
    names += list(SAMPLE_CONSTS)
    ins += [lw[k] for k in SAMPLE_CONSTS]
    in_specs += [_layer_spec(lw[k].shape, li) if k in PER_LAYER else full(lw[k]) for k in SAMPLE_CONSTS]

    mk = lambda last: jax.ShapeDtypeStruct((rows, last), F32)
    out_names = ["x1", "route", "carry_out", "xs", "slots", "cx", "pin", "gl", "v"]
    out_shape = [mk(D_MODEL), mk(LANES), jax.ShapeDtypeStruct((1, LANES), F32),
                 jax.ShapeDtypeStruct(xs.shape, xs.dtype), jax.ShapeDtypeStruct((SUBLANES, rows), I32),
                 mk(MIX_W), mk(MIX_W), mk(MIX_W), mk(MIX_W)]
    out_specs = [pl.BlockSpec(s.shape, lambda i, sp, nd=len(s.shape): (0,) * nd) for s in out_shape]
    out_specs[3] = anyspec
    scratch = {"hb": pltpu.VMEM((rows, D_MODEL), BF16), "proj": pltpu.VMEM((rows, D_MIX_IN), F32),
               "ys": pltpu.VMEM((N_BRANCH, rows, MIX_W), BF16), "m": pltpu.VMEM((rows, D_MODEL), F32),
               "h2buf": pltpu.VMEM((rows, D_MODEL), F32), "lg": pltpu.VMEM((rows, LANES), F32),
               "slot_v": pltpu.VMEM((SUBLANES, rows), I32), "slot_s": pltpu.SMEM((2, rows), I32),
               "carry": pltpu.VMEM((1, LANES), F32), "sem": pltpu.SemaphoreType.DMA((3,))}
    if not first:
        scratch["xin"] = pltpu.VMEM((rows, D_MODEL), F32)
        scratch["gbuf"] = pltpu.VMEM((2, rows, D_MODEL), F32)
    names = tuple(names + out_names + list(scratch))
    outs = pl.pallas_call(
        functools.partial(_sample_mixer_kernel, names=names, first=first, cap=cap),
        grid_spec=pltpu.PrefetchScalarGridSpec(
            num_scalar_prefetch=1, grid=(1,), in_specs=in_specs, out_specs=out_specs,
            scratch_shapes=list(scratch.values())),
        out_shape=out_shape,
        input_output_aliases={1 + names.index("xs_in"): out_names.index("xs")},
        compiler_params=pltpu.CompilerParams(dimension_semantics=("arbitrary",), vmem_limit_bytes=MIXER_VMEM_BYTES,
                                             disable_bounds_checks=True),
        name="sample_mixer",
    )(slots_prev, *ins)
    return dict(zip(out_names, outs))


def _expert_kernel(vb_ref, be_ref, nv_ref, nu_ref, xs_ref, wg_ref, wu_ref, wd_ref, ys_ref, wgb, wub, wdb):
    b = pl.program_id(0)

    @pl.when(b < nu_ref[0])
    def _():
        @pl.when(jnp.logical_or(b == 0, be_ref[b] != be_ref[jnp.maximum(b - 1, 0)]))
        def _():
            wgb[...] = wg_ref[0, 0].astype(BF16)
            wub[...] = wu_ref[0, 0].astype(BF16)
            wdb[...] = wd_ref[0, 0].astype(BF16)

        valid = lax.broadcasted_iota(I32, (xs_ref.shape[0], 1), 0) < nv_ref[b]
        xb = jnp.where(valid, xs_ref[...], 0.0).astype(BF16)
        g = jnp.dot(xb, wgb[...], preferred_element_type=F32)
        u = jnp.dot(xb, wub[...], preferred_element_type=F32)
        ys_ref[...] = jnp.dot((jax.nn.silu(g) * u).astype(BF16), wdb[...], preferred_element_type=F32)


def _experts(xs, counts, cap, li, w_eg, w_eu, w_ed):
    bm = MOE_BM
    n_blocks = -(-xs.shape[0] // bm)
    max_used = min(N_EXPERTS * (cap // bm), -(-(cap * 2) // bm) + N_EXPERTS)
    nblk = (counts + bm - 1) // bm
    ends = jnp.cumsum(nblk)
    n_used = ends[-1:]
    b = jnp.arange(max_used, dtype=I32)
    bc = jnp.minimum(b, n_used[0] - 1)
    be = jnp.minimum(jnp.sum(ends[None, :] <= bc[:, None], axis=1), N_EXPERTS - 1).astype(I32)
    mine = be[:, None] == jnp.arange(N_EXPERTS, dtype=I32)[None, :]
    pick = lambda per_expert: jnp.sum(jnp.where(mine, per_expert[None, :], 0), axis=1)
    local = bc - pick(ends - nblk)
    vblock = (be * (cap // bm) + local).astype(I32)
    nvalid = jnp.clip(pick(counts) - local * bm, 0, bm).astype(I32)
    wspec = lambda shape: pl.BlockSpec((1, 1) + shape, lambda i, vb, be_, nv, nu: (li, be_[i], 0, 0))
    return pl.pallas_call(
        _expert_kernel,
        grid_spec=pltpu.PrefetchScalarGridSpec(
            num_scalar_prefetch=4,
            grid=(max_used,),
            in_specs=[pl.BlockSpec((bm, D_MODEL), lambda i, vb, be_, nv, nu: (vb[i], 0)),
                      wspec((D_MODEL, D_EXPERT)), wspec((D_MODEL, D_EXPERT)), wspec((D_EXPERT, D_MODEL))],
            out_specs=pl.BlockSpec((bm, D_MODEL), lambda i, vb, be_, nv, nu: (vb[i], 0)),
            scratch_shapes=[pltpu.VMEM((D_MODEL, D_EXPERT), BF16), pltpu.VMEM((D_MODEL, D_EXPERT), BF16),
                            pltpu.VMEM((D_EXPERT, D_MODEL), BF16)]),
        out_shape=jax.ShapeDtypeStruct((n_blocks * bm, D_MODEL), F32),
        compiler_params=pltpu.CompilerParams(dimension_semantics=("arbitrary",)),
        name="experts",
    )(vblock, be, nvalid, n_used.astype(I32), xs, w_eg, w_eu, w_ed)


def _combine_kernel(pos_ref, ys_hbm, x1_ref, route_ref, g2_ref, fg_ref, out_ref, buf, sem):
    tc = x1_ref.shape[0]
    i = pl.program_id(0)
    n = pl.num_programs(0)
    slot = i % 2

    def issue(tile, s):
        def body(row, carry):
            for k in range(2):
                p = pos_ref[(tile * 2 + k) * tc + row]
                pltpu.make_async_copy(ys_hbm.at[pl.ds(p, 1)], buf.at[s, k, pl.ds(row, 1)], sem.at[s]).start()
            return carry
        lax.fori_loop(0, tc, body, 0, unroll=DMA_UNROLL)

    @pl.when(i == 0)
    def _():
        issue(0, 0)

    @pl.when(i + 1 < n)
    def _():
        issue(i + 1, 1 - slot)

    for k in range(2):
        pltpu.make_async_copy(ys_hbm.at[pl.ds(0, tc)], buf.at[slot, k], sem.at[slot]).wait()
    x2 = _combine_rows(x1_ref[...], route_ref[...], g2_ref[...].reshape(-1, D_MODEL), buf.at[slot])
    out_ref[...] = _rms(x2, fg_ref[...])


def _combine(ys, pos, tc, x1, route, g2, fg):
    t = x1.shape[0]
    if g2.ndim == 3:
        seq = t // g2.shape[0]
        g2spec = pl.BlockSpec((1, 1, D_MODEL), lambda i, pos: (i * tc // seq, 0, 0))
    else:
        g2spec = pl.BlockSpec((tc, D_MODEL), lambda i, pos: (i, 0))
    tokspec = lambda last: pl.BlockSpec((tc, last), lambda i, pos: (i, 0))
    return pl.pallas_call(
        _combine_kernel,
        grid_spec=pltpu.PrefetchScalarGridSpec(
            num_scalar_prefetch=1,
            grid=(t // tc,),
            in_specs=[pl.BlockSpec(memory_space=pl.ANY), tokspec(D_MODEL), tokspec(LANES), g2spec,
                      pl.BlockSpec((1, D_MODEL), lambda i, pos: (0, 0))],
            out_specs=tokspec(D_MODEL),
            scratch_shapes=[pltpu.VMEM((2, 2, tc, D_MODEL), F32), pltpu.SemaphoreType.DMA((2,))]),
        out_shape=jax.ShapeDtypeStruct((t, D_MODEL), F32),
        compiler_params=pltpu.CompilerParams(dimension_semantics=("arbitrary",), disable_bounds_checks=True,
                                             vmem_limit_bytes=MIXER_VMEM_BYTES),
        name="combine",
    )(pos, ys, x1, route, g2, fg)


def _layer_weights(li, norm1_g, norm2_g, w_in, a_ln_g, a_ln_b, a_ws, a_bs, conv_b_w, pool_w, pool_scale,
                   conv_d_w, conv_d_b, d_ln_g, d_ln_b, w_branch, w_out, w_rg, b_rg, w_re, b_re):
    row = lambda a: a[li].reshape(1, -1)
    cg = MIX_W // A_GROUPS
    pad = LANES - N_GROUPS_MOE - N_EXPERTS
    return {
        "n1g": row(norm1_g), "n2g": row(norm2_g), "w_in": w_in.astype(BF16),
        "a_ln_g": row(a_ln_g), "a_ln_b": row(a_ln_b), "a_ws": a_ws[li],
        "a_bs_full": jnp.repeat(a_bs[li].T, cg, axis=1),
        "a_w0": jnp.repeat(a_ws[li, :, 0, 0], cg).reshape(1, MIX_W),
        "a_b0": jnp.repeat(a_bs[li, :, 0], cg).reshape(1, MIX_W),
        "conv_b_w": conv_b_w[li], "pool_w": pool_w[li], "pool_scale": row(pool_scale),
        "conv_b_w8": jnp.broadcast_to(conv_b_w[li][:, None, :], (CONV_B, SUBLANES, MIX_W)),
        "conv_d_w8": jnp.broadcast_to(conv_d_w[li][:, None, :], (CONV_D, SUBLANES, MIX_W)),
        "conv_d_w": conv_d_w[li], "conv_d_b": row(conv_d_b), "d_ln_g": row(d_ln_g), "d_ln_b": row(d_ln_b),
        "w_branch": w_branch.astype(BF16), "w_out": w_out.astype(BF16),
        "w_r": jnp.concatenate([w_rg[li], w_re[li], jnp.zeros((D_MODEL, pad), F32)], axis=1),
        "b_r": jnp.concatenate([b_rg[li], b_re[li], jnp.zeros((pad,), F32)]).reshape(1, LANES),
    }


def _slots(slot_rows):
    return slot_rows.reshape(-1, SUBLANES, slot_rows.shape[-1])[:, :2].reshape(-1)


def kernel(x_prompt, x_sample, state_conv_b, state_pool, state_conv_d, c_prompt, c_sample, w_mod, b_mod, norm1_g, norm2_g, w_in, a_ln_g, a_ln_b, a_ws, a_bs, conv_b_w, pool_w, pool_scale, conv_d_w, conv_d_b, d_ln_g, d_ln_b, w_branch, w_out, w_rg, b_rg, w_re, b_re, w_eg, w_eu, w_ed, final_g):
    depth = w_mod.shape[0]
    nb, seq, _ = x_prompt.shape
    ns = x_sample.shape[0]
    t = nb * seq
    cap = -(-(t + ns) // MOE_BM) * MOE_BM
    mod_all = _modulation(jnp.concatenate([c_prompt, c_sample], axis=0), w_mod, b_mod)
    fg = final_g.reshape(1, D_MODEL)

    xp = x_prompt
    xs_tok = x_sample.reshape(ns, D_MODEL)
    prev_p = prev_s = None
    outs = {k: [] for k in ("bp", "pp", "dp", "bs", "ps", "ds", "vs")}
    for li in range(depth):
        lw = _layer_weights(li, norm1_g, norm2_g, w_in, a_ln_g, a_ln_b, a_ws, a_bs, conv_b_w, pool_w, pool_scale,
                            conv_d_w, conv_d_b, d_ln_g, d_ln_b, w_branch, w_out, w_rg, b_rg, w_re, b_re)
        mod_p = mod_all[li, :nb].reshape(nb, 1, 6 * D_MODEL)
        mod_s = mod_all[li, nb:]

        pm = _prompt_mixer(xp, mod_p, lw, li, cap, prev_p)
        outs["bp"].append(pm["new_b"])
        outs["pp"].append(pm["new_p"])
        outs["dp"].append(pm["new_d"])

        st_b, st_p, st_d = state_conv_b[li], state_pool[li], state_conv_d[li]
        sm = _sample_mixer(xs_tok, mod_s, st_b, st_p, st_d, lw, li, cap,
                           pm["carry_out"], pm["xs"], prev_s)
        outs["bs"].append(jnp.concatenate([st_b[:, 1:], sm["cx"][:, None]], axis=1))
        outs["ps"].append(jnp.concatenate([st_p[:, 1:], sm["pin"][:, None]], axis=1))
        outs["ds"].append(jnp.concatenate([st_d[:, 1:], sm["gl"][:, None]], axis=1))
        outs["vs"].append(sm["v"][:, None])

        counts = sm["carry_out"][0, ROUTE_LANE0:ROUTE_LANE0 + N_EXPERTS].astype(I32)
        ys = _experts(sm["xs"], counts, cap, li, w_eg, w_eu, w_ed)
        route_p = pm["route"]
        prev_p = (route_p, mod_p, ys, _slots(pm["slots"]))
        prev_s = (sm["route"], mod_s, ys, _slots(sm["slots"]))
        xp, xs_tok = pm["x1"], sm["x1"]

    route_p, mod_p, ys, slots_p = prev_p
    route_s, mod_s, _, slots_s = prev_s
    y_p = _combine(ys, slots_p, MIX_TM, xp.reshape(t, D_MODEL), route_p.reshape(t, LANES), mod_p[:, :, 5 * D_MODEL:], fg)
    y_s = _combine(ys, slots_s, ns, xs_tok, route_s, mod_s[:, 5 * D_MODEL:], fg)

    stack = lambda k: jnp.stack(outs[k])
    return (y_p.reshape(nb, seq, D_MODEL), y_s.reshape(ns, 1, D_MODEL), stack("bp"), stack("pp"), stack("dp"),
            stack("bs"), stack("ps"), stack("ds"), stack("vs"))
```

```python
import functools

import jax
import jax.numpy as jnp
from jax import lax
from jax.experimental import pallas as pl
from jax.experimental.pallas import tpu as pltpu

F32 = jnp.float32
BF16 = jnp.bfloat16
I32 = jnp.int32
U32 = jnp.uint32

D_MODEL = 1024
MIX_W = D_MODEL // 2
CHUNK = 128
A_GROUPS = 4
CONV_B = 3
POOL_WINDOWS = (2, 4, 8, 16)
POOL_CTX = 15
CONV_D = 31
N_BRANCH = 4
N_GROUPS_MOE = 4
EXPERTS_PER_GROUP = 8
N_EXPERTS = N_GROUPS_MOE * EXPERTS_PER_GROUP
D_EXPERT = D_MODEL // 4
EPS = 1e-6
D_MIX_IN = 8 * MIX_W
D_IN = D_MIX_IN + N_BRANCH * D_MODEL
PAST_LEN = 16384

LANES = 128
SUBLANES = 8
ROUTE_LANE0 = N_GROUPS_MOE
ROUTE_E, ROUTE_SLOT, ROUTE_W = 0, 2, 4

MIX_TM = 512
CONV_ROWS = 32
TAIL_ROWS = 256
MXU_COLS = 256
CTX_B = 8
CTX_P = 32
CTX_D = 32
POOL_TMP_SLOTS = sum(w.bit_length() - 2 for w in POOL_WINDOWS)
MOE_BM = 256
DMA_UNROLL = 8
MIXER_VMEM_BYTES = 56 * 1024 * 1024


def _rms(x, g):
    return x * lax.rsqrt(jnp.mean(x * x, axis=-1, keepdims=True) + EPS) * g


def _ln(x, g, b):
    mu = jnp.mean(x, axis=-1, keepdims=True)
    xc = x - mu
    return xc * lax.rsqrt(jnp.mean(xc * xc, axis=-1, keepdims=True) + EPS) * g + b


def _conv_in(x):
    return x.astype(BF16).astype(F32)


def _pack_rows(x):
    c = x.shape[1] // 2
    bits = lambda a: lax.bitcast_convert_type(a.astype(BF16).astype(F32), U32)
    return (bits(x[:, :c]) >> 16) | (bits(x[:, c:]) & jnp.uint32(0xFFFF0000))


def _unpack_rows(p):
    lo = lax.bitcast_convert_type(p << 16, F32)
    hi = lax.bitcast_convert_type(p & jnp.uint32(0xFFFF0000), F32)
    return jnp.concatenate([lo, hi], axis=1).astype(BF16)


def _bdot(a, b):
    return jnp.dot(a.astype(BF16), b.astype(BF16), preferred_element_type=F32)


def _mod_kernel(c_ref, w_ref, b_ref, o_ref):
    o_ref[0] = _bdot(c_ref[...], w_ref[0]) + b_ref[0]


def _modulation(c_all, w_mod, b_mod):
    depth = w_mod.shape[0]
    rows = c_all.shape[0]
    tn = 1536
    return pl.pallas_call(
        _mod_kernel,
        grid=(depth, 6 * D_MODEL // tn),
        in_specs=[pl.BlockSpec((rows, D_MODEL), lambda l, n: (0, 0)),
                  pl.BlockSpec((1, D_MODEL, tn), lambda l, n: (l, 0, n)),
                  pl.BlockSpec((1, 1, tn), lambda l, n: (l, 0, n))],
        out_specs=pl.BlockSpec((1, rows, tn), lambda l, n: (l, 0, n)),
        out_shape=jax.ShapeDtypeStruct((depth, rows, 6 * D_MODEL), F32),
        name="modulation",
    )(c_all, w_mod, b_mod.reshape(depth, 1, 6 * D_MODEL))


def _project(h, win_ref, hb_ref, proj_ref, between=None):
    hb_ref[...] = h.astype(BF16)
    for i, c in enumerate(range(0, D_MIX_IN, D_MODEL)):
        proj_ref[:, c:c + D_MODEL] = jnp.dot(hb_ref[...], win_ref[:, c:c + D_MODEL], preferred_element_type=F32)
        if between is not None:
            between(i)


def _merge_branch(i, hb_ref, win_ref, ys_ref, wbr_ref, m_ref):
    lo = D_MIX_IN + i * D_MODEL
    rows = hb_ref.shape[0]
    for r0 in range(0, rows, TAIL_ROWS):
        rs = slice(r0, min(r0 + TAIL_ROWS, rows))
        gates = jnp.dot(hb_ref[rs, :], win_ref[:, lo:lo + D_MODEL], preferred_element_type=F32)
        br = jnp.dot(ys_ref[i, rs, :], wbr_ref[i], preferred_element_type=F32)
        t = jax.nn.sigmoid(gates) * br
        m_ref[rs, :] = t if i == 0 else m_ref[rs, :] + t


def _merge_tail(x_ref, g1, sh2, sc2, n2g, m_rows, wout_ref, wr_ref, br_ref, x1_ref, h2_ref, lg_ref):
    rows = x1_ref.shape[0]
    for r0 in range(0, rows, TAIL_ROWS):
        rs = slice(r0, min(r0 + TAIL_ROWS, rows))
        per_row = lambda a: a if a.shape[0] == 1 else a[rs]
        out = jnp.dot(m_rows(rs).astype(BF16), wout_ref[...], preferred_element_type=F32)
        x1 = x_ref[rs, :] + per_row(g1) * out
        h2 = _rms(x1, n2g) * (1.0 + per_row(sc2)) + per_row(sh2)
        x1_ref[rs, :] = x1
        h2_ref[rs, :] = _pack_rows(h2)
        lg_ref[rs, :] = _bdot(h2, wr_ref[...]) + br_ref[...]


def _route_rows(lg, carry_ref, cap):
    tr = lg.shape[0]
    lane = lax.broadcasted_iota(I32, (tr, LANES), 1)
    lanef = lane.astype(F32)

    def first_argmax(vals):
        mx = jnp.max(vals, axis=-1, keepdims=True)
        return mx, jnp.min(jnp.where(vals == mx, lanef, float(LANES)), axis=-1, keepdims=True)

    neg = -jnp.inf
    gl = jnp.where(lane < N_GROUPS_MOE, lg, neg)
    gmax, gsel = first_argmax(gl)
    p_g = 1.0 / jnp.sum(jnp.exp(gl - gmax), axis=-1, keepdims=True)
    e_lo = ROUTE_LANE0 + EXPERTS_PER_GROUP * gsel
    emask = jnp.logical_and(lanef >= e_lo, lanef < e_lo + EXPERTS_PER_GROUP)
    el = jnp.where(emask, lg, neg)
    emax = jnp.max(el, axis=-1, keepdims=True)
    ex = jnp.exp(el - emax)
    prob = jnp.where(emask, ex / jnp.sum(ex, axis=-1, keepdims=True), -1.0)
    p1, i1 = first_argmax(prob)
    p2, i2 = first_argmax(jnp.where(lanef == i1, -1.0, prob))
    w1 = p_g * p1 / (p1 + p2)
    w2 = p_g * p2 / (p1 + p2)

    hot1 = lanef == i1
    hot2 = lanef == i2
    hot = jnp.logical_or(hot1, hot2).astype(F32)
    rr = lax.broadcasted_iota(I32, (tr, tr), 0)
    cc = lax.broadcasted_iota(I32, (tr, tr), 1)
    before = jnp.dot((rr > cc).astype(BF16), hot.astype(BF16), preferred_element_type=F32) + carry_ref[...]
    rank1 = jnp.sum(jnp.where(hot1, before, 0.0), axis=-1, keepdims=True)
    rank2 = jnp.sum(jnp.where(hot2, before, 0.0), axis=-1, keepdims=True)
    carry_ref[...] = carry_ref[...] + jnp.sum(hot, axis=0, keepdims=True)

    e1 = i1 - ROUTE_LANE0
    e2 = i2 - ROUTE_LANE0
    fields = (e1, e2, e1 * float(cap) + rank1, e2 * float(cap) + rank2, w1, w2)
    out = jnp.zeros((tr, LANES), F32)
    for k, f in enumerate(fields):
        out = jnp.where(lane == k, f, out)
    return out


def _slots_to_smem(route, slot_v, slot_s, sem):
    slot_v[0:2, :] = jnp.transpose(route)[ROUTE_SLOT:ROUTE_SLOT + 2, :].astype(I32)
    pltpu.make_async_copy(slot_v.at[pl.ds(0, 2)], slot_s, sem).start()


def _combine_rows(x1_prev, route_prev, g2, gbuf):
    return x1_prev + g2 * (route_prev[:, ROUTE_W:ROUTE_W + 1] * gbuf[0]
                           + route_prev[:, ROUTE_W + 1:ROUTE_W + 2] * gbuf[1])


MIXER_COL_GROUPS = tuple(range(lo * MIX_W // MXU_COLS, hi * MIX_W // MXU_COLS) for lo, hi in ((0, 2), (2, 5), (5, 6), (6, 8)))
PROJ_SLABS = max(len(g) for g in MIXER_COL_GROUPS)
PROJ_SLAB_OF = {g: g - groups[0] for groups in MIXER_COL_GROUPS for g in groups}


class _ProjCols:
    def __init__(self, ref):
        self.ref = ref

    def slab(self, col):
        return self.ref.at[PROJ_SLAB_OF[col // MXU_COLS]]

    def __getitem__(self, idx):
        rows, cols = idx
        lo, hi = cols.start, cols.stop
        parts = []
        while lo < hi:
            g = lo // MXU_COLS
            end = min(hi, (g + 1) * MXU_COLS)
            parts.append(self.slab(lo)[rows, lo - g * MXU_COLS:end - g * MXU_COLS])
            lo = end
        return parts[0] if len(parts) == 1 else jnp.concatenate(parts, axis=1)


def _tap(w8_ref, k, x):
    rows, c = x.shape
    return (x.reshape(rows // SUBLANES, SUBLANES, c) * w8_ref[k][None]).reshape(rows, c)


PROMPT_CONSTS = ("n1g", "n2g", "w_in", "a_ln_g", "a_ln_b", "a_ws", "a_bs_full", "conv_b_w8", "pool_w", "pool_scale",
                 "conv_d_w8", "conv_d_b", "d_ln_g", "d_ln_b", "w_branch", "w_out", "w_r", "b_r")


def _prompt_mixer_kernel(slots_prev, *refs, names, first, cap):
    r = dict(zip(names, refs))
    tm = r["x1"].shape[1]
    hb_ref, proj_ref, ys_ref = r["hb"], _ProjCols(r["proj"]), r["ys"]
    m_ref = r["x1"].at[0]
    cx_ext, p_ext, gl_ext, pool_tmp, conv_tmp = r["cx_ext"], r["p_ext"], r["gl_ext"], r["pool_tmp"], r["conv_tmp"]
    h2buf, slot_v, slot_s, carry, sem = r["h2buf"], r["slot_v"], r["slot_s"], r["carry"], r["sem"]
    xs_hbm = r["xs"]
    win_ref, wbr_ref, wout_ref = r["w_in"].at[0], r["w_branch"].at[0], r["w_out"].at[0]
    j = pl.program_id(1)
    nj = pl.num_programs(1)
    step = pl.program_id(0) * nj + j
    nsteps = pl.num_programs(0) * nj
    trash = N_EXPERTS * cap
    sem_g, sem_d, sem_s = sem.at[0], sem.at[1], sem.at[2]

    def gather_row(tile, row):
        for k in range(2):
            p = slots_prev[(tile * 2 + k) * tm + row]
            pltpu.make_async_copy(r["ys_prev"].at[pl.ds(p, 1)], r["gbuf"].at[k, pl.ds(row, 1)], sem_g).start()

    def dispatch_row(row):
        for k in range(2):
            pltpu.make_async_copy(h2buf.at[pl.ds(row, 1)], xs_hbm.at[pl.ds(slot_s[k, row], 1)], sem_d).start()

    def wait_rows(src, dst, sem_, times):
        for _ in range(times):
            pltpu.make_async_copy(src, dst, sem_).wait()

    @pl.when(step == 0)
    def _():
        carry[...] = r["carry_in"][...]
        h2buf[...] = jnp.zeros_like(h2buf)
        slot_v[...] = (trash + tm * lax.broadcasted_iota(I32, slot_v.shape, 0)
                       + lax.broadcasted_iota(I32, slot_v.shape, 1))
        pltpu.make_async_copy(slot_v.at[pl.ds(0, 2)], slot_s, sem_s).start()
        if not first:
            lax.fori_loop(0, tm, lambda i, c: (gather_row(0, i), c)[1], 0, unroll=DMA_UNROLL)

    @pl.when(j == 0)
    def _():
        cx_ext[0:CTX_B, :] = jnp.zeros((CTX_B, MIX_W), F32)
        p_ext[0:CTX_P, :] = jnp.zeros((CTX_P, MIX_W), F32)
        gl_ext[0:CTX_D, :] = jnp.zeros((CTX_D, MIX_W), F32)

    mod = r["mod"][0]
    sh1, sc1, g1, sh2, sc2 = (mod[:, i * D_MODEL:(i + 1) * D_MODEL] for i in range(5))
    if first:
        x_src = r["x"].at[0]
    else:
        wait_rows(r["ys_prev"].at[pl.ds(0, tm)], r["gbuf"].at[0], sem_g, 2)
        x_src = r["xin"]
        x_src[...] = _combine_rows(r["x"][0], r["route_prev"][0], r["mod_prev"][0][:, 5 * D_MODEL:], r["gbuf"])
    h = _rms(x_src[...], r["n1g"][...]) * (1.0 + sc1) + sh1

    pltpu.make_async_copy(slot_v.at[pl.ds(0, 2)], slot_s, sem_s).wait()

    def proj_piece(c):
        def f():
            proj_ref.slab(c)[...] = jnp.dot(hb_ref[...], win_ref[:, c:c + MXU_COLS], preferred_element_type=F32)
        return f

    def proj_pieces(mixer):
        return [proj_piece(g * MXU_COLS) for g in MIXER_COL_GROUPS[mixer]]

    def merge_piece(i, c):
        def f():
            lo = D_MIX_IN + i * D_MODEL + c
            gates = jnp.dot(hb_ref[...], win_ref[:, lo:lo + MXU_COLS], preferred_element_type=F32)
            br = jnp.dot(ys_ref[i], wbr_ref[i, :, c:c + MXU_COLS], preferred_element_type=F32)
            t = jax.nn.sigmoid(gates) * br
            m_ref[:, c:c + MXU_COLS] = t if i == 0 else m_ref[:, c:c + MXU_COLS] + t
        return f

    def merge_pieces(i):
        return [merge_piece(i, c) for c in range(0, D_MODEL, MXU_COLS)]

    def emit(vector_pieces, matrix_pieces):
        nv, nm = len(vector_pieces), len(matrix_pieces)
        for i, vp in enumerate(vector_pieces):
            vp()
            for mp in matrix_pieces[i * nm // nv:(i + 1) * nm // nv]:
                mp()

    def dispatch_chunk(i, n):
        def f():
            for row in range(i * tm // n, (i + 1) * tm // n):
                dispatch_row(row)
        return f

    hb_ref[...] = h.astype(BF16)
    n_a = len(MIXER_COL_GROUPS[0])
    emit(proj_pieces(0), [dispatch_chunk(i, n_a) for i in range(n_a)])

    rr = lax.broadcasted_iota(I32, (CHUNK, CHUNK), 0)
    cc = lax.broadcasted_iota(I32, (CHUNK, CHUNK), 1)
    wtril = [jnp.where(rr >= cc, r["a_ws"][g], 0.0).astype(BF16) for g in range(A_GROUPS)]
    cg = MIX_W // A_GROUPS

    def a_chunk(c0):
        def f():
            u = jax.nn.gelu(proj_ref[c0:c0 + CHUNK, 0:MIX_W])
            v = _ln(jax.nn.gelu(proj_ref[c0:c0 + CHUNK, MIX_W:2 * MIX_W]), r["a_ln_g"][...], r["a_ln_b"][...])
            vb = v.astype(BF16)
            mixed = jnp.concatenate(
                [jnp.dot(wtril[g], vb[:, g * cg:(g + 1) * cg], preferred_element_type=F32)
                 for g in range(A_GROUPS)], axis=1) + r["a_bs_full"][...]
            ys_ref[0, c0:c0 + CHUNK, :] = (u * mixed).astype(BF16)
        return f

    emit([a_chunk(c0) for c0 in range(0, tm, CHUNK)], [])
    emit(proj_pieces(1), [])

    cbw_ref = r["conv_b_w8"]

    def b_prep(c0):
        def f():
            cx_ext[CTX_B + c0:CTX_B + c0 + CHUNK, :] = _conv_in(
                proj_ref[c0:c0 + CHUNK, 3 * MIX_W:4 * MIX_W] * proj_ref[c0:c0 + CHUNK, 4 * MIX_W:5 * MIX_W])
        return f

    def b_block(r0):
        def f():
            acc = None
            for k in range(CONV_B):
                t = _tap(cbw_ref, k, cx_ext[pl.ds(r0 + CTX_B - (CONV_B - 1) + k, CONV_ROWS), :])
                acc = t if acc is None else acc + t
            ys_ref[1, r0:r0 + CONV_ROWS, :] = (proj_ref[r0:r0 + CONV_ROWS, 2 * MIX_W:3 * MIX_W] * acc).astype(BF16)
        return f

    def b_end():
        tail = pl.ds(tm - (CONV_B - 1), CONV_B - 1)
        r["new_b"][0] = proj_ref[tail, 3 * MIX_W:4 * MIX_W] * proj_ref[tail, 4 * MIX_W:5 * MIX_W]
        cx_ext[0:CTX_B, :] = cx_ext[tm:tm + CTX_B, :]

    emit([b_prep(c0) for c0 in range(0, tm, CHUNK)] + [b_block(r0) for r0 in range(0, tm, CONV_ROWS)] + [b_end],
         merge_pieces(0))
    emit(proj_pieces(2), [])

    pos = j * tm + lax.broadcasted_iota(I32, (tm, 1), 0)
    pg = MIX_W // len(POOL_WINDOWS)

    def c_prep():
        p_ext[CTX_P:CTX_P + tm, :] = proj_ref[:, 5 * MIX_W:6 * MIX_W]

    def c_group(gi, w):
        def f():
            lo = gi * pg
            levels = w.bit_length() - 1
            for lv in range(levels):
                span = 1 << lv
                start = CTX_P if lv == levels - 1 else SUBLANES * (lv + 1)
                rows = CTX_P + tm - start
                if lv == 0:
                    a = p_ext[start:start + rows, lo:lo + pg] + p_ext[pl.ds(start - span, rows), lo:lo + pg]
                else:
                    prev = pool_tmp.at[gi * (gi - 1) // 2 + lv - 1]
                    a = prev[start:start + rows, :] + prev[pl.ds(start - span, rows), :]
                if lv == levels - 1:
                    s = a
                else:
                    pool_tmp[gi * (gi - 1) // 2 + lv, start:start + rows, :] = a
            cnt = jnp.minimum(pos + 1, w).astype(F32)
            pooled = s / cnt - p_ext[CTX_P:CTX_P + tm, lo:lo + pg]
            ys_ref[2, :, lo:lo + pg] = (_bdot(pooled, r["pool_w"][gi]) * r["pool_scale"][:, lo:lo + pg]).astype(BF16)
        return f

    def c_end():
        r["new_p"][0] = p_ext[pl.ds(CTX_P + tm - POOL_CTX, POOL_CTX), :]
        p_ext[0:CTX_P, :] = p_ext[tm:tm + CTX_P, :]

    emit([c_prep] + proj_pieces(3), [])
    emit([c_group(gi, w) for gi, w in enumerate(POOL_WINDOWS)] + [c_end], merge_pieces(1))

    cdw_ref = r["conv_d_w8"]
    off0 = CTX_D - (CONV_D - 1)
    next_tile = jnp.minimum(step + 1, nsteps - 1)

    def d_prep(c0):
        def f():
            gl_ext[CTX_D + c0:CTX_D + c0 + CHUNK, :] = _conv_in(
                proj_ref[c0:c0 + CHUNK, 6 * MIX_W:7 * MIX_W]
                * jax.nn.sigmoid(proj_ref[c0:c0 + CHUNK, 7 * MIX_W:8 * MIX_W]))
        return f

    def d_block(r0):
        def f():
            acc = None
            for rr_ in range(SUBLANES):
                rows = CONV_ROWS if rr_ == 0 else CONV_ROWS + SUBLANES
                y = None
                for q in range(-(-(off0 + CONV_D) // SUBLANES)):
                    k = SUBLANES * q + rr_ - off0
                    if 0 <= k < CONV_D:
                        t = _tap(cdw_ref, k, gl_ext[r0 + SUBLANES * q:r0 + SUBLANES * q + rows, :])
                        y = t if y is None else y + t
                if rr_ > 0:
                    conv_tmp[rr_ - 1] = y
                    y = conv_tmp[rr_ - 1, pl.ds(rr_, CONV_ROWS), :]
                acc = y if acc is None else acc + y
            dc = acc + r["conv_d_b"][...]
            ys_ref[3, r0:r0 + CONV_ROWS, :] = jax.nn.silu(_ln(dc, r["d_ln_g"][...], r["d_ln_b"][...])).astype(BF16)
            if not first:
                for row in range(r0, r0 + CONV_ROWS):
                    gather_row(next_tile, row)
        return f

    def d_end():
        tail = pl.ds(tm - (CONV_D - 1), CONV_D - 1)
        r["new_d"][0] = proj_ref[tail, 6 * MIX_W:7 * MIX_W] * jax.nn.sigmoid(proj_ref[tail, 7 * MIX_W:8 * MIX_W])
        gl_ext[0:CTX_D, :] = gl_ext[tm:tm + CTX_D, :]

    emit([d_prep(c0) for c0 in range(0, tm, CHUNK)] + [d_block(r0) for r0 in range(0, tm, CONV_ROWS)] + [d_end],
         merge_pieces(2))
    for mp in merge_pieces(3):
        mp()

    wait_rows(h2buf, xs_hbm.at[pl.ds(0, tm)], sem_d, 2)
    _merge_tail(x_src, g1, sh2, sc2, r["n2g"][...], lambda rs: m_ref[rs, :], wout_ref, r["w_r"], r["b_r"],
                r["x1"].at[0], h2buf, r["lg"])
    route = _route_rows(r["lg"][...], carry, cap)
    r["route"][0] = route
    r["carry_out"][...] = carry[...]
    _slots_to_smem(route, slot_v, slot_s, sem_s)
    r["slots"][...] = slot_v[...]

    @pl.when(step == nsteps - 1)
    def _():
        pltpu.make_async_copy(slot_v.at[pl.ds(0, 2)], slot_s, sem_s).wait()
        lax.fori_loop(0, tm, lambda i, c: (dispatch_row(i), c)[1], 0, unroll=DMA_UNROLL)
        wait_rows(h2buf, xs_hbm.at[pl.ds(0, tm)], sem_d, 2)
        if not first:
            wait_rows(r["ys_prev"].at[pl.ds(0, tm)], r["gbuf"].at[0], sem_g, 2)


def _const_spec(shape):
    nd = len(shape)
    return pl.BlockSpec(shape, lambda *_: (0,) * nd, pipeline_mode=pl.Buffered(1))


PER_LAYER = ("w_in", "w_branch", "w_out")


def _layer_spec(shape, li):
    nd = len(shape)
    return pl.BlockSpec((1,) + tuple(shape[1:]), lambda *_: (li,) + (0,) * (nd - 1), pipeline_mode=pl.Buffered(1))


def _prompt_mixer(x, mod, lw, li, cap, prev=None):
    n, l, _ = x.shape
    tm = MIX_TM
    first = prev is None
    tok = lambda last: pl.BlockSpec((1, tm, last), lambda b, j, sp: (b, j, 0))
    per_seq = lambda rows, last: pl.BlockSpec((1, rows, last), lambda b, j, sp: (b, 0, 0))
    anyspec = pl.BlockSpec(memory_space=pl.ANY)
    names, ins, in_specs = ["x", "mod", "carry_in"], [x, mod, jnp.zeros((1, LANES), F32)], \
        [tok(D_MODEL), per_seq(1, 6 * D_MODEL), _const_spec((1, LANES))]
    if first:
        slots_prev = jnp.zeros((SUBLANES,), I32)
    else:
        route_prev, mod_prev, ys_prev, slots_prev = prev
        names += ["route_prev", "mod_prev", "ys_prev"]
        ins += [route_prev, mod_prev, ys_prev]
        in_specs += [tok(LANES), per_seq(1, 6 * D_MODEL), anyspec]
    names += list(PROMPT_CONSTS)
    ins += [lw[k] for k in PROMPT_CONSTS]
    in_specs += [_layer_spec(lw[k].shape, li) if k in PER_LAYER else _const_spec(lw[k].shape) for k in PROMPT_CONSTS]

    xs_rows = N_EXPERTS * cap + 2 * tm
    out_names = ["x1", "route", "carry_out", "xs", "slots", "new_b", "new_p", "new_d"]
    out_specs = [tok(D_MODEL), tok(LANES), pl.BlockSpec((1, LANES), lambda b, j, sp: (0, 0)), anyspec,
                 pl.BlockSpec((SUBLANES, tm), lambda b, j, sp: (b * (l // tm) + j, 0)),
                 per_seq(CONV_B - 1, MIX_W), per_seq(POOL_CTX, MIX_W), per_seq(CONV_D - 1, MIX_W)]
    out_shape = [jax.ShapeDtypeStruct((n, l, D_MODEL), F32), jax.ShapeDtypeStruct((n, l, LANES), F32),
                 jax.ShapeDtypeStruct((1, LANES), F32), jax.ShapeDtypeStruct((xs_rows, D_MODEL // 2), U32),
                 jax.ShapeDtypeStruct((n * (l // tm) * SUBLANES, tm), I32),
                 jax.ShapeDtypeStruct((n, CONV_B - 1, MIX_W), F32), jax.ShapeDtypeStruct((n, POOL_CTX, MIX_W), F32),
                 jax.ShapeDtypeStruct((n, CONV_D - 1, MIX_W), F32)]
    scratch = {"hb": pltpu.VMEM((tm, D_MODEL), BF16), "proj": pltpu.VMEM((PROJ_SLABS, tm, MXU_COLS), F32),
               "ys": pltpu.VMEM((N_BRANCH, tm, MIX_W), BF16),
               "cx_ext": pltpu.VMEM((CTX_B + tm, MIX_W), F32), "p_ext": pltpu.VMEM((CTX_P + tm, MIX_W), F32),
               "gl_ext": pltpu.VMEM((CTX_D + tm, MIX_W), F32),
               "pool_tmp": pltpu.VMEM((POOL_TMP_SLOTS, CTX_P + tm, MIX_W // len(POOL_WINDOWS)), F32),
               "conv_tmp": pltpu.VMEM((SUBLANES - 1, CONV_ROWS + SUBLANES, MIX_W), F32),
               "h2buf": pltpu.VMEM((tm, D_MODEL // 2), U32), "lg": pltpu.VMEM((tm, LANES), F32),
               "slot_v": pltpu.VMEM((SUBLANES, tm), I32), "slot_s": pltpu.SMEM((2, tm), I32),
               "carry": pltpu.VMEM((1, LANES), F32), "sem": pltpu.SemaphoreType.DMA((3,))}
    if not first:
        scratch["xin"] = pltpu.VMEM((tm, D_MODEL), F32)
        scratch["gbuf"] = pltpu.VMEM((2, tm, D_MODEL), F32)
    names = tuple(names + out_names + list(scratch))
    outs = pl.pallas_call(
        functools.partial(_prompt_mixer_kernel, names=names, first=first, cap=cap),
        grid_spec=pltpu.PrefetchScalarGridSpec(
            num_scalar_prefetch=1, grid=(n, l // tm), in_specs=in_specs, out_specs=out_specs,
            scratch_shapes=list(scratch.values())),
        out_shape=out_shape,
        compiler_params=pltpu.CompilerParams(dimension_semantics=("arbitrary", "arbitrary"),
                                             vmem_limit_bytes=MIXER_VMEM_BYTES, disable_bounds_checks=True),
        name="prompt_mixer",
    )(slots_prev, *ins)
    return dict(zip(out_names, outs))


SAMPLE_CONSTS = ("n1g", "n2g", "w_in", "a_ln_g", "a_ln_b", "a_w0", "a_b0", "conv_b_w", "pool_w", "pool_scale",
                 "conv_d_w", "conv_d_b", "d_ln_g", "d_ln_b", "w_branch", "w_out", "w_r", "b_r")


def _sample_mixer_kernel(slots_prev, *refs, names, first, cap):
    r = dict(zip(names, refs))
    rows = r["x1"].shape[0]
    hb_ref, proj_ref, ys_ref, m_ref = r["hb"], r["proj"], r["ys"], r["m"]
    h2buf, slot_v, slot_s, carry, sem = r["h2buf"], r["slot_v"], r["slot_s"], r["carry"], r["sem"]
    xs_hbm = r["xs"]
    win_ref, wbr_ref, wout_ref = r["w_in"].at[0], r["w_branch"].at[0], r["w_out"].at[0]
    cbw_ref, cdw_ref, stb_ref, stp_ref, std_ref = r["conv_b_w"], r["conv_d_w"], r["st_b"], r["st_p"], r["st_d"]
    sem_g, sem_d, sem_s = sem.at[0], sem.at[1], sem.at[2]

    mod = r["mod"][...]
    sh1, sc1, g1, sh2, sc2 = (mod[:, i * D_MODEL:(i + 1) * D_MODEL] for i in range(5))
    if first:
        x_src = r["x"]
    else:
        def gather_row(i, c):
            for k in range(2):
                p = slots_prev[k * rows + i]
                pltpu.make_async_copy(r["ys_prev"].at[pl.ds(p, 1)], r["gbuf"].at[k, pl.ds(i, 1)], sem_g).start()
            return c
        lax.fori_loop(0, rows, gather_row, 0, unroll=DMA_UNROLL)
        for _ in range(2):
            pltpu.make_async_copy(r["ys_prev"].at[pl.ds(0, rows)], r["gbuf"].at[0], sem_g).wait()
        x_src = r["xin"]
        x_src[...] = _combine_rows(r["x"][...], r["route_prev"][...], r["mod_prev"][...][:, 5 * D_MODEL:], r["gbuf"])
    h = _rms(x_src[...], r["n1g"][...]) * (1.0 + sc1) + sh1
    _project(h, win_ref, hb_ref, proj_ref)

    u = jax.nn.gelu(proj_ref[:, 0:MIX_W])
    v = _ln(jax.nn.gelu(proj_ref[:, MIX_W:2 * MIX_W]), r["a_ln_g"][...], r["a_ln_b"][...])
    r["v"][...] = v
    ys_ref[0] = (u * (r["a_w0"][...] * v + r["a_b0"][...])).astype(BF16)

    cx = proj_ref[:, 3 * MIX_W:4 * MIX_W] * proj_ref[:, 4 * MIX_W:5 * MIX_W]
    r["cx"][...] = cx
    acc = cbw_ref[CONV_B - 1:CONV_B, :] * _conv_in(cx)
    for k in range(CONV_B - 1):
        acc = acc + cbw_ref[k:k + 1, :] * _conv_in(stb_ref[:, k, :])
    ys_ref[1] = (proj_ref[:, 2 * MIX_W:3 * MIX_W] * acc).astype(BF16)

    pin = proj_ref[:, 5 * MIX_W:6 * MIX_W]
    r["pin"][...] = pin
    pg = MIX_W // len(POOL_WINDOWS)
    mixed = []
    for gi, w in enumerate(POOL_WINDOWS):
        lo = gi * pg
        s = pin[:, lo:lo + pg]
        for i in range(1, w):
            row = POOL_CTX - i
            s = s + stp_ref[:, row, lo:lo + pg]
        pooled = s / float(min(PAST_LEN + 1, w)) - pin[:, lo:lo + pg]
        mixed.append(_bdot(pooled, r["pool_w"][gi]))
    ys_ref[2] = (jnp.concatenate(mixed, axis=1) * r["pool_scale"][...]).astype(BF16)

    gl = proj_ref[:, 6 * MIX_W:7 * MIX_W] * jax.nn.sigmoid(proj_ref[:, 7 * MIX_W:8 * MIX_W])
    r["gl"][...] = gl
    acc = cdw_ref[CONV_D - 1:CONV_D, :] * _conv_in(gl)
    for k in range(CONV_D - 1):
        acc = acc + cdw_ref[k:k + 1, :] * _conv_in(std_ref[:, k, :])
    dc = acc + r["conv_d_b"][...]
    ys_ref[3] = jax.nn.silu(_ln(dc, r["d_ln_g"][...], r["d_ln_b"][...])).astype(BF16)

    for i in range(N_BRANCH):
        _merge_branch(i, hb_ref, win_ref, ys_ref, wbr_ref, m_ref)
    _merge_tail(x_src, g1, sh2, sc2, r["n2g"][...], lambda rs: m_ref[rs, :], wout_ref, r["w_r"], r["b_r"],
                r["x1"], h2buf, r["lg"])

    carry[...] = r["carry_in"][...]
    route = _route_rows(r["lg"][...], carry, cap)
    r["route"][...] = route
    r["carry_out"][...] = carry[...]
    slot_v[...] = jnp.zeros_like(slot_v)
    _slots_to_smem(route, slot_v, slot_s, sem_s)
    r["slots"][...] = slot_v[...]
    pltpu.make_async_copy(slot_v.at[pl.ds(0, 2)], slot_s, sem_s).wait()

    def dispatch_row(i, c):
        for k in range(2):
            pltpu.make_async_copy(h2buf.at[pl.ds(i, 1)], xs_hbm.at[pl.ds(slot_s[k, i], 1)], sem_d).start()
        return c
    lax.fori_loop(0, rows, dispatch_row, 0, unroll=DMA_UNROLL)
    for _ in range(2):
        pltpu.make_async_copy(h2buf, xs_hbm.at[pl.ds(0, rows)], sem_d).wait()


def _sample_mixer(x, mod, st_b, st_p, st_d, lw, li, cap, carry_in, xs, prev=None):
    rows = x.shape[0]
    first = prev is None
    full = lambda a: pl.BlockSpec(a.shape, lambda i, sp: (0,) * a.ndim)
    anyspec = pl.BlockSpec(memory_space=pl.ANY)
    names, ins = ["x", "mod", "carry_in", "xs_in", "st_b", "st_p", "st_d"], [x, mod, carry_in, xs, st_b, st_p, st_d]
    in_specs = [full(x), full(mod), full(carry_in), anyspec, full(st_b), full(st_p), full(st_d)]
    if first:
        slots_prev = jnp.zeros((SUBLANES,), I32)
    else:
        route_prev, mod_prev, ys_prev, slots_prev = prev
        names += ["route_prev", "mod_prev", "ys_prev"]
        ins += [route_prev, mod_prev, ys_prev]
        in_specs += [full(route_prev), full(mod_prev), anyspec]
    names += list(SAMPLE_CONSTS)
    ins += [lw[k] for k in SAMPLE_CONSTS]
    in_specs += [_layer_spec(lw[k].shape, li) if k in PER_LAYER else full(lw[k]) for k in SAMPLE_CONSTS]

    mk = lambda last: jax.ShapeDtypeStruct((rows, last), F32)
    out_names = ["x1", "route", "carry_out", "xs", "slots", "cx", "pin", "gl", "v"]
    out_shape = [mk(D_MODEL), mk(LANES), jax.ShapeDtypeStruct((1, LANES), F32),
                 jax.ShapeDtypeStruct(xs.shape, xs.dtype), jax.ShapeDtypeStruct((SUBLANES, rows), I32),
                 mk(MIX_W), mk(MIX_W), mk(MIX_W), mk(MIX_W)]
    out_specs = [pl.BlockSpec(s.shape, lambda i, sp, nd=len(s.shape): (0,) * nd) for s in out_shape]
    out_specs[3] = anyspec
    scratch = {"hb": pltpu.VMEM((rows, D_MODEL), BF16), "proj": pltpu.VMEM((rows, D_MIX_IN), F32),
               "ys": pltpu.VMEM((N_BRANCH, rows, MIX_W), BF16), "m": pltpu.VMEM((rows, D_MODEL), F32),
               "h2buf": pltpu.VMEM((rows, D_MODEL // 2), U32), "lg": pltpu.VMEM((rows, LANES), F32),
               "slot_v": pltpu.VMEM((SUBLANES, rows), I32), "slot_s": pltpu.SMEM((2, rows), I32),
               "carry": pltpu.VMEM((1, LANES), F32), "sem": pltpu.SemaphoreType.DMA((3,))}
    if not first:
        scratch["xin"] = pltpu.VMEM((rows, D_MODEL), F32)
        scratch["gbuf"] = pltpu.VMEM((2, rows, D_MODEL), F32)
    names = tuple(names + out_names + list(scratch))
    outs = pl.pallas_call(
        functools.partial(_sample_mixer_kernel, names=names, first=first, cap=cap),
        grid_spec=pltpu.PrefetchScalarGridSpec(
            num_scalar_prefetch=1, grid=(1,), in_specs=in_specs, out_specs=out_specs,
            scratch_shapes=list(scratch.values())),
        out_shape=out_shape,
        input_output_aliases={1 + names.index("xs_in"): out_names.index("xs")},
        compiler_params=pltpu.CompilerParams(dimension_semantics=("arbitrary",), vmem_limit_bytes=MIXER_VMEM_BYTES,
                                             disable_bounds_checks=True),
        name="sample_mixer",
    )(slots_prev, *ins)
    return dict(zip(out_names, outs))


def _expert_kernel(vb_ref, be_ref, nv_ref, nu_ref, xs_ref, wg_ref, wu_ref, wd_ref, ys_ref, wgb, wub, wdb):
    b = pl.program_id(0)

    @pl.when(b < nu_ref[0])
    def _():
        @pl.when(jnp.logical_or(b == 0, be_ref[b] != be_ref[jnp.maximum(b - 1, 0)]))
        def _():
            wgb[...] = wg_ref[0, 0].astype(BF16)
            wub[...] = wu_ref[0, 0].astype(BF16)
            wdb[...] = wd_ref[0, 0].astype(BF16)

        valid = lax.broadcasted_iota(I32, (xs_ref.shape[0], 1), 0) < nv_ref[b]
        xb = _unpack_rows(jnp.where(valid, xs_ref[...], jnp.uint32(0)))
        g = jnp.dot(xb, wgb[...], preferred_element_type=F32)
        u = jnp.dot(xb, wub[...], preferred_element_type=F32)
        ys_ref[...] = jnp.dot((jax.nn.silu(g) * u).astype(BF16), wdb[...], preferred_element_type=F32)


def _experts(xs, counts, cap, li, w_eg, w_eu, w_ed):
    bm = MOE_BM
    n_blocks = -(-xs.shape[0] // bm)
    max_used = min(N_EXPERTS * (cap // bm), -(-(cap * 2) // bm) + N_EXPERTS)
    nblk = (counts + bm - 1) // bm
    ends = jnp.cumsum(nblk)
    n_used = ends[-1:]
    b = jnp.arange(max_used, dtype=I32)
    bc = jnp.minimum(b, n_used[0] - 1)
    be = jnp.minimum(jnp.sum(ends[None, :] <= bc[:, None], axis=1), N_EXPERTS - 1).astype(I32)
    mine = be[:, None] == jnp.arange(N_EXPERTS, dtype=I32)[None, :]
    pick = lambda per_expert: jnp.sum(jnp.where(mine, per_expert[None, :], 0), axis=1)
    local = bc - pick(ends - nblk)
    vblock = (be * (cap // bm) + local).astype(I32)
    nvalid = jnp.clip(pick(counts) - local * bm, 0, bm).astype(I32)
    wspec = lambda shape: pl.BlockSpec((1, 1) + shape, lambda i, vb, be_, nv, nu: (li, be_[i], 0, 0))
    return pl.pallas_call(
        _expert_kernel,
        grid_spec=pltpu.PrefetchScalarGridSpec(
            num_scalar_prefetch=4,
            grid=(max_used,),
            in_specs=[pl.BlockSpec((bm, D_MODEL // 2), lambda i, vb, be_, nv, nu: (vb[i], 0)),
                      wspec((D_MODEL, D_EXPERT)), wspec((D_MODEL, D_EXPERT)), wspec((D_EXPERT, D_MODEL))],
            out_specs=pl.BlockSpec((bm, D_MODEL), lambda i, vb, be_, nv, nu: (vb[i], 0)),
            scratch_shapes=[pltpu.VMEM((D_MODEL, D_EXPERT), BF16), pltpu.VMEM((D_MODEL, D_EXPERT), BF16),
                            pltpu.VMEM((D_EXPERT, D_MODEL), BF16)]),
        out_shape=jax.ShapeDtypeStruct((n_blocks * bm, D_MODEL), F32),
        compiler_params=pltpu.CompilerParams(dimension_semantics=("arbitrary",)),
        name="experts",
    )(vblock, be, nvalid, n_used.astype(I32), xs, w_eg, w_eu, w_ed)


def _combine_kernel(pos_ref, ys_hbm, x1_ref, route_ref, g2_ref, fg_ref, out_ref, buf, sem):
    tc = x1_ref.shape[0]
    i = pl.program_id(0)
    n = pl.num_programs(0)
    slot = i % 2

    def issue(tile, s):
        def body(row, carry):
            for k in range(2):
                p = pos_ref[(tile * 2 + k) * tc + row]
                pltpu.make_async_copy(ys_hbm.at[pl.ds(p, 1)], buf.at[s, k, pl.ds(row, 1)], sem.at[s]).start()
            return carry
        lax.fori_loop(0, tc, body, 0, unroll=DMA_UNROLL)

    @pl.when(i == 0)
    def _():
        issue(0, 0)

    @pl.when(i + 1 < n)
    def _():
        issue(i + 1, 1 - slot)

    for k in range(2):
        pltpu.make_async_copy(ys_hbm.at[pl.ds(0, tc)], buf.at[slot, k], sem.at[slot]).wait()
    x2 = _combine_rows(x1_ref[...], route_ref[...], g2_ref[...].reshape(-1, D_MODEL), buf.at[slot])
    out_ref[...] = _rms(x2, fg_ref[...])


def _combine(ys, pos, tc, x1, route, g2, fg):
    t = x1.shape[0]
    if g2.ndim == 3:
        seq = t // g2.shape[0]
        g2spec = pl.BlockSpec((1, 1, D_MODEL), lambda i, pos: (i * tc // seq, 0, 0))
    else:
        g2spec = pl.BlockSpec((tc, D_MODEL), lambda i, pos: (i, 0))
    tokspec = lambda last: pl.BlockSpec((tc, last), lambda i, pos: (i, 0))
    return pl.pallas_call(
        _combine_kernel,
        grid_spec=pltpu.PrefetchScalarGridSpec(
            num_scalar_prefetch=1,
            grid=(t // tc,),
            in_specs=[pl.BlockSpec(memory_space=pl.ANY), tokspec(D_MODEL), tokspec(LANES), g2spec,
                      pl.BlockSpec((1, D_MODEL), lambda i, pos: (0, 0))],
            out_specs=tokspec(D_MODEL),
            scratch_shapes=[pltpu.VMEM((2, 2, tc, D_MODEL), F32), pltpu.SemaphoreType.DMA((2,))]),
        out_shape=jax.ShapeDtypeStruct((t, D_MODEL), F32),
        compiler_params=pltpu.CompilerParams(dimension_semantics=("arbitrary",), disable_bounds_checks=True,
                                             vmem_limit_bytes=MIXER_VMEM_BYTES),
        name="combine",
    )(pos, ys, x1, route, g2, fg)


def _layer_weights(li, norm1_g, norm2_g, w_in, a_ln_g, a_ln_b, a_ws, a_bs, conv_b_w, pool_w, pool_scale,
                   conv_d_w, conv_d_b, d_ln_g, d_ln_b, w_branch, w_out, w_rg, b_rg, w_re, b_re):
    row = lambda a: a[li].reshape(1, -1)
    cg = MIX_W // A_GROUPS
    pad = LANES - N_GROUPS_MOE - N_EXPERTS
    return {
        "n1g": row(norm1_g), "n2g": row(norm2_g), "w_in": w_in.astype(BF16),
        "a_ln_g": row(a_ln_g), "a_ln_b": row(a_ln_b), "a_ws": a_ws[li],
        "a_bs_full": jnp.repeat(a_bs[li].T, cg, axis=1),
        "a_w0": jnp.repeat(a_ws[li, :, 0, 0], cg).reshape(1, MIX_W),
        "a_b0": jnp.repeat(a_bs[li, :, 0], cg).reshape(1, MIX_W),
        "conv_b_w": conv_b_w[li], "pool_w": pool_w[li], "pool_scale": row(pool_scale),
        "conv_b_w8": jnp.broadcast_to(conv_b_w[li][:, None, :], (CONV_B, SUBLANES, MIX_W)),
        "conv_d_w8": jnp.broadcast_to(conv_d_w[li][:, None, :], (CONV_D, SUBLANES, MIX_W)),
        "conv_d_w": conv_d_w[li], "conv_d_b": row(conv_d_b), "d_ln_g": row(d_ln_g), "d_ln_b": row(d_ln_b),
        "w_branch": w_branch.astype(BF16), "w_out": w_out.astype(BF16),
        "w_r": jnp.concatenate([w_rg[li], w_re[li], jnp.zeros((D_MODEL, pad), F32)], axis=1),
        "b_r": jnp.concatenate([b_rg[li], b_re[li], jnp.zeros((pad,), F32)]).reshape(1, LANES),
    }


def _slots(slot_rows):
    return slot_rows.reshape(-1, SUBLANES, slot_rows.shape[-1])[:, :2].reshape(-1)


def kernel(x_prompt, x_sample, state_conv_b, state_pool, state_conv_d, c_prompt, c_sample, w_mod, b_mod, norm1_g, norm2_g, w_in, a_ln_g, a_ln_b, a_ws, a_bs, conv_b_w, pool_w, pool_scale, conv_d_w, conv_d_b, d_ln_g, d_ln_b, w_branch, w_out, w_rg, b_rg, w_re, b_re, w_eg, w_eu, w_ed, final_g):
    depth = w_mod.shape[0]
    nb, seq, _ = x_prompt.shape
    ns = x_sample.shape[0]
    t = nb * seq
    cap = -(-(t + ns) // MOE_BM) * MOE_BM
    mod_all = _modulation(jnp.concatenate([c_prompt, c_sample], axis=0), w_mod, b_mod)
    fg = final_g.reshape(1, D_MODEL)

    xp = x_prompt
    xs_tok = x_sample.reshape(ns, D_MODEL)
    prev_p = prev_s = None
    outs = {k: [] for k in ("bp", "pp", "dp", "bs", "ps", "ds", "vs")}
    for li in range(depth):
        lw = _layer_weights(li, norm1_g, norm2_g, w_in, a_ln_g, a_ln_b, a_ws, a_bs, conv_b_w, pool_w, pool_scale,
                            conv_d_w, conv_d_b, d_ln_g, d_ln_b, w_branch, w_out, w_rg, b_rg, w_re, b_re)
        mod_p = mod_all[li, :nb].reshape(nb, 1, 6 * D_MODEL)
        mod_s = mod_all[li, nb:]

        pm = _prompt_mixer(xp, mod_p, lw, li, cap, prev_p)
        outs["bp"].append(pm["new_b"])
        outs["pp"].append(pm["new_p"])
        outs["dp"].append(pm["new_d"])

        st_b, st_p, st_d = state_conv_b[li], state_pool[li], state_conv_d[li]
        sm = _sample_mixer(xs_tok, mod_s, st_b, st_p, st_d, lw, li, cap,
                           pm["carry_out"], pm["xs"], prev_s)
        outs["bs"].append(jnp.concatenate([st_b[:, 1:], sm["cx"][:, None]], axis=1))
        outs["ps"].append(jnp.concatenate([st_p[:, 1:], sm["pin"][:, None]], axis=1))
        outs["ds"].append(jnp.concatenate([st_d[:, 1:], sm["gl"][:, None]], axis=1))
        outs["vs"].append(sm["v"][:, None])

        counts = sm["carry_out"][0, ROUTE_LANE0:ROUTE_LANE0 + N_EXPERTS].astype(I32)
        ys = _experts(sm["xs"], counts, cap, li, w_eg, w_eu, w_ed)
        route_p = pm["route"]
        prev_p = (route_p, mod_p, ys, _slots(pm["slots"]))
        prev_s = (sm["route"], mod_s, ys, _slots(sm["slots"]))
        xp, xs_tok = pm["x1"], sm["x1"]

    route_p, mod_p, ys, slots_p = prev_p
    route_s, mod_s, _, slots_s = prev_s
    y_p = _combine(ys, slots_p, MIX_TM, xp.reshape(t, D_MODEL), route_p.reshape(t, LANES), mod_p[:, :, 5 * D_MODEL:], fg)
    y_s = _combine(ys, slots_s, ns, xs_tok, route_s, mod_s[:, 5 * D_MODEL:], fg)

    stack = lambda k: jnp.stack(outs[k])
    return (y_p.reshape(nb, seq, D_MODEL), y_s.reshape(ns, 1, D_MODEL), stack("bp"), stack("pp"), stack("dp"),
            stack("bs"), stack("ps"), stack("ds"), stack("vs"))
```

```python
import functools

import jax
import jax.numpy as jnp
from jax import lax
from jax.experimental import pallas as pl
from jax.experimental.pallas import tpu as pltpu

F32 = jnp.float32
BF16 = jnp.bfloat16
I32 = jnp.int32
U32 = jnp.uint32

D_MODEL = 1024
MIX_W = D_MODEL // 2
CHUNK = 128
A_GROUPS = 4
CONV_B = 3
POOL_WINDOWS = (2, 4, 8, 16)
POOL_CTX = 15
CONV_D = 31
N_BRANCH = 4
N_GROUPS_MOE = 4
EXPERTS_PER_GROUP = 8
N_EXPERTS = N_GROUPS_MOE * EXPERTS_PER_GROUP
D_EXPERT = D_MODEL // 4
EPS = 1e-6
D_MIX_IN = 8 * MIX_W
D_IN = D_MIX_IN + N_BRANCH * D_MODEL
PAST_LEN = 16384

LANES = 128
SUBLANES = 8
ROUTE_LANE0 = N_GROUPS_MOE
ROUTE_E, ROUTE_SLOT, ROUTE_W = 0, 2, 4

MIX_TM = 512
CONV_ROWS = 32
TAIL_ROWS = 256
MXU_COLS = 256
CTX_B = 8
CTX_P = 32
CTX_D = 32
POOL_TMP_SLOTS = sum(w.bit_length() - 2 for w in POOL_WINDOWS)
MOE_BM = 512
DMA_UNROLL = 8
MIXER_VMEM_BYTES = 56 * 1024 * 1024


def _rms(x, g):
    return x * lax.rsqrt(jnp.mean(x * x, axis=-1, keepdims=True) + EPS) * g


def _ln(x, g, b):
    mu = jnp.mean(x, axis=-1, keepdims=True)
    xc = x - mu
    return xc * lax.rsqrt(jnp.mean(xc * xc, axis=-1, keepdims=True) + EPS) * g + b


def _conv_in(x):
    return x.astype(BF16).astype(F32)


def _pack_rows(x):
    c = x.shape[1] // 2
    bits = lambda a: lax.bitcast_convert_type(a.astype(BF16).astype(F32), U32)
    return (bits(x[:, :c]) >> 16) | (bits(x[:, c:]) & jnp.uint32(0xFFFF0000))


def _unpack_rows(p):
    lo = lax.bitcast_convert_type(p << 16, F32)
    hi = lax.bitcast_convert_type(p & jnp.uint32(0xFFFF0000), F32)
    return jnp.concatenate([lo, hi], axis=1).astype(BF16)


def _bdot(a, b):
    return jnp.dot(a.astype(BF16), b.astype(BF16), preferred_element_type=F32)


def _mod_kernel(c_ref, w_ref, b_ref, o_ref):
    o_ref[0] = _bdot(c_ref[...], w_ref[0]) + b_ref[0]


def _modulation(c_all, w_mod, b_mod):
    depth = w_mod.shape[0]
    rows = c_all.shape[0]
    tn = 1536
    return pl.pallas_call(
        _mod_kernel,
        grid=(depth, 6 * D_MODEL // tn),
        in_specs=[pl.BlockSpec((rows, D_MODEL), lambda l, n: (0, 0)),
                  pl.BlockSpec((1, D_MODEL, tn), lambda l, n: (l, 0, n)),
                  pl.BlockSpec((1, 1, tn), lambda l, n: (l, 0, n))],
        out_specs=pl.BlockSpec((1, rows, tn), lambda l, n: (l, 0, n)),
        out_shape=jax.ShapeDtypeStruct((depth, rows, 6 * D_MODEL), F32),
        name="modulation",
    )(c_all, w_mod, b_mod.reshape(depth, 1, 6 * D_MODEL))


def _project(h, win_ref, hb_ref, proj_ref, between=None):
    hb_ref[...] = h.astype(BF16)
    for i, c in enumerate(range(0, D_MIX_IN, D_MODEL)):
        proj_ref[:, c:c + D_MODEL] = jnp.dot(hb_ref[...], win_ref[:, c:c + D_MODEL], preferred_element_type=F32)
        if between is not None:
            between(i)


def _merge_branch(i, hb_ref, win_ref, ys_ref, wbr_ref, m_ref):
    lo = D_MIX_IN + i * D_MODEL
    rows = hb_ref.shape[0]
    for r0 in range(0, rows, TAIL_ROWS):
        rs = slice(r0, min(r0 + TAIL_ROWS, rows))
        gates = jnp.dot(hb_ref[rs, :], win_ref[:, lo:lo + D_MODEL], preferred_element_type=F32)
        br = jnp.dot(ys_ref[i, rs, :], wbr_ref[i], preferred_element_type=F32)
        t = jax.nn.sigmoid(gates) * br
        m_ref[rs, :] = t if i == 0 else m_ref[rs, :] + t


def _merge_tail(x_ref, g1, sh2, sc2, n2g, m_rows, wout_ref, wr_ref, br_ref, x1_ref, h2_ref, lg_ref):
    rows = x1_ref.shape[0]
    for r0 in range(0, rows, TAIL_ROWS):
        rs = slice(r0, min(r0 + TAIL_ROWS, rows))
        per_row = lambda a: a if a.shape[0] == 1 else a[rs]
        out = jnp.dot(m_rows(rs).astype(BF16), wout_ref[...], preferred_element_type=F32)
        x1 = x_ref[rs, :] + per_row(g1) * out
        h2 = _rms(x1, n2g) * (1.0 + per_row(sc2)) + per_row(sh2)
        x1_ref[rs, :] = x1
        h2_ref[rs, :] = _pack_rows(h2)
        lg_ref[rs, :] = _bdot(h2, wr_ref[...]) + br_ref[...]


def _route_rows(lg, carry_ref, cap):
    tr = lg.shape[0]
    lane = lax.broadcasted_iota(I32, (tr, LANES), 1)
    lanef = lane.astype(F32)

    def first_argmax(vals):
        mx = jnp.max(vals, axis=-1, keepdims=True)
        return mx, jnp.min(jnp.where(vals == mx, lanef, float(LANES)), axis=-1, keepdims=True)

    neg = -jnp.inf
    gl = jnp.where(lane < N_GROUPS_MOE, lg, neg)
    gmax, gsel = first_argmax(gl)
    p_g = 1.0 / jnp.sum(jnp.exp(gl - gmax), axis=-1, keepdims=True)
    e_lo = ROUTE_LANE0 + EXPERTS_PER_GROUP * gsel
    emask = jnp.logical_and(lanef >= e_lo, lanef < e_lo + EXPERTS_PER_GROUP)
    el = jnp.where(emask, lg, neg)
    emax = jnp.max(el, axis=-1, keepdims=True)
    ex = jnp.exp(el - emax)
    prob = jnp.where(emask, ex / jnp.sum(ex, axis=-1, keepdims=True), -1.0)
    p1, i1 = first_argmax(prob)
    p2, i2 = first_argmax(jnp.where(lanef == i1, -1.0, prob))
    w1 = p_g * p1 / (p1 + p2)
    w2 = p_g * p2 / (p1 + p2)

    hot1 = lanef == i1
    hot2 = lanef == i2
    hot = jnp.logical_or(hot1, hot2).astype(F32)
    rr = lax.broadcasted_iota(I32, (tr, tr), 0)
    cc = lax.broadcasted_iota(I32, (tr, tr), 1)
    before = jnp.dot((rr > cc).astype(BF16), hot.astype(BF16), preferred_element_type=F32) + carry_ref[...]
    rank1 = jnp.sum(jnp.where(hot1, before, 0.0), axis=-1, keepdims=True)
    rank2 = jnp.sum(jnp.where(hot2, before, 0.0), axis=-1, keepdims=True)
    carry_ref[...] = carry_ref[...] + jnp.sum(hot, axis=0, keepdims=True)

    e1 = i1 - ROUTE_LANE0
    e2 = i2 - ROUTE_LANE0
    fields = (e1, e2, e1 * float(cap) + rank1, e2 * float(cap) + rank2, w1, w2)
    out = jnp.zeros((tr, LANES), F32)
    for k, f in enumerate(fields):
        out = jnp.where(lane == k, f, out)
    return out


def _slots_to_smem(route, slot_v, slot_s, sem):
    slot_v[0:2, :] = jnp.transpose(route)[ROUTE_SLOT:ROUTE_SLOT + 2, :].astype(I32)
    pltpu.make_async_copy(slot_v.at[pl.ds(0, 2)], slot_s, sem).start()


def _combine_rows(x1_prev, route_prev, g2, gbuf):
    return x1_prev + g2 * (route_prev[:, ROUTE_W:ROUTE_W + 1] * gbuf[0]
                           + route_prev[:, ROUTE_W + 1:ROUTE_W + 2] * gbuf[1])


MIXER_COL_GROUPS = tuple(range(lo * MIX_W // MXU_COLS, hi * MIX_W // MXU_COLS) for lo, hi in ((0, 2), (2, 5), (5, 6), (6, 8)))
PROJ_SLABS = max(len(g) for g in MIXER_COL_GROUPS)
PROJ_SLAB_OF = {g: g - groups[0] for groups in MIXER_COL_GROUPS for g in groups}


class _ProjCols:
    def __init__(self, ref):
        self.ref = ref

    def slab(self, col):
        return self.ref.at[PROJ_SLAB_OF[col // MXU_COLS]]

    def __getitem__(self, idx):
        rows, cols = idx
        lo, hi = cols.start, cols.stop
        parts = []
        while lo < hi:
            g = lo // MXU_COLS
            end = min(hi, (g + 1) * MXU_COLS)
            parts.append(self.slab(lo)[rows, lo - g * MXU_COLS:end - g * MXU_COLS])
            lo = end
        return parts[0] if len(parts) == 1 else jnp.concatenate(parts, axis=1)


def _tap(w8_ref, k, x):
    rows, c = x.shape
    return (x.reshape(rows // SUBLANES, SUBLANES, c) * w8_ref[k][None]).reshape(rows, c)


PROMPT_CONSTS = ("n1g", "n2g", "w_in", "a_ln_g", "a_ln_b", "a_ws", "a_bs_full", "conv_b_w8", "pool_w", "pool_scale",
                 "conv_d_w8", "conv_d_b", "d_ln_g", "d_ln_b", "w_branch", "w_out", "w_r", "b_r")


def _prompt_mixer_kernel(slots_prev, *refs, names, first, cap):
    r = dict(zip(names, refs))
    tm = r["x1"].shape[1]
    hb_ref, proj_ref, ys_ref = r["hb"], _ProjCols(r["proj"]), r["ys"]
    m_ref = r["x1"].at[0]
    cx_ext, p_ext, gl_ext, pool_tmp, conv_tmp = r["cx_ext"], r["p_ext"], r["gl_ext"], r["pool_tmp"], r["conv_tmp"]
    h2buf, slot_v, slot_s, carry, sem = r["h2buf"], r["slot_v"], r["slot_s"], r["carry"], r["sem"]
    xs_hbm = r["xs"]
    win_ref, wbr_ref, wout_ref = r["w_in"].at[0], r["w_branch"].at[0], r["w_out"].at[0]
    j = pl.program_id(1)
    nj = pl.num_programs(1)
    step = pl.program_id(0) * nj + j
    nsteps = pl.num_programs(0) * nj
    trash = N_EXPERTS * cap
    sem_g, sem_d, sem_s = sem.at[0], sem.at[1], sem.at[2]

    def gather_row(tile, row):
        for k in range(2):
            p = slots_prev[(tile * 2 + k) * tm + row]
            pltpu.make_async_copy(r["ys_prev"].at[pl.ds(p, 1)], r["gbuf"].at[k, pl.ds(row, 1)], sem_g).start()

    def dispatch_row(row):
        for k in range(2):
            pltpu.make_async_copy(h2buf.at[pl.ds(row, 1)], xs_hbm.at[pl.ds(slot_s[k, row], 1)], sem_d).start()

    def wait_rows(src, dst, sem_, times):
        for _ in range(times):
            pltpu.make_async_copy(src, dst, sem_).wait()

    @pl.when(step == 0)
    def _():
        carry[...] = r["carry_in"][...]
        h2buf[...] = jnp.zeros_like(h2buf)
        slot_v[...] = (trash + tm * lax.broadcasted_iota(I32, slot_v.shape, 0)
                       + lax.broadcasted_iota(I32, slot_v.shape, 1))
        pltpu.make_async_copy(slot_v.at[pl.ds(0, 2)], slot_s, sem_s).start()
        if not first:
            lax.fori_loop(0, tm, lambda i, c: (gather_row(0, i), c)[1], 0, unroll=DMA_UNROLL)

    @pl.when(j == 0)
    def _():
        cx_ext[0:CTX_B, :] = jnp.zeros((CTX_B, MIX_W), F32)
        p_ext[0:CTX_P, :] = jnp.zeros((CTX_P, MIX_W), F32)
        gl_ext[0:CTX_D, :] = jnp.zeros((CTX_D, MIX_W), F32)

    mod = r["mod"][0]
    sh1, sc1, g1, sh2, sc2 = (mod[:, i * D_MODEL:(i + 1) * D_MODEL] for i in range(5))
    if first:
        x_src = r["x"].at[0]
    else:
        wait_rows(r["ys_prev"].at[pl.ds(0, tm)], r["gbuf"].at[0], sem_g, 2)
        x_src = r["xin"]
        x_src[...] = _combine_rows(r["x"][0], r["route_prev"][0], r["mod_prev"][0][:, 5 * D_MODEL:], r["gbuf"])
    h = _rms(x_src[...], r["n1g"][...]) * (1.0 + sc1) + sh1

    pltpu.make_async_copy(slot_v.at[pl.ds(0, 2)], slot_s, sem_s).wait()

    def proj_piece(c):
        def f():
            proj_ref.slab(c)[...] = jnp.dot(hb_ref[...], win_ref[:, c:c + MXU_COLS], preferred_element_type=F32)
        return f

    def proj_pieces(mixer):
        return [proj_piece(g * MXU_COLS) for g in MIXER_COL_GROUPS[mixer]]

    def merge_piece(i, c):
        def f():
            lo = D_MIX_IN + i * D_MODEL + c
            gates = jnp.dot(hb_ref[...], win_ref[:, lo:lo + MXU_COLS], preferred_element_type=F32)
            br = jnp.dot(ys_ref[i], wbr_ref[i, :, c:c + MXU_COLS], preferred_element_type=F32)
            t = jax.nn.sigmoid(gates) * br
            m_ref[:, c:c + MXU_COLS] = t if i == 0 else m_ref[:, c:c + MXU_COLS] + t
        return f

    def merge_pieces(i):
        return [merge_piece(i, c) for c in range(0, D_MODEL, MXU_COLS)]

    def emit(vector_pieces, matrix_pieces):
        nv, nm = len(vector_pieces), len(matrix_pieces)
        for i, vp in enumerate(vector_pieces):
            vp()
            for mp in matrix_pieces[i * nm // nv:(i + 1) * nm // nv]:
                mp()

    def dispatch_chunk(i, n):
        def f():
            for row in range(i * tm // n, (i + 1) * tm // n):
                dispatch_row(row)
        return f

    hb_ref[...] = h.astype(BF16)
    n_a = len(MIXER_COL_GROUPS[0])
    emit(proj_pieces(0), [dispatch_chunk(i, n_a) for i in range(n_a)])

    rr = lax.broadcasted_iota(I32, (CHUNK, CHUNK), 0)
    cc = lax.broadcasted_iota(I32, (CHUNK, CHUNK), 1)
    wtril = [jnp.where(rr >= cc, r["a_ws"][g], 0.0).astype(BF16) for g in range(A_GROUPS)]
    cg = MIX_W // A_GROUPS

    def a_chunk(c0):
        def f():
            u = jax.nn.gelu(proj_ref[c0:c0 + CHUNK, 0:MIX_W])
            v = _ln(jax.nn.gelu(proj_ref[c0:c0 + CHUNK, MIX_W:2 * MIX_W]), r["a_ln_g"][...], r["a_ln_b"][...])
            vb = v.astype(BF16)
            mixed = jnp.concatenate(
                [jnp.dot(wtril[g], vb[:, g * cg:(g + 1) * cg], preferred_element_type=F32)
                 for g in range(A_GROUPS)], axis=1) + r["a_bs_full"][...]
            ys_ref[0, c0:c0 + CHUNK, :] = (u * mixed).astype(BF16)
        return f

    emit([a_chunk(c0) for c0 in range(0, tm, CHUNK)], [])
    emit(proj_pieces(1), [])

    cbw_ref = r["conv_b_w8"]

    def b_prep(c0):
        def f():
            cx_ext[CTX_B + c0:CTX_B + c0 + CHUNK, :] = _conv_in(
                proj_ref[c0:c0 + CHUNK, 3 * MIX_W:4 * MIX_W] * proj_ref[c0:c0 + CHUNK, 4 * MIX_W:5 * MIX_W])
        return f

    def b_block(r0):
        def f():
            acc = None
            for k in range(CONV_B):
                t = _tap(cbw_ref, k, cx_ext[pl.ds(r0 + CTX_B - (CONV_B - 1) + k, CONV_ROWS), :])
                acc = t if acc is None else acc + t
            ys_ref[1, r0:r0 + CONV_ROWS, :] = (proj_ref[r0:r0 + CONV_ROWS, 2 * MIX_W:3 * MIX_W] * acc).astype(BF16)
        return f

    def b_end():
        tail = pl.ds(tm - (CONV_B - 1), CONV_B - 1)
        r["new_b"][0] = proj_ref[tail, 3 * MIX_W:4 * MIX_W] * proj_ref[tail, 4 * MIX_W:5 * MIX_W]
        cx_ext[0:CTX_B, :] = cx_ext[tm:tm + CTX_B, :]

    emit([b_prep(c0) for c0 in range(0, tm, CHUNK)] + [b_block(r0) for r0 in range(0, tm, CONV_ROWS)] + [b_end],
         merge_pieces(0))
    emit(proj_pieces(2), [])

    pos = j * tm + lax.broadcasted_iota(I32, (tm, 1), 0)
    pg = MIX_W // len(POOL_WINDOWS)

    def c_prep():
        p_ext[CTX_P:CTX_P + tm, :] = proj_ref[:, 5 * MIX_W:6 * MIX_W]

    def c_group(gi, w):
        def f():
            lo = gi * pg
            levels = w.bit_length() - 1
            for lv in range(levels):
                span = 1 << lv
                start = CTX_P if lv == levels - 1 else SUBLANES * (lv + 1)
                rows = CTX_P + tm - start
                if lv == 0:
                    a = p_ext[start:start + rows, lo:lo + pg] + p_ext[pl.ds(start - span, rows), lo:lo + pg]
                else:
                    prev = pool_tmp.at[gi * (gi - 1) // 2 + lv - 1]
                    a = prev[start:start + rows, :] + prev[pl.ds(start - span, rows), :]
                if lv == levels - 1:
                    s = a
                else:
                    pool_tmp[gi * (gi - 1) // 2 + lv, start:start + rows, :] = a
            cnt = jnp.minimum(pos + 1, w).astype(F32)
            pooled = s / cnt - p_ext[CTX_P:CTX_P + tm, lo:lo + pg]
            ys_ref[2, :, lo:lo + pg] = (_bdot(pooled, r["pool_w"][gi]) * r["pool_scale"][:, lo:lo + pg]).astype(BF16)
        return f

    def c_end():
        r["new_p"][0] = p_ext[pl.ds(CTX_P + tm - POOL_CTX, POOL_CTX), :]
        p_ext[0:CTX_P, :] = p_ext[tm:tm + CTX_P, :]

    emit([c_prep] + proj_pieces(3), [])
    emit([c_group(gi, w) for gi, w in enumerate(POOL_WINDOWS)] + [c_end], merge_pieces(1))

    cdw_ref = r["conv_d_w8"]
    off0 = CTX_D - (CONV_D - 1)
    next_tile = jnp.minimum(step + 1, nsteps - 1)

    def d_prep(c0):
        def f():
            gl_ext[CTX_D + c0:CTX_D + c0 + CHUNK, :] = _conv_in(
                proj_ref[c0:c0 + CHUNK, 6 * MIX_W:7 * MIX_W]
                * jax.nn.sigmoid(proj_ref[c0:c0 + CHUNK, 7 * MIX_W:8 * MIX_W]))
        return f

    def d_block(r0):
        def f():
            acc = None
            for rr_ in range(SUBLANES):
                rows = CONV_ROWS if rr_ == 0 else CONV_ROWS + SUBLANES
                y = None
                for q in range(-(-(off0 + CONV_D) // SUBLANES)):
                    k = SUBLANES * q + rr_ - off0
                    if 0 <= k < CONV_D:
                        t = _tap(cdw_ref, k, gl_ext[r0 + SUBLANES * q:r0 + SUBLANES * q + rows, :])
                        y = t if y is None else y + t
                if rr_ > 0:
                    conv_tmp[rr_ - 1] = y
                    y = conv_tmp[rr_ - 1, pl.ds(rr_, CONV_ROWS), :]
                acc = y if acc is None else acc + y
            dc = acc + r["conv_d_b"][...]
            ys_ref[3, r0:r0 + CONV_ROWS, :] = jax.nn.silu(_ln(dc, r["d_ln_g"][...], r["d_ln_b"][...])).astype(BF16)
            if not first:
                for row in range(r0, r0 + CONV_ROWS):
                    gather_row(next_tile, row)
        return f

    def d_end():
        tail = pl.ds(tm - (CONV_D - 1), CONV_D - 1)
        r["new_d"][0] = proj_ref[tail, 6 * MIX_W:7 * MIX_W] * jax.nn.sigmoid(proj_ref[tail, 7 * MIX_W:8 * MIX_W])
        gl_ext[0:CTX_D, :] = gl_ext[tm:tm + CTX_D, :]

    emit([d_prep(c0) for c0 in range(0, tm, CHUNK)] + [d_block(r0) for r0 in range(0, tm, CONV_ROWS)] + [d_end],
         merge_pieces(2))
    for mp in merge_pieces(3):
        mp()

    wait_rows(h2buf, xs_hbm.at[pl.ds(0, tm)], sem_d, 2)
    _merge_tail(x_src, g1, sh2, sc2, r["n2g"][...], lambda rs: m_ref[rs, :], wout_ref, r["w_r"], r["b_r"],
                r["x1"].at[0], h2buf, r["lg"])
    route = _route_rows(r["lg"][...], carry, cap)
    r["route"][0] = route
    r["carry_out"][...] = carry[...]
    _slots_to_smem(route, slot_v, slot_s, sem_s)
    r["slots"][...] = slot_v[...]

    @pl.when(step == nsteps - 1)
    def _():
        pltpu.make_async_copy(slot_v.at[pl.ds(0, 2)], slot_s, sem_s).wait()
        lax.fori_loop(0, tm, lambda i, c: (dispatch_row(i), c)[1], 0, unroll=DMA_UNROLL)
        wait_rows(h2buf, xs_hbm.at[pl.ds(0, tm)], sem_d, 2)
        if not first:
            wait_rows(r["ys_prev"].at[pl.ds(0, tm)], r["gbuf"].at[0], sem_g, 2)


def _const_spec(shape):
    nd = len(shape)
    return pl.BlockSpec(shape, lambda *_: (0,) * nd, pipeline_mode=pl.Buffered(1))


PER_LAYER = ("w_in", "w_branch", "w_out")


def _layer_spec(shape, li):
    nd = len(shape)
    return pl.BlockSpec((1,) + tuple(shape[1:]), lambda *_: (li,) + (0,) * (nd - 1), pipeline_mode=pl.Buffered(1))


def _prompt_mixer(x, mod, lw, li, cap, prev=None):
    n, l, _ = x.shape
    tm = MIX_TM
    first = prev is None
    tok = lambda last: pl.BlockSpec((1, tm, last), lambda b, j, sp: (b, j, 0))
    per_seq = lambda rows, last: pl.BlockSpec((1, rows, last), lambda b, j, sp: (b, 0, 0))
    anyspec = pl.BlockSpec(memory_space=pl.ANY)
    names, ins, in_specs = ["x", "mod", "carry_in"], [x, mod, jnp.zeros((1, LANES), F32)], \
        [tok(D_MODEL), per_seq(1, 6 * D_MODEL), _const_spec((1, LANES))]
    if first:
        slots_prev = jnp.zeros((SUBLANES,), I32)
    else:
        route_prev, mod_prev, ys_prev, slots_prev = prev
        names += ["route_prev", "mod_prev", "ys_prev"]
        ins += [route_prev, mod_prev, ys_prev]
        in_specs += [tok(LANES), per_seq(1, 6 * D_MODEL), anyspec]
    names += list(PROMPT_CONSTS)
    ins += [lw[k] for k in PROMPT_CONSTS]
    in_specs += [_layer_spec(lw[k].shape, li) if k in PER_LAYER else _const_spec(lw[k].shape) for k in PROMPT_CONSTS]

    xs_rows = N_EXPERTS * cap + 2 * tm
    out_names = ["x1", "route", "carry_out", "xs", "slots", "new_b", "new_p", "new_d"]
    out_specs = [tok(D_MODEL), tok(LANES), pl.BlockSpec((1, LANES), lambda b, j, sp: (0, 0)), anyspec,
                 pl.BlockSpec((SUBLANES, tm), lambda b, j, sp: (b * (l // tm) + j, 0)),
                 per_seq(CONV_B - 1, MIX_W), per_seq(POOL_CTX, MIX_W), per_seq(CONV_D - 1, MIX_W)]
    out_shape = [jax.ShapeDtypeStruct((n, l, D_MODEL), F32), jax.ShapeDtypeStruct((n, l, LANES), F32),
                 jax.ShapeDtypeStruct((1, LANES), F32), jax.ShapeDtypeStruct((xs_rows, D_MODEL // 2), U32),
                 jax.ShapeDtypeStruct((n * (l // tm) * SUBLANES, tm), I32),
                 jax.ShapeDtypeStruct((n, CONV_B - 1, MIX_W), F32), jax.ShapeDtypeStruct((n, POOL_CTX, MIX_W), F32),
                 jax.ShapeDtypeStruct((n, CONV_D - 1, MIX_W), F32)]
    scratch = {"hb": pltpu.VMEM((tm, D_MODEL), BF16), "proj": pltpu.VMEM((PROJ_SLABS, tm, MXU_COLS), F32),
               "ys": pltpu.VMEM((N_BRANCH, tm, MIX_W), BF16),
               "cx_ext": pltpu.VMEM((CTX_B + tm, MIX_W), F32), "p_ext": pltpu.VMEM((CTX_P + tm, MIX_W), F32),
               "gl_ext": pltpu.VMEM((CTX_D + tm, MIX_W), F32),
               "pool_tmp": pltpu.VMEM((POOL_TMP_SLOTS, CTX_P + tm, MIX_W // len(POOL_WINDOWS)), F32),
               "conv_tmp": pltpu.VMEM((SUBLANES - 1, CONV_ROWS + SUBLANES, MIX_W), F32),
               "h2buf": pltpu.VMEM((tm, D_MODEL // 2), U32), "lg": pltpu.VMEM((tm, LANES), F32),
               "slot_v": pltpu.VMEM((SUBLANES, tm), I32), "slot_s": pltpu.SMEM((2, tm), I32),
               "carry": pltpu.VMEM((1, LANES), F32), "sem": pltpu.SemaphoreType.DMA((3,))}
    if not first:
        scratch["xin"] = pltpu.VMEM((tm, D_MODEL), F32)
        scratch["gbuf"] = pltpu.VMEM((2, tm, D_MODEL), F32)
    names = tuple(names + out_names + list(scratch))
    outs = pl.pallas_call(
        functools.partial(_prompt_mixer_kernel, names=names, first=first, cap=cap),
        grid_spec=pltpu.PrefetchScalarGridSpec(
            num_scalar_prefetch=1, grid=(n, l // tm), in_specs=in_specs, out_specs=out_specs,
            scratch_shapes=list(scratch.values())),
        out_shape=out_shape,
        compiler_params=pltpu.CompilerParams(dimension_semantics=("arbitrary", "arbitrary"),
                                             vmem_limit_bytes=MIXER_VMEM_BYTES, disable_bounds_checks=True),
        name="prompt_mixer",
    )(slots_prev, *ins)
    return dict(zip(out_names, outs))


SAMPLE_CONSTS = ("n1g", "n2g", "w_in", "a_ln_g", "a_ln_b", "a_w0", "a_b0", "conv_b_w", "pool_w", "pool_scale",
                 "conv_d_w", "conv_d_b", "d_ln_g", "d_ln_b", "w_branch", "w_out", "w_r", "b_r")


def _sample_mixer_kernel(slots_prev, *refs, names, first, cap):
    r = dict(zip(names, refs))
    rows = r["x1"].shape[0]
    hb_ref, proj_ref, ys_ref, m_ref = r["hb"], r["proj"], r["ys"], r["m"]
    h2buf, slot_v, slot_s, carry, sem = r["h2buf"], r["slot_v"], r["slot_s"], r["carry"], r["sem"]
    xs_hbm = r["xs"]
    win_ref, wbr_ref, wout_ref = r["w_in"].at[0], r["w_branch"].at[0], r["w_out"].at[0]
    cbw_ref, cdw_ref, stb_ref, stp_ref, std_ref = r["conv_b_w"], r["conv_d_w"], r["st_b"], r["st_p"], r["st_d"]
    sem_g, sem_d, sem_s = sem.at[0], sem.at[1], sem.at[2]

    mod = r["mod"][...]
    sh1, sc1, g1, sh2, sc2 = (mod[:, i * D_MODEL:(i + 1) * D_MODEL] for i in range(5))
    if first:
        x_src = r["x"]
    else:
        def gather_row(i, c):
            for k in range(2):
                p = slots_prev[k * rows + i]
                pltpu.make_async_copy(r["ys_prev"].at[pl.ds(p, 1)], r["gbuf"].at[k, pl.ds(i, 1)], sem_g).start()
            return c
        lax.fori_loop(0, rows, gather_row, 0, unroll=DMA_UNROLL)
        for _ in range(2):
            pltpu.make_async_copy(r["ys_prev"].at[pl.ds(0, rows)], r["gbuf"].at[0], sem_g).wait()
        x_src = r["xin"]
        x_src[...] = _combine_rows(r["x"][...], r["route_prev"][...], r["mod_prev"][...][:, 5 * D_MODEL:], r["gbuf"])
    h = _rms(x_src[...], r["n1g"][...]) * (1.0 + sc1) + sh1
    _project(h, win_ref, hb_ref, proj_ref)

    u = jax.nn.gelu(proj_ref[:, 0:MIX_W])
    v = _ln(jax.nn.gelu(proj_ref[:, MIX_W:2 * MIX_W]), r["a_ln_g"][...], r["a_ln_b"][...])
    r["v"][...] = v
    ys_ref[0] = (u * (r["a_w0"][...] * v + r["a_b0"][...])).astype(BF16)

    cx = proj_ref[:, 3 * MIX_W:4 * MIX_W] * proj_ref[:, 4 * MIX_W:5 * MIX_W]
    r["cx"][...] = cx
    acc = cbw_ref[CONV_B - 1:CONV_B, :] * _conv_in(cx)
    for k in range(CONV_B - 1):
        acc = acc + cbw_ref[k:k + 1, :] * _conv_in(stb_ref[:, k, :])
    ys_ref[1] = (proj_ref[:, 2 * MIX_W:3 * MIX_W] * acc).astype(BF16)

    pin = proj_ref[:, 5 * MIX_W:6 * MIX_W]
    r["pin"][...] = pin
    pg = MIX_W // len(POOL_WINDOWS)
    mixed = []
    for gi, w in enumerate(POOL_WINDOWS):
        lo = gi * pg
        s = pin[:, lo:lo + pg]
        for i in range(1, w):
            row = POOL_CTX - i
            s = s + stp_ref[:, row, lo:lo + pg]
        pooled = s / float(min(PAST_LEN + 1, w)) - pin[:, lo:lo + pg]
        mixed.append(_bdot(pooled, r["pool_w"][gi]))
    ys_ref[2] = (jnp.concatenate(mixed, axis=1) * r["pool_scale"][...]).astype(BF16)

    gl = proj_ref[:, 6 * MIX_W:7 * MIX_W] * jax.nn.sigmoid(proj_ref[:, 7 * MIX_W:8 * MIX_W])
    r["gl"][...] = gl
    acc = cdw_ref[CONV_D - 1:CONV_D, :] * _conv_in(gl)
    for k in range(CONV_D - 1):
        acc = acc + cdw_ref[k:k + 1, :] * _conv_in(std_ref[:, k, :])
    dc = acc + r["conv_d_b"][...]
    ys_ref[3] = jax.nn.silu(_ln(dc, r["d_ln_g"][...], r["d_ln_b"][...])).astype(BF16)

    for i in range(N_BRANCH):
        _merge_branch(i, hb_ref, win_ref, ys_ref, wbr_ref, m_ref)
    _merge_tail(x_src, g1, sh2, sc2, r["n2g"][...], lambda rs: m_ref[rs, :], wout_ref, r["w_r"], r["b_r"],
                r["x1"], h2buf, r["lg"])

    carry[...] = r["carry_in"][...]
    route = _route_rows(r["lg"][...], carry, cap)
    r["route"][...] = route
    r["carry_out"][...] = carry[...]
    slot_v[...] = jnp.zeros_like(slot_v)
    _slots_to_smem(route, slot_v, slot_s, sem_s)
    r["slots"][...] = slot_v[...]
    pltpu.make_async_copy(slot_v.at[pl.ds(0, 2)], slot_s, sem_s).wait()

    def dispatch_row(i, c):
        for k in range(2):
            pltpu.make_async_copy(h2buf.at[pl.ds(i, 1)], xs_hbm.at[pl.ds(slot_s[k, i], 1)], sem_d).start()
        return c
    lax.fori_loop(0, rows, dispatch_row, 0, unroll=DMA_UNROLL)
    for _ in range(2):
        pltpu.make_async_copy(h2buf, xs_hbm.at[pl.ds(0, rows)], sem_d).wait()


def _sample_mixer(x, mod, st_b, st_p, st_d, lw, li, cap, carry_in, xs, prev=None):
    rows = x.shape[0]
    first = prev is None
    full = lambda a: pl.BlockSpec(a.shape, lambda i, sp: (0,) * a.ndim)
    anyspec = pl.BlockSpec(memory_space=pl.ANY)
    names, ins = ["x", "mod", "carry_in", "xs_in", "st_b", "st_p", "st_d"], [x, mod, carry_in, xs, st_b, st_p, st_d]
    in_specs = [full(x), full(mod), full(carry_in), anyspec, full(st_b), full(st_p), full(st_d)]
    if first:
        slots_prev = jnp.zeros((SUBLANES,), I32)
    else:
        route_prev, mod_prev, ys_prev, slots_prev = prev
        names += ["route_prev", "mod_prev", "ys_prev"]
        ins += [route_prev, mod_prev, ys_prev]
        in_specs += [full(route_prev), full(mod_prev), anyspec]
    names += list(SAMPLE_CONSTS)
    ins += [lw[k] for k in SAMPLE_CONSTS]
    in_specs += [_layer_spec(lw[k].shape, li) if k in PER_LAYER else full(lw[k]) for k in SAMPLE_CONSTS]

    mk = lambda last: jax.ShapeDtypeStruct((rows, last), F32)
    out_names = ["x1", "route", "carry_out", "xs", "slots", "cx", "pin", "gl", "v"]
    out_shape = [mk(D_MODEL), mk(LANES), jax.ShapeDtypeStruct((1, LANES), F32),
                 jax.ShapeDtypeStruct(xs.shape, xs.dtype), jax.ShapeDtypeStruct((SUBLANES, rows), I32),
                 mk(MIX_W), mk(MIX_W), mk(MIX_W), mk(MIX_W)]
    out_specs = [pl.BlockSpec(s.shape, lambda i, sp, nd=len(s.shape): (0,) * nd) for s in out_shape]
    out_specs[3] = anyspec
    scratch = {"hb": pltpu.VMEM((rows, D_MODEL), BF16), "proj": pltpu.VMEM((rows, D_MIX_IN), F32),
               "ys": pltpu.VMEM((N_BRANCH, rows, MIX_W), BF16), "m": pltpu.VMEM((rows, D_MODEL), F32),
               "h2buf": pltpu.VMEM((rows, D_MODEL // 2), U32), "lg": pltpu.VMEM((rows, LANES), F32),
               "slot_v": pltpu.VMEM((SUBLANES, rows), I32), "slot_s": pltpu.SMEM((2, rows), I32),
               "carry": pltpu.VMEM((1, LANES), F32), "sem": pltpu.SemaphoreType.DMA((3,))}
    if not first:
        scratch["xin"] = pltpu.VMEM((rows, D_MODEL), F32)
        scratch["gbuf"] = pltpu.VMEM((2, rows, D_MODEL), F32)
    names = tuple(names + out_names + list(scratch))
    outs = pl.pallas_call(
        functools.partial(_sample_mixer_kernel, names=names, first=first, cap=cap),
        grid_spec=pltpu.PrefetchScalarGridSpec(
            num_scalar_prefetch=1, grid=(1,), in_specs=in_specs, out_specs=out_specs,
            scratch_shapes=list(scratch.values())),
        out_shape=out_shape,
        input_output_aliases={1 + names.index("xs_in"): out_names.index("xs")},
        compiler_params=pltpu.CompilerParams(dimension_semantics=("arbitrary",), vmem_limit_bytes=MIXER_VMEM_BYTES,
                                             disable_bounds_checks=True),
        name="sample_mixer",
    )(slots_prev, *ins)
    return dict(zip(out_names, outs))


def _expert_kernel(vb_ref, be_ref, nv_ref, nu_ref, xs_ref, wg_ref, wu_ref, wd_ref, ys_ref, wgb, wub, wdb):
    b = pl.program_id(0)

    @pl.when(b < nu_ref[0])
    def _():
        @pl.when(jnp.logical_or(b == 0, be_ref[b] != be_ref[jnp.maximum(b - 1, 0)]))
        def _():
            wgb[...] = wg_ref[0, 0].astype(BF16)
            wub[...] = wu_ref[0, 0].astype(BF16)
            wdb[...] = wd_ref[0, 0].astype(BF16)

        valid = lax.broadcasted_iota(I32, (xs_ref.shape[0], 1), 0) < nv_ref[b]
        xb = _unpack_rows(jnp.where(valid, xs_ref[...], jnp.uint32(0)))
        g = jnp.dot(xb, wgb[...], preferred_element_type=F32)
        u = jnp.dot(xb, wub[...], preferred_element_type=F32)
        ys_ref[...] = jnp.dot((jax.nn.silu(g) * u).astype(BF16), wdb[...], preferred_element_type=F32)


def _experts(xs, counts, cap, li, w_eg, w_eu, w_ed):
    bm = MOE_BM
    n_blocks = -(-xs.shape[0] // bm)
    max_used = min(N_EXPERTS * (cap // bm), -(-(cap * 2) // bm) + N_EXPERTS)
    nblk = (counts + bm - 1) // bm
    ends = jnp.cumsum(nblk)
    n_used = ends[-1:]
    b = jnp.arange(max_used, dtype=I32)
    bc = jnp.minimum(b, n_used[0] - 1)
    be = jnp.minimum(jnp.sum(ends[None, :] <= bc[:, None], axis=1), N_EXPERTS - 1).astype(I32)
    mine = be[:, None] == jnp.arange(N_EXPERTS, dtype=I32)[None, :]
    pick = lambda per_expert: jnp.sum(jnp.where(mine, per_expert[None, :], 0), axis=1)
    local = bc - pick(ends - nblk)
    vblock = (be * (cap // bm) + local).astype(I32)
    nvalid = jnp.clip(pick(counts) - local * bm, 0, bm).astype(I32)
    wspec = lambda shape: pl.BlockSpec((1, 1) + shape, lambda i, vb, be_, nv, nu: (li, be_[i], 0, 0))
    return pl.pallas_call(
        _expert_kernel,
        grid_spec=pltpu.PrefetchScalarGridSpec(
            num_scalar_prefetch=4,
            grid=(max_used,),
            in_specs=[pl.BlockSpec((bm, D_MODEL // 2), lambda i, vb, be_, nv, nu: (vb[i], 0)),
                      wspec((D_MODEL, D_EXPERT)), wspec((D_MODEL, D_EXPERT)), wspec((D_EXPERT, D_MODEL))],
            out_specs=pl.BlockSpec((bm, D_MODEL), lambda i, vb, be_, nv, nu: (vb[i], 0)),
            scratch_shapes=[pltpu.VMEM((D_MODEL, D_EXPERT), BF16), pltpu.VMEM((D_MODEL, D_EXPERT), BF16),
                            pltpu.VMEM((D_EXPERT, D_MODEL), BF16)]),
        out_shape=jax.ShapeDtypeStruct((n_blocks * bm, D_MODEL), F32),
        compiler_params=pltpu.CompilerParams(dimension_semantics=("arbitrary",)),
        name="experts",
    )(vblock, be, nvalid, n_used.astype(I32), xs, w_eg, w_eu, w_ed)


def _combine_kernel(pos_ref, ys_hbm, x1_ref, route_ref, g2_ref, fg_ref, out_ref, buf, sem):
    tc = x1_ref.shape[0]
    hc = tc // 2
    i = pl.program_id(0)
    n = pl.num_programs(0)
    g2 = g2_ref[...].reshape(-1, D_MODEL)

    def issue(tile, half):
        for row in range(hc):
            for k in range(2):
                p = pos_ref[(tile * 2 + k) * tc + half * hc + row]
                pltpu.make_async_copy(ys_hbm.at[pl.ds(p, 1)], buf.at[half, k, pl.ds(row, 1)], sem.at[half]).start()

    def finish(half):
        for k in range(2):
            pltpu.make_async_copy(ys_hbm.at[pl.ds(0, hc)], buf.at[half, k], sem.at[half]).wait()
        rs = slice(half * hc, (half + 1) * hc)
        x2 = _combine_rows(x1_ref[rs, :], route_ref[rs, :], g2 if g2.shape[0] == 1 else g2[rs], buf.at[half])
        out_ref[rs, :] = _rms(x2, fg_ref[...])

    @pl.when(i == 0)
    def _():
        issue(0, 0)

    issue(i, 1)
    finish(0)

    @pl.when(i + 1 < n)
    def _():
        issue(i + 1, 0)

    finish(1)


def _combine(ys, pos, tc, x1, route, g2, fg):
    t = x1.shape[0]
    if g2.ndim == 3:
        seq = t // g2.shape[0]
        g2spec = pl.BlockSpec((1, 1, D_MODEL), lambda i, pos: (i * tc // seq, 0, 0))
    else:
        g2spec = pl.BlockSpec((tc, D_MODEL), lambda i, pos: (i, 0))
    tokspec = lambda last: pl.BlockSpec((tc, last), lambda i, pos: (i, 0))
    return pl.pallas_call(
        _combine_kernel,
        grid_spec=pltpu.PrefetchScalarGridSpec(
            num_scalar_prefetch=1,
            grid=(t // tc,),
            in_specs=[pl.BlockSpec(memory_space=pl.ANY), tokspec(D_MODEL), tokspec(LANES), g2spec,
                      pl.BlockSpec((1, D_MODEL), lambda i, pos: (0, 0))],
            out_specs=tokspec(D_MODEL),
            scratch_shapes=[pltpu.VMEM((2, 2, tc // 2, D_MODEL), F32), pltpu.SemaphoreType.DMA((2,))]),
        out_shape=jax.ShapeDtypeStruct((t, D_MODEL), F32),
        compiler_params=pltpu.CompilerParams(dimension_semantics=("arbitrary",), disable_bounds_checks=True,
                                             vmem_limit_bytes=MIXER_VMEM_BYTES),
        name="combine",
    )(pos, ys, x1, route, g2, fg)


def _layer_weights(li, norm1_g, norm2_g, w_in, a_ln_g, a_ln_b, a_ws, a_bs, conv_b_w, pool_w, pool_scale,
                   conv_d_w, conv_d_b, d_ln_g, d_ln_b, w_branch, w_out, w_rg, b_rg, w_re, b_re):
    row = lambda a: a[li].reshape(1, -1)
    cg = MIX_W // A_GROUPS
    pad = LANES - N_GROUPS_MOE - N_EXPERTS
    return {
        "n1g": row(norm1_g), "n2g": row(norm2_g), "w_in": w_in.astype(BF16),
        "a_ln_g": row(a_ln_g), "a_ln_b": row(a_ln_b), "a_ws": a_ws[li],
        "a_bs_full": jnp.repeat(a_bs[li].T, cg, axis=1),
        "a_w0": jnp.repeat(a_ws[li, :, 0, 0], cg).reshape(1, MIX_W),
        "a_b0": jnp.repeat(a_bs[li, :, 0], cg).reshape(1, MIX_W),
        "conv_b_w": conv_b_w[li], "pool_w": pool_w[li], "pool_scale": row(pool_scale),
        "conv_b_w8": jnp.broadcast_to(conv_b_w[li][:, None, :], (CONV_B, SUBLANES, MIX_W)),
        "conv_d_w8": jnp.broadcast_to(conv_d_w[li][:, None, :], (CONV_D, SUBLANES, MIX_W)),
        "conv_d_w": conv_d_w[li], "conv_d_b": row(conv_d_b), "d_ln_g": row(d_ln_g), "d_ln_b": row(d_ln_b),
        "w_branch": w_branch.astype(BF16), "w_out": w_out.astype(BF16),
        "w_r": jnp.concatenate([w_rg[li], w_re[li], jnp.zeros((D_MODEL, pad), F32)], axis=1),
        "b_r": jnp.concatenate([b_rg[li], b_re[li], jnp.zeros((pad,), F32)]).reshape(1, LANES),
    }


def _slots(slot_rows):
    return slot_rows.reshape(-1, SUBLANES, slot_rows.shape[-1])[:, :2].reshape(-1)


def kernel(x_prompt, x_sample, state_conv_b, state_pool, state_conv_d, c_prompt, c_sample, w_mod, b_mod, norm1_g, norm2_g, w_in, a_ln_g, a_ln_b, a_ws, a_bs, conv_b_w, pool_w, pool_scale, conv_d_w, conv_d_b, d_ln_g, d_ln_b, w_branch, w_out, w_rg, b_rg, w_re, b_re, w_eg, w_eu, w_ed, final_g):
    depth = w_mod.shape[0]
    nb, seq, _ = x_prompt.shape
    ns = x_sample.shape[0]
    t = nb * seq
    cap = -(-(t + ns) // MOE_BM) * MOE_BM
    mod_all = _modulation(jnp.concatenate([c_prompt, c_sample], axis=0), w_mod, b_mod)
    fg = final_g.reshape(1, D_MODEL)

    xp = x_prompt
    xs_tok = x_sample.reshape(ns, D_MODEL)
    prev_p = prev_s = None
    outs = {k: [] for k in ("bp", "pp", "dp", "bs", "ps", "ds", "vs")}
    for li in range(depth):
        lw = _layer_weights(li, norm1_g, norm2_g, w_in, a_ln_g, a_ln_b, a_ws, a_bs, conv_b_w, pool_w, pool_scale,
                            conv_d_w, conv_d_b, d_ln_g, d_ln_b, w_branch, w_out, w_rg, b_rg, w_re, b_re)
        mod_p = mod_all[li, :nb].reshape(nb, 1, 6 * D_MODEL)
        mod_s = mod_all[li, nb:]

        pm = _prompt_mixer(xp, mod_p, lw, li, cap, prev_p)
        outs["bp"].append(pm["new_b"])
        outs["pp"].append(pm["new_p"])
        outs["dp"].append(pm["new_d"])

        st_b, st_p, st_d = state_conv_b[li], state_pool[li], state_conv_d[li]
        sm = _sample_mixer(xs_tok, mod_s, st_b, st_p, st_d, lw, li, cap,
                           pm["carry_out"], pm["xs"], prev_s)
        outs["bs"].append(jnp.concatenate([st_b[:, 1:], sm["cx"][:, None]], axis=1))
        outs["ps"].append(jnp.concatenate([st_p[:, 1:], sm["pin"][:, None]], axis=1))
        outs["ds"].append(jnp.concatenate([st_d[:, 1:], sm["gl"][:, None]], axis=1))
        outs["vs"].append(sm["v"][:, None])

        counts = sm["carry_out"][0, ROUTE_LANE0:ROUTE_LANE0 + N_EXPERTS].astype(I32)
        ys = _experts(sm["xs"], counts, cap, li, w_eg, w_eu, w_ed)
        route_p = pm["route"]
        prev_p = (route_p, mod_p, ys, _slots(pm["slots"]))
        prev_s = (sm["route"], mod_s, ys, _slots(sm["slots"]))
        xp, xs_tok = pm["x1"], sm["x1"]

    route_p, mod_p, ys, slots_p = prev_p
    route_s, mod_s, _, slots_s = prev_s
    y_p = _combine(ys, slots_p, MIX_TM, xp.reshape(t, D_MODEL), route_p.reshape(t, LANES), mod_p[:, :, 5 * D_MODEL:], fg)
    y_s = _combine(ys, slots_s, ns, xs_tok, route_s, mod_s[:, 5 * D_MODEL:], fg)

    stack = lambda k: jnp.stack(outs[k])
    return (y_p.reshape(nb, seq, D_MODEL), y_s.reshape(ns, 1, D_MODEL), stack("bp"), stack("pp"), stack("dp"),
            stack("bs"), stack("ps"), stack("ds"), stack("vs"))
```

```python
import functools

import jax
import jax.numpy as jnp
from jax import lax
from jax.experimental import pallas as pl
from jax.experimental.pallas import tpu as pltpu

F32 = jnp.float32
BF16 = jnp.bfloat16
I32 = jnp.int32
U32 = jnp.uint32

D_MODEL = 1024
MIX_W = D_MODEL // 2
CHUNK = 128
A_GROUPS = 4
CONV_B = 3
POOL_WINDOWS = (2, 4, 8, 16)
POOL_CTX = 15
CONV_D = 31
N_BRANCH = 4
N_GROUPS_MOE = 4
EXPERTS_PER_GROUP = 8
N_EXPERTS = N_GROUPS_MOE * EXPERTS_PER_GROUP
D_EXPERT = D_MODEL // 4
EPS = 1e-6
D_MIX_IN = 8 * MIX_W
D_IN = D_MIX_IN + N_BRANCH * D_MODEL
PAST_LEN = 16384

LANES = 128
SUBLANES = 8
ROUTE_LANE0 = N_GROUPS_MOE
ROUTE_E, ROUTE_SLOT, ROUTE_W = 0, 2, 4

MIX_TM = 512
CONV_ROWS = 64
TAIL_ROWS = 256
MXU_COLS = 256
CTX_B = 8
CTX_P = 32
CTX_D = 32
POOL_TMP_SLOTS = sum(w.bit_length() - 2 for w in POOL_WINDOWS)
MOE_BM = 512
DMA_UNROLL = 8
MIXER_VMEM_BYTES = 56 * 1024 * 1024


def _rms(x, g):
    return x * lax.rsqrt(jnp.mean(x * x, axis=-1, keepdims=True) + EPS) * g


def _ln(x, g, b):
    mu = jnp.mean(x, axis=-1, keepdims=True)
    xc = x - mu
    return xc * lax.rsqrt(jnp.mean(xc * xc, axis=-1, keepdims=True) + EPS) * g + b


def _conv_in(x):
    return x.astype(BF16).astype(F32)


def _pack_rows(x):
    c = x.shape[1] // 2
    bits = lambda a: lax.bitcast_convert_type(a.astype(BF16).astype(F32), U32)
    return (bits(x[:, :c]) >> 16) | (bits(x[:, c:]) & jnp.uint32(0xFFFF0000))


def _unpack_rows(p):
    lo = lax.bitcast_convert_type(p << 16, F32)
    hi = lax.bitcast_convert_type(p & jnp.uint32(0xFFFF0000), F32)
    return jnp.concatenate([lo, hi], axis=1).astype(BF16)


def _bdot(a, b):
    return jnp.dot(a.astype(BF16), b.astype(BF16), preferred_element_type=F32)


def _mod_kernel(c_ref, w_ref, b_ref, o_ref):
    o_ref[0] = _bdot(c_ref[...], w_ref[0]) + b_ref[0]


def _modulation(c_all, w_mod, b_mod):
    depth = w_mod.shape[0]
    rows = c_all.shape[0]
    tn = 1536
    return pl.pallas_call(
        _mod_kernel,
        grid=(depth, 6 * D_MODEL // tn),
        in_specs=[pl.BlockSpec((rows, D_MODEL), lambda l, n: (0, 0)),
                  pl.BlockSpec((1, D_MODEL, tn), lambda l, n: (l, 0, n)),
                  pl.BlockSpec((1, 1, tn), lambda l, n: (l, 0, n))],
        out_specs=pl.BlockSpec((1, rows, tn), lambda l, n: (l, 0, n)),
        out_shape=jax.ShapeDtypeStruct((depth, rows, 6 * D_MODEL), F32),
        name="modulation",
    )(c_all, w_mod, b_mod.reshape(depth, 1, 6 * D_MODEL))


def _project(h, win_ref, hb_ref, proj_ref, between=None):
    hb_ref[...] = h.astype(BF16)
    for i, c in enumerate(range(0, D_MIX_IN, D_MODEL)):
        proj_ref[:, c:c + D_MODEL] = jnp.dot(hb_ref[...], win_ref[:, c:c + D_MODEL], preferred_element_type=F32)
        if between is not None:
            between(i)


def _merge_branch(i, hb_ref, win_ref, ys_ref, wbr_ref, m_ref):
    lo = D_MIX_IN + i * D_MODEL
    rows = hb_ref.shape[0]
    for r0 in range(0, rows, TAIL_ROWS):
        rs = slice(r0, min(r0 + TAIL_ROWS, rows))
        gates = jnp.dot(hb_ref[rs, :], win_ref[:, lo:lo + D_MODEL], preferred_element_type=F32)
        br = jnp.dot(ys_ref[i, rs, :], wbr_ref[i], preferred_element_type=F32)
        t = jax.nn.sigmoid(gates) * br
        m_ref[rs, :] = t if i == 0 else m_ref[rs, :] + t


def _merge_tail(x_ref, g1, sh2, sc2, n2g, m_rows, wout_ref, wr_ref, br_ref, x1_ref, h2_ref, lg_ref):
    rows = x1_ref.shape[0]
    for r0 in range(0, rows, TAIL_ROWS):
        rs = slice(r0, min(r0 + TAIL_ROWS, rows))
        per_row = lambda a: a if a.shape[0] == 1 else a[rs]
        out = jnp.dot(m_rows(rs).astype(BF16), wout_ref[...], preferred_element_type=F32)
        x1 = x_ref[rs, :] + per_row(g1) * out
        h2 = _rms(x1, n2g) * (1.0 + per_row(sc2)) + per_row(sh2)
        x1_ref[rs, :] = x1
        h2_ref[rs, :] = _pack_rows(h2)
        lg_ref[rs, :] = _bdot(h2, wr_ref[...]) + br_ref[...]


def _route_rows(lg, carry_ref, cap):
    tr = lg.shape[0]
    lane = lax.broadcasted_iota(I32, (tr, LANES), 1)
    lanef = lane.astype(F32)

    def first_argmax(vals):
        mx = jnp.max(vals, axis=-1, keepdims=True)
        return mx, jnp.min(jnp.where(vals == mx, lanef, float(LANES)), axis=-1, keepdims=True)

    neg = -jnp.inf
    gl = jnp.where(lane < N_GROUPS_MOE, lg, neg)
    gmax, gsel = first_argmax(gl)
    p_g = 1.0 / jnp.sum(jnp.exp(gl - gmax), axis=-1, keepdims=True)
    e_lo = ROUTE_LANE0 + EXPERTS_PER_GROUP * gsel
    emask = jnp.logical_and(lanef >= e_lo, lanef < e_lo + EXPERTS_PER_GROUP)
    el = jnp.where(emask, lg, neg)
    emax = jnp.max(el, axis=-1, keepdims=True)
    ex = jnp.exp(el - emax)
    prob = jnp.where(emask, ex / jnp.sum(ex, axis=-1, keepdims=True), -1.0)
    p1, i1 = first_argmax(prob)
    p2, i2 = first_argmax(jnp.where(lanef == i1, -1.0, prob))
    w1 = p_g * p1 / (p1 + p2)
    w2 = p_g * p2 / (p1 + p2)

    hot1 = lanef == i1
    hot2 = lanef == i2
    hot = jnp.logical_or(hot1, hot2).astype(F32)
    rr = lax.broadcasted_iota(I32, (tr, tr), 0)
    cc = lax.broadcasted_iota(I32, (tr, tr), 1)
    before = jnp.dot((rr > cc).astype(BF16), hot.astype(BF16), preferred_element_type=F32) + carry_ref[...]
    rank1 = jnp.sum(jnp.where(hot1, before, 0.0), axis=-1, keepdims=True)
    rank2 = jnp.sum(jnp.where(hot2, before, 0.0), axis=-1, keepdims=True)
    carry_ref[...] = carry_ref[...] + jnp.sum(hot, axis=0, keepdims=True)

    e1 = i1 - ROUTE_LANE0
    e2 = i2 - ROUTE_LANE0
    fields = (e1, e2, e1 * float(cap) + rank1, e2 * float(cap) + rank2, w1, w2)
    out = jnp.zeros((tr, LANES), F32)
    for k, f in enumerate(fields):
        out = jnp.where(lane == k, f, out)
    return out


def _slots_to_smem(route, slot_v, slot_s, sem):
    slot_v[0:2, :] = jnp.transpose(route)[ROUTE_SLOT:ROUTE_SLOT + 2, :].astype(I32)
    pltpu.make_async_copy(slot_v.at[pl.ds(0, 2)], slot_s, sem).start()


def _combine_rows(x1_prev, route_prev, g2, gbuf):
    return x1_prev + g2 * (route_prev[:, ROUTE_W:ROUTE_W + 1] * gbuf[0]
                           + route_prev[:, ROUTE_W + 1:ROUTE_W + 2] * gbuf[1])


MIXER_COL_GROUPS = tuple(range(lo * MIX_W // MXU_COLS, hi * MIX_W // MXU_COLS) for lo, hi in ((0, 2), (2, 5), (5, 6), (6, 8)))
PROJ_SLABS = max(len(g) for g in MIXER_COL_GROUPS)
PROJ_SLAB_OF = {g: g - groups[0] for groups in MIXER_COL_GROUPS for g in groups}


class _ProjCols:
    def __init__(self, ref):
        self.ref = ref

    def slab(self, col):
        return self.ref.at[PROJ_SLAB_OF[col // MXU_COLS]]

    def __getitem__(self, idx):
        rows, cols = idx
        lo, hi = cols.start, cols.stop
        parts = []
        while lo < hi:
            g = lo // MXU_COLS
            end = min(hi, (g + 1) * MXU_COLS)
            parts.append(self.slab(lo)[rows, lo - g * MXU_COLS:end - g * MXU_COLS])
            lo = end
        return parts[0] if len(parts) == 1 else jnp.concatenate(parts, axis=1)


def _tap(w8_ref, k, x):
    rows, c = x.shape
    return (x.reshape(rows // SUBLANES, SUBLANES, c) * w8_ref[k][None]).reshape(rows, c)


PROMPT_CONSTS = ("n1g", "n2g", "w_in", "a_ln_g", "a_ln_b", "a_ws", "a_bs_full", "conv_b_w8", "pool_w", "pool_scale",
                 "conv_d_w8", "conv_d_b", "d_ln_g", "d_ln_b", "w_branch", "w_out", "w_r", "b_r")


def _prompt_mixer_kernel(slots_prev, *refs, names, first, cap):
    r = dict(zip(names, refs))
    tm = r["x1"].shape[1]
    hb_ref, proj_ref, ys_ref = r["hb"], _ProjCols(r["proj"]), r["ys"]
    m_ref = r["x1"].at[0]
    cx_ext, p_ext, gl_ext, pool_tmp, conv_tmp = r["cx_ext"], r["p_ext"], r["gl_ext"], r["pool_tmp"], r["conv_tmp"]
    h2buf, slot_v, slot_s, carry, sem = r["h2buf"], r["slot_v"], r["slot_s"], r["carry"], r["sem"]
    xs_hbm = r["xs"]
    win_ref, wbr_ref, wout_ref = r["w_in"].at[0], r["w_branch"].at[0], r["w_out"].at[0]
    j = pl.program_id(1)
    nj = pl.num_programs(1)
    step = pl.program_id(0) * nj + j
    nsteps = pl.num_programs(0) * nj
    trash = N_EXPERTS * cap
    sem_g, sem_d, sem_s = sem.at[0], sem.at[1], sem.at[2]

    def gather_row(tile, row):
        for k in range(2):
            p = slots_prev[(tile * 2 + k) * tm + row]
            pltpu.make_async_copy(r["ys_prev"].at[pl.ds(p, 1)], r["gbuf"].at[k, pl.ds(row, 1)], sem_g).start()

    def dispatch_row(row):
        for k in range(2):
            pltpu.make_async_copy(h2buf.at[pl.ds(row, 1)], xs_hbm.at[pl.ds(slot_s[k, row], 1)], sem_d).start()

    def wait_rows(src, dst, sem_, times):
        for _ in range(times):
            pltpu.make_async_copy(src, dst, sem_).wait()

    @pl.when(step == 0)
    def _():
        carry[...] = r["carry_in"][...]
        h2buf[...] = jnp.zeros_like(h2buf)
        slot_v[...] = (trash + tm * lax.broadcasted_iota(I32, slot_v.shape, 0)
                       + lax.broadcasted_iota(I32, slot_v.shape, 1))
        pltpu.make_async_copy(slot_v.at[pl.ds(0, 2)], slot_s, sem_s).start()
        if not first:
            lax.fori_loop(0, tm, lambda i, c: (gather_row(0, i), c)[1], 0, unroll=DMA_UNROLL)

    @pl.when(j == 0)
    def _():
        cx_ext[0:CTX_B, :] = jnp.zeros((CTX_B, MIX_W), F32)
        p_ext[0:CTX_P, :] = jnp.zeros((CTX_P, MIX_W), F32)
        gl_ext[0:CTX_D, :] = jnp.zeros((CTX_D, MIX_W), F32)

    mod = r["mod"][0]
    sh1, sc1, g1, sh2, sc2 = (mod[:, i * D_MODEL:(i + 1) * D_MODEL] for i in range(5))
    if first:
        x_src = r["x"].at[0]
    else:
        wait_rows(r["ys_prev"].at[pl.ds(0, tm)], r["gbuf"].at[0], sem_g, 2)
        x_src = r["xin"]
        x_src[...] = _combine_rows(r["x"][0], r["route_prev"][0], r["mod_prev"][0][:, 5 * D_MODEL:], r["gbuf"])
    h = _rms(x_src[...], r["n1g"][...]) * (1.0 + sc1) + sh1

    pltpu.make_async_copy(slot_v.at[pl.ds(0, 2)], slot_s, sem_s).wait()

    row_blocks = [slice(r0, r0 + TAIL_ROWS) for r0 in range(0, tm, TAIL_ROWS)]

    def proj_piece(mixer, rs):
        groups = MIXER_COL_GROUPS[mixer]
        lo = groups[0] * MXU_COLS

        def f():
            res = jnp.dot(hb_ref[rs, :], win_ref[:, lo:lo + len(groups) * MXU_COLS], preferred_element_type=F32)
            for n, g in enumerate(groups):
                proj_ref.slab(g * MXU_COLS)[rs, :] = res[:, n * MXU_COLS:(n + 1) * MXU_COLS]
        return f

    def proj_pieces(mixer):
        return [proj_piece(mixer, rs) for rs in row_blocks]

    def merge_piece(i, rs):
        def f():
            lo = D_MIX_IN + i * D_MODEL
            gates = jnp.dot(hb_ref[rs, :], win_ref[:, lo:lo + D_MODEL], preferred_element_type=F32)
            br = jnp.dot(ys_ref[i, rs, :], wbr_ref[i], preferred_element_type=F32)
            t = jax.nn.sigmoid(gates) * br
            m_ref[rs, :] = t if i == 0 else m_ref[rs, :] + t
        return f

    def merge_pieces(i):
        return [merge_piece(i, rs) for rs in row_blocks]

    def emit(vector_pieces, matrix_pieces):
        nv, nm = len(vector_pieces), len(matrix_pieces)
        for i, vp in enumerate(vector_pieces):
            vp()
            for mp in matrix_pieces[i * nm // nv:(i + 1) * nm // nv]:
                mp()

    def dispatch_chunk(i, n):
        def f():
            for row in range(i * tm // n, (i + 1) * tm // n):
                dispatch_row(row)
        return f

    hb_ref[...] = h.astype(BF16)
    n_a = len(MIXER_COL_GROUPS[0])
    emit(proj_pieces(0), [dispatch_chunk(i, n_a) for i in range(n_a)])

    rr = lax.broadcasted_iota(I32, (CHUNK, CHUNK), 0)
    cc = lax.broadcasted_iota(I32, (CHUNK, CHUNK), 1)
    wtril = [jnp.where(rr >= cc, r["a_ws"][g], 0.0).astype(BF16) for g in range(A_GROUPS)]
    cg = MIX_W // A_GROUPS

    def a_chunk(c0):
        def f():
            u = jax.nn.gelu(proj_ref[c0:c0 + CHUNK, 0:MIX_W])
            v = _ln(jax.nn.gelu(proj_ref[c0:c0 + CHUNK, MIX_W:2 * MIX_W]), r["a_ln_g"][...], r["a_ln_b"][...])
            vb = v.astype(BF16)
            mixed = jnp.concatenate(
                [jnp.dot(wtril[g], vb[:, g * cg:(g + 1) * cg], preferred_element_type=F32)
                 for g in range(A_GROUPS)], axis=1) + r["a_bs_full"][...]
            ys_ref[0, c0:c0 + CHUNK, :] = (u * mixed).astype(BF16)
        return f

    emit([a_chunk(c0) for c0 in range(0, tm, CHUNK)], [])
    emit(proj_pieces(1), [])

    cbw_ref = r["conv_b_w8"]

    def b_prep(c0):
        def f():
            cx_ext[CTX_B + c0:CTX_B + c0 + CHUNK, :] = _conv_in(
                proj_ref[c0:c0 + CHUNK, 3 * MIX_W:4 * MIX_W] * proj_ref[c0:c0 + CHUNK, 4 * MIX_W:5 * MIX_W])
        return f

    def b_block(r0):
        def f():
            acc = None
            for k in range(CONV_B):
                t = _tap(cbw_ref, k, cx_ext[pl.ds(r0 + CTX_B - (CONV_B - 1) + k, CONV_ROWS), :])
                acc = t if acc is None else acc + t
            ys_ref[1, r0:r0 + CONV_ROWS, :] = (proj_ref[r0:r0 + CONV_ROWS, 2 * MIX_W:3 * MIX_W] * acc).astype(BF16)
        return f

    def b_end():
        tail = pl.ds(tm - (CONV_B - 1), CONV_B - 1)
        r["new_b"][0] = proj_ref[tail, 3 * MIX_W:4 * MIX_W] * proj_ref[tail, 4 * MIX_W:5 * MIX_W]
        cx_ext[0:CTX_B, :] = cx_ext[tm:tm + CTX_B, :]

    emit([b_prep(c0) for c0 in range(0, tm, CHUNK)] + [b_block(r0) for r0 in range(0, tm, CONV_ROWS)] + [b_end],
         merge_pieces(0))
    emit(proj_pieces(2), [])

    pos = j * tm + lax.broadcasted_iota(I32, (tm, 1), 0)
    pg = MIX_W // len(POOL_WINDOWS)

    def c_prep():
        p_ext[CTX_P:CTX_P + tm, :] = proj_ref[:, 5 * MIX_W:6 * MIX_W]

    def c_group(gi, w):
        def f():
            lo = gi * pg
            levels = w.bit_length() - 1
            for lv in range(levels):
                span = 1 << lv
                start = CTX_P if lv == levels - 1 else SUBLANES * (lv + 1)
                rows = CTX_P + tm - start
                if lv == 0:
                    a = p_ext[start:start + rows, lo:lo + pg] + p_ext[pl.ds(start - span, rows), lo:lo + pg]
                else:
                    prev = pool_tmp.at[gi * (gi - 1) // 2 + lv - 1]
                    a = prev[start:start + rows, :] + prev[pl.ds(start - span, rows), :]
                if lv == levels - 1:
                    s = a
                else:
                    pool_tmp[gi * (gi - 1) // 2 + lv, start:start + rows, :] = a
            cnt = jnp.minimum(pos + 1, w).astype(F32)
            pooled = s / cnt - p_ext[CTX_P:CTX_P + tm, lo:lo + pg]
            ys_ref[2, :, lo:lo + pg] = (_bdot(pooled, r["pool_w"][gi]) * r["pool_scale"][:, lo:lo + pg]).astype(BF16)
        return f

    def c_end():
        r["new_p"][0] = p_ext[pl.ds(CTX_P + tm - POOL_CTX, POOL_CTX), :]
        p_ext[0:CTX_P, :] = p_ext[tm:tm + CTX_P, :]

    emit([c_prep] + proj_pieces(3), [])
    emit([c_group(gi, w) for gi, w in enumerate(POOL_WINDOWS)] + [c_end], merge_pieces(1))

    cdw_ref = r["conv_d_w8"]
    off0 = CTX_D - (CONV_D - 1)
    next_tile = jnp.minimum(step + 1, nsteps - 1)

    def d_prep(c0):
        def f():
            gl_ext[CTX_D + c0:CTX_D + c0 + CHUNK, :] = _conv_in(
                proj_ref[c0:c0 + CHUNK, 6 * MIX_W:7 * MIX_W]
                * jax.nn.sigmoid(proj_ref[c0:c0 + CHUNK, 7 * MIX_W:8 * MIX_W]))
        return f

    def d_block(r0):
        def f():
            acc = None
            for rr_ in range(SUBLANES):
                rows = CONV_ROWS if rr_ == 0 else CONV_ROWS + SUBLANES
                y = None
                for q in range(-(-(off0 + CONV_D) // SUBLANES)):
                    k = SUBLANES * q + rr_ - off0
                    if 0 <= k < CONV_D:
                        t = _tap(cdw_ref, k, gl_ext[r0 + SUBLANES * q:r0 + SUBLANES * q + rows, :])
                        y = t if y is None else y + t
                if rr_ > 0:
                    conv_tmp[rr_ - 1] = y
                    y = conv_tmp[rr_ - 1, pl.ds(rr_, CONV_ROWS), :]
                acc = y if acc is None else acc + y
            dc = acc + r["conv_d_b"][...]
            ys_ref[3, r0:r0 + CONV_ROWS, :] = jax.nn.silu(_ln(dc, r["d_ln_g"][...], r["d_ln_b"][...])).astype(BF16)
            if not first:
                for row in range(r0, r0 + CONV_ROWS):
                    gather_row(next_tile, row)
        return f

    def d_end():
        tail = pl.ds(tm - (CONV_D - 1), CONV_D - 1)
        r["new_d"][0] = proj_ref[tail, 6 * MIX_W:7 * MIX_W] * jax.nn.sigmoid(proj_ref[tail, 7 * MIX_W:8 * MIX_W])
        gl_ext[0:CTX_D, :] = gl_ext[tm:tm + CTX_D, :]

    emit([d_prep(c0) for c0 in range(0, tm, CHUNK)] + [d_block(r0) for r0 in range(0, tm, CONV_ROWS)] + [d_end],
         merge_pieces(2))
    for mp in merge_pieces(3):
        mp()

    wait_rows(h2buf, xs_hbm.at[pl.ds(0, tm)], sem_d, 2)
    _merge_tail(x_src, g1, sh2, sc2, r["n2g"][...], lambda rs: m_ref[rs, :], wout_ref, r["w_r"], r["b_r"],
                r["x1"].at[0], h2buf, r["lg"])
    route = _route_rows(r["lg"][...], carry, cap)
    r["route"][0] = route
    r["carry_out"][...] = carry[...]
    _slots_to_smem(route, slot_v, slot_s, sem_s)
    r["slots"][...] = slot_v[...]

    @pl.when(step == nsteps - 1)
    def _():
        pltpu.make_async_copy(slot_v.at[pl.ds(0, 2)], slot_s, sem_s).wait()
        lax.fori_loop(0, tm, lambda i, c: (dispatch_row(i), c)[1], 0, unroll=DMA_UNROLL)
        wait_rows(h2buf, xs_hbm.at[pl.ds(0, tm)], sem_d, 2)
        if not first:
            wait_rows(r["ys_prev"].at[pl.ds(0, tm)], r["gbuf"].at[0], sem_g, 2)


def _const_spec(shape):
    nd = len(shape)
    return pl.BlockSpec(shape, lambda *_: (0,) * nd, pipeline_mode=pl.Buffered(1))


PER_LAYER = ("w_in", "w_branch", "w_out")


def _layer_spec(shape, li):
    nd = len(shape)
    return pl.BlockSpec((1,) + tuple(shape[1:]), lambda *_: (li,) + (0,) * (nd - 1), pipeline_mode=pl.Buffered(1))


def _prompt_mixer(x, mod, lw, li, cap, prev=None):
    n, l, _ = x.shape
    tm = MIX_TM
    first = prev is None
    tok = lambda last: pl.BlockSpec((1, tm, last), lambda b, j, sp: (b, j, 0))
    per_seq = lambda rows, last: pl.BlockSpec((1, rows, last), lambda b, j, sp: (b, 0, 0))
    anyspec = pl.BlockSpec(memory_space=pl.ANY)
    names, ins, in_specs = ["x", "mod", "carry_in"], [x, mod, jnp.zeros((1, LANES), F32)], \
        [tok(D_MODEL), per_seq(1, 6 * D_MODEL), _const_spec((1, LANES))]
    if first:
        slots_prev = jnp.zeros((SUBLANES,), I32)
    else:
        route_prev, mod_prev, ys_prev, slots_prev = prev
        names += ["route_prev", "mod_prev", "ys_prev"]
        ins += [route_prev, mod_prev, ys_prev]
        in_specs += [tok(LANES), per_seq(1, 6 * D_MODEL), anyspec]
    names += list(PROMPT_CONSTS)
    ins += [lw[k] for k in PROMPT_CONSTS]
    in_specs += [_layer_spec(lw[k].shape, li) if k in PER_LAYER else _const_spec(lw[k].shape) for k in PROMPT_CONSTS]

    xs_rows = N_EXPERTS * cap + 2 * tm
    out_names = ["x1", "route", "carry_out", "xs", "slots", "new_b", "new_p", "new_d"]
    out_specs = [tok(D_MODEL), tok(LANES), pl.BlockSpec((1, LANES), lambda b, j, sp: (0, 0)), anyspec,
                 pl.BlockSpec((SUBLANES, tm), lambda b, j, sp: (b * (l // tm) + j, 0)),
                 per_seq(CONV_B - 1, MIX_W), per_seq(POOL_CTX, MIX_W), per_seq(CONV_D - 1, MIX_W)]
    out_shape = [jax.ShapeDtypeStruct((n, l, D_MODEL), F32), jax.ShapeDtypeStruct((n, l, LANES), F32),
                 jax.ShapeDtypeStruct((1, LANES), F32), jax.ShapeDtypeStruct((xs_rows, D_MODEL // 2), U32),
                 jax.ShapeDtypeStruct((n * (l // tm) * SUBLANES, tm), I32),
                 jax.ShapeDtypeStruct((n, CONV_B - 1, MIX_W), F32), jax.ShapeDtypeStruct((n, POOL_CTX, MIX_W), F32),
                 jax.ShapeDtypeStruct((n, CONV_D - 1, MIX_W), F32)]
    scratch = {"hb": pltpu.VMEM((tm, D_MODEL), BF16), "proj": pltpu.VMEM((PROJ_SLABS, tm, MXU_COLS), F32),
               "ys": pltpu.VMEM((N_BRANCH, tm, MIX_W), BF16),
               "cx_ext": pltpu.VMEM((CTX_B + tm, MIX_W), F32), "p_ext": pltpu.VMEM((CTX_P + tm, MIX_W), F32),
               "gl_ext": pltpu.VMEM((CTX_D + tm, MIX_W), F32),
               "pool_tmp": pltpu.VMEM((POOL_TMP_SLOTS, CTX_P + tm, MIX_W // len(POOL_WINDOWS)), F32),
               "conv_tmp": pltpu.VMEM((SUBLANES - 1, CONV_ROWS + SUBLANES, MIX_W), F32),
               "h2buf": pltpu.VMEM((tm, D_MODEL // 2), U32), "lg": pltpu.VMEM((tm, LANES), F32),
               "slot_v": pltpu.VMEM((SUBLANES, tm), I32), "slot_s": pltpu.SMEM((2, tm), I32),
               "carry": pltpu.VMEM((1, LANES), F32), "sem": pltpu.SemaphoreType.DMA((3,))}
    if not first:
        scratch["xin"] = pltpu.VMEM((tm, D_MODEL), F32)
        scratch["gbuf"] = pltpu.VMEM((2, tm, D_MODEL), F32)
    names = tuple(names + out_names + list(scratch))
    outs = pl.pallas_call(
        functools.partial(_prompt_mixer_kernel, names=names, first=first, cap=cap),
        grid_spec=pltpu.PrefetchScalarGridSpec(
            num_scalar_prefetch=1, grid=(n, l // tm), in_specs=in_specs, out_specs=out_specs,
            scratch_shapes=list(scratch.values())),
        out_shape=out_shape,
        compiler_params=pltpu.CompilerParams(dimension_semantics=("arbitrary", "arbitrary"),
                                             vmem_limit_bytes=MIXER_VMEM_BYTES, disable_bounds_checks=True),
        name="prompt_mixer",
    )(slots_prev, *ins)
    return dict(zip(out_names, outs))


SAMPLE_CONSTS = ("n1g", "n2g", "w_in", "a_ln_g", "a_ln_b", "a_w0", "a_b0", "conv_b_w", "pool_w", "pool_scale",
                 "conv_d_w", "conv_d_b", "d_ln_g", "d_ln_b", "w_branch", "w_out", "w_r", "b_r")


def _sample_mixer_kernel(slots_prev, *refs, names, first, cap):
    r = dict(zip(names, refs))
    rows = r["x1"].shape[0]
    hb_ref, proj_ref, ys_ref, m_ref = r["hb"], r["proj"], r["ys"], r["m"]
    h2buf, slot_v, slot_s, carry, sem = r["h2buf"], r["slot_v"], r["slot_s"], r["carry"], r["sem"]
    xs_hbm = r["xs"]
    win_ref, wbr_ref, wout_ref = r["w_in"].at[0], r["w_branch"].at[0], r["w_out"].at[0]
    cbw_ref, cdw_ref, stb_ref, stp_ref, std_ref = r["conv_b_w"], r["conv_d_w"], r["st_b"], r["st_p"], r["st_d"]
    sem_g, sem_d, sem_s = sem.at[0], sem.at[1], sem.at[2]

    mod = r["mod"][...]
    sh1, sc1, g1, sh2, sc2 = (mod[:, i * D_MODEL:(i + 1) * D_MODEL] for i in range(5))
    if first:
        x_src = r["x"]
    else:
        def gather_row(i, c):
            for k in range(2):
                p = slots_prev[k * rows + i]
                pltpu.make_async_copy(r["ys_prev"].at[pl.ds(p, 1)], r["gbuf"].at[k, pl.ds(i, 1)], sem_g).start()
            return c
        lax.fori_loop(0, rows, gather_row, 0, unroll=DMA_UNROLL)
        for _ in range(2):
            pltpu.make_async_copy(r["ys_prev"].at[pl.ds(0, rows)], r["gbuf"].at[0], sem_g).wait()
        x_src = r["xin"]
        x_src[...] = _combine_rows(r["x"][...], r["route_prev"][...], r["mod_prev"][...][:, 5 * D_MODEL:], r["gbuf"])
    h = _rms(x_src[...], r["n1g"][...]) * (1.0 + sc1) + sh1
    _project(h, win_ref, hb_ref, proj_ref)

    u = jax.nn.gelu(proj_ref[:, 0:MIX_W])
    v = _ln(jax.nn.gelu(proj_ref[:, MIX_W:2 * MIX_W]), r["a_ln_g"][...], r["a_ln_b"][...])
    r["v"][...] = v
    ys_ref[0] = (u * (r["a_w0"][...] * v + r["a_b0"][...])).astype(BF16)

    cx = proj_ref[:, 3 * MIX_W:4 * MIX_W] * proj_ref[:, 4 * MIX_W:5 * MIX_W]
    r["cx"][...] = cx
    acc = cbw_ref[CONV_B - 1:CONV_B, :] * _conv_in(cx)
    for k in range(CONV_B - 1):
        acc = acc + cbw_ref[k:k + 1, :] * _conv_in(stb_ref[:, k, :])
    ys_ref[1] = (proj_ref[:, 2 * MIX_W:3 * MIX_W] * acc).astype(BF16)

    pin = proj_ref[:, 5 * MIX_W:6 * MIX_W]
    r["pin"][...] = pin
    pg = MIX_W // len(POOL_WINDOWS)
    mixed = []
    for gi, w in enumerate(POOL_WINDOWS):
        lo = gi * pg
        s = pin[:, lo:lo + pg]
        for i in range(1, w):
            row = POOL_CTX - i
            s = s + stp_ref[:, row, lo:lo + pg]
        pooled = s / float(min(PAST_LEN + 1, w)) - pin[:, lo:lo + pg]
        mixed.append(_bdot(pooled, r["pool_w"][gi]))
    ys_ref[2] = (jnp.concatenate(mixed, axis=1) * r["pool_scale"][...]).astype(BF16)

    gl = proj_ref[:, 6 * MIX_W:7 * MIX_W] * jax.nn.sigmoid(proj_ref[:, 7 * MIX_W:8 * MIX_W])
    r["gl"][...] = gl
    acc = cdw_ref[CONV_D - 1:CONV_D, :] * _conv_in(gl)
    for k in range(CONV_D - 1):
        acc = acc + cdw_ref[k:k + 1, :] * _conv_in(std_ref[:, k, :])
    dc = acc + r["conv_d_b"][...]
    ys_ref[3] = jax.nn.silu(_ln(dc, r["d_ln_g"][...], r["d_ln_b"][...])).astype(BF16)

    for i in range(N_BRANCH):
        _merge_branch(i, hb_ref, win_ref, ys_ref, wbr_ref, m_ref)
    _merge_tail(x_src, g1, sh2, sc2, r["n2g"][...], lambda rs: m_ref[rs, :], wout_ref, r["w_r"], r["b_r"],
                r["x1"], h2buf, r["lg"])

    carry[...] = r["carry_in"][...]
    route = _route_rows(r["lg"][...], carry, cap)
    r["route"][...] = route
    r["carry_out"][...] = carry[...]
    slot_v[...] = jnp.zeros_like(slot_v)
    _slots_to_smem(route, slot_v, slot_s, sem_s)
    r["slots"][...] = slot_v[...]
    pltpu.make_async_copy(slot_v.at[pl.ds(0, 2)], slot_s, sem_s).wait()

    def dispatch_row(i, c):
        for k in range(2):
            pltpu.make_async_copy(h2buf.at[pl.ds(i, 1)], xs_hbm.at[pl.ds(slot_s[k, i], 1)], sem_d).start()
        return c
    lax.fori_loop(0, rows, dispatch_row, 0, unroll=DMA_UNROLL)
    for _ in range(2):
        pltpu.make_async_copy(h2buf, xs_hbm.at[pl.ds(0, rows)], sem_d).wait()


def _sample_mixer(x, mod, st_b, st_p, st_d, lw, li, cap, carry_in, xs, prev=None):
    rows = x.shape[0]
    first = prev is None
    full = lambda a: pl.BlockSpec(a.shape, lambda i, sp: (0,) * a.ndim)
    anyspec = pl.BlockSpec(memory_space=pl.ANY)
    names, ins = ["x", "mod", "carry_in", "xs_in", "st_b", "st_p", "st_d"], [x, mod, carry_in, xs, st_b, st_p, st_d]
    in_specs = [full(x), full(mod), full(carry_in), anyspec, full(st_b), full(st_p), full(st_d)]
    if first:
        slots_prev = jnp.zeros((SUBLANES,), I32)
    else:
        route_prev, mod_prev, ys_prev, slots_prev = prev
        names += ["route_prev", "mod_prev", "ys_prev"]
        ins += [route_prev, mod_prev, ys_prev]
        in_specs += [full(route_prev), full(mod_prev), anyspec]
    names += list(SAMPLE_CONSTS)
    ins += [lw[k] for k in SAMPLE_CONSTS]
    in_specs += [_layer_spec(lw[k].shape, li) if k in PER_LAYER else full(lw[k]) for k in SAMPLE_CONSTS]

    mk = lambda last: jax.ShapeDtypeStruct((rows, last), F32)
    out_names = ["x1", "route", "carry_out", "xs", "slots", "cx", "pin", "gl", "v"]
    out_shape = [mk(D_MODEL), mk(LANES), jax.ShapeDtypeStruct((1, LANES), F32),
                 jax.ShapeDtypeStruct(xs.shape, xs.dtype), jax.ShapeDtypeStruct((SUBLANES, rows), I32),
                 mk(MIX_W), mk(MIX_W), mk(MIX_W), mk(MIX_W)]
    out_specs = [pl.BlockSpec(s.shape, lambda i, sp, nd=len(s.shape): (0,) * nd) for s in out_shape]
    out_specs[3] = anyspec
    scratch = {"hb": pltpu.VMEM((rows, D_MODEL), BF16), "proj": pltpu.VMEM((rows, D_MIX_IN), F32),
               "ys": pltpu.VMEM((N_BRANCH, rows, MIX_W), BF16), "m": pltpu.VMEM((rows, D_MODEL), F32),
               "h2buf": pltpu.VMEM((rows, D_MODEL // 2), U32), "lg": pltpu.VMEM((rows, LANES), F32),
               "slot_v": pltpu.VMEM((SUBLANES, rows), I32), "slot_s": pltpu.SMEM((2, rows), I32),
               "carry": pltpu.VMEM((1, LANES), F32), "sem": pltpu.SemaphoreType.DMA((3,))}
    if not first:
        scratch["xin"] = pltpu.VMEM((rows, D_MODEL), F32)
        scratch["gbuf"] = pltpu.VMEM((2, rows, D_MODEL), F32)
    names = tuple(names + out_names + list(scratch))
    outs = pl.pallas_call(
        functools.partial(_sample_mixer_kernel, names=names, first=first, cap=cap),
        grid_spec=pltpu.PrefetchScalarGridSpec(
            num_scalar_prefetch=1, grid=(1,), in_specs=in_specs, out_specs=out_specs,
            scratch_shapes=list(scratch.values())),
        out_shape=out_shape,
        input_output_aliases={1 + names.index("xs_in"): out_names.index("xs")},
        compiler_params=pltpu.CompilerParams(dimension_semantics=("arbitrary",), vmem_limit_bytes=MIXER_VMEM_BYTES,
                                             disable_bounds_checks=True),
        name="sample_mixer",
    )(slots_prev, *ins)
    return dict(zip(out_names, outs))


def _expert_kernel(vb_ref, be_ref, nv_ref, nu_ref, xs_ref, wg_ref, wu_ref, wd_ref, ys_ref, wgb, wub, wdb):
    b = pl.program_id(0)

    @pl.when(b < nu_ref[0])
    def _():
        @pl.when(jnp.logical_or(b == 0, be_ref[b] != be_ref[jnp.maximum(b - 1, 0)]))
        def _():
            wgb[...] = wg_ref[0, 0].astype(BF16)
            wub[...] = wu_ref[0, 0].astype(BF16)
            wdb[...] = wd_ref[0, 0].astype(BF16)

        valid = lax.broadcasted_iota(I32, (xs_ref.shape[0], 1), 0) < nv_ref[b]
        xb = _unpack_rows(jnp.where(valid, xs_ref[...], jnp.uint32(0)))
        g = jnp.dot(xb, wgb[...], preferred_element_type=F32)
        u = jnp.dot(xb, wub[...], preferred_element_type=F32)
        ys_ref[...] = jnp.dot((jax.nn.silu(g) * u).astype(BF16), wdb[...], preferred_element_type=F32)


def _experts(xs, counts, cap, li, w_eg, w_eu, w_ed):
    bm = MOE_BM
    n_blocks = -(-xs.shape[0] // bm)
    max_used = min(N_EXPERTS * (cap // bm), -(-(cap * 2) // bm) + N_EXPERTS)
    nblk = (counts + bm - 1) // bm
    ends = jnp.cumsum(nblk)
    n_used = ends[-1:]
    b = jnp.arange(max_used, dtype=I32)
    bc = jnp.minimum(b, n_used[0] - 1)
    be = jnp.minimum(jnp.sum(ends[None, :] <= bc[:, None], axis=1), N_EXPERTS - 1).astype(I32)
    mine = be[:, None] == jnp.arange(N_EXPERTS, dtype=I32)[None, :]
    pick = lambda per_expert: jnp.sum(jnp.where(mine, per_expert[None, :], 0), axis=1)
    local = bc - pick(ends - nblk)
    vblock = (be * (cap // bm) + local).astype(I32)
    nvalid = jnp.clip(pick(counts) - local * bm, 0, bm).astype(I32)
    wspec = lambda shape: pl.BlockSpec((1, 1) + shape, lambda i, vb, be_, nv, nu: (li, be_[i], 0, 0))
    return pl.pallas_call(
        _expert_kernel,
        grid_spec=pltpu.PrefetchScalarGridSpec(
            num_scalar_prefetch=4,
            grid=(max_used,),
            in_specs=[pl.BlockSpec((bm, D_MODEL // 2), lambda i, vb, be_, nv, nu: (vb[i], 0)),
                      wspec((D_MODEL, D_EXPERT)), wspec((D_MODEL, D_EXPERT)), wspec((D_EXPERT, D_MODEL))],
            out_specs=pl.BlockSpec((bm, D_MODEL), lambda i, vb, be_, nv, nu: (vb[i], 0)),
            scratch_shapes=[pltpu.VMEM((D_MODEL, D_EXPERT), BF16), pltpu.VMEM((D_MODEL, D_EXPERT), BF16),
                            pltpu.VMEM((D_EXPERT, D_MODEL), BF16)]),
        out_shape=jax.ShapeDtypeStruct((n_blocks * bm, D_MODEL), F32),
        compiler_params=pltpu.CompilerParams(dimension_semantics=("arbitrary",)),
        name="experts",
    )(vblock, be, nvalid, n_used.astype(I32), xs, w_eg, w_eu, w_ed)


def _combine_kernel(pos_ref, ys_hbm, x1_ref, route_ref, g2_ref, fg_ref, out_ref, buf, sem):
    tc = x1_ref.shape[0]
    hc = tc // 2
    i = pl.program_id(0)
    n = pl.num_programs(0)
    g2 = g2_ref[...].reshape(-1, D_MODEL)

    def issue(tile, half):
        for row in range(hc):
            for k in range(2):
                p = pos_ref[(tile * 2 + k) * tc + half * hc + row]
                pltpu.make_async_copy(ys_hbm.at[pl.ds(p, 1)], buf.at[half, k, pl.ds(row, 1)], sem.at[half]).start()

    def finish(half):
        for k in range(2):
            pltpu.make_async_copy(ys_hbm.at[pl.ds(0, hc)], buf.at[half, k], sem.at[half]).wait()
        rs = slice(half * hc, (half + 1) * hc)
        x2 = _combine_rows(x1_ref[rs, :], route_ref[rs, :], g2 if g2.shape[0] == 1 else g2[rs], buf.at[half])
        out_ref[rs, :] = _rms(x2, fg_ref[...])

    @pl.when(i == 0)
    def _():
        issue(0, 0)

    issue(i, 1)
    finish(0)

    @pl.when(i + 1 < n)
    def _():
        issue(i + 1, 0)

    finish(1)


def _combine(ys, pos, tc, x1, route, g2, fg):
    t = x1.shape[0]
    if g2.ndim == 3:
        seq = t // g2.shape[0]
        g2spec = pl.BlockSpec((1, 1, D_MODEL), lambda i, pos: (i * tc // seq, 0, 0))
    else:
        g2spec = pl.BlockSpec((tc, D_MODEL), lambda i, pos: (i, 0))
    tokspec = lambda last: pl.BlockSpec((tc, last), lambda i, pos: (i, 0))
    return pl.pallas_call(
        _combine_kernel,
        grid_spec=pltpu.PrefetchScalarGridSpec(
            num_scalar_prefetch=1,
            grid=(t // tc,),
            in_specs=[pl.BlockSpec(memory_space=pl.ANY), tokspec(D_MODEL), tokspec(LANES), g2spec,
                      pl.BlockSpec((1, D_MODEL), lambda i, pos: (0, 0))],
            out_specs=tokspec(D_MODEL),
            scratch_shapes=[pltpu.VMEM((2, 2, tc // 2, D_MODEL), F32), pltpu.SemaphoreType.DMA((2,))]),
        out_shape=jax.ShapeDtypeStruct((t, D_MODEL), F32),
        compiler_params=pltpu.CompilerParams(dimension_semantics=("arbitrary",), disable_bounds_checks=True,
                                             vmem_limit_bytes=MIXER_VMEM_BYTES),
        name="combine",
    )(pos, ys, x1, route, g2, fg)


def _layer_weights(li, norm1_g, norm2_g, w_in, a_ln_g, a_ln_b, a_ws, a_bs, conv_b_w, pool_w, pool_scale,
                   conv_d_w, conv_d_b, d_ln_g, d_ln_b, w_branch, w_out, w_rg, b_rg, w_re, b_re):
    row = lambda a: a[li].reshape(1, -1)
    cg = MIX_W // A_GROUPS
    pad = LANES - N_GROUPS_MOE - N_EXPERTS
    return {
        "n1g": row(norm1_g), "n2g": row(norm2_g), "w_in": w_in.astype(BF16),
        "a_ln_g": row(a_ln_g), "a_ln_b": row(a_ln_b), "a_ws": a_ws[li],
        "a_bs_full": jnp.repeat(a_bs[li].T, cg, axis=1),
        "a_w0": jnp.repeat(a_ws[li, :, 0, 0], cg).reshape(1, MIX_W),
        "a_b0": jnp.repeat(a_bs[li, :, 0], cg).reshape(1, MIX_W),
        "conv_b_w": conv_b_w[li], "pool_w": pool_w[li], "pool_scale": row(pool_scale),
        "conv_b_w8": jnp.broadcast_to(conv_b_w[li][:, None, :], (CONV_B, SUBLANES, MIX_W)),
        "conv_d_w8": jnp.broadcast_to(conv_d_w[li][:, None, :], (CONV_D, SUBLANES, MIX_W)),
        "conv_d_w": conv_d_w[li], "conv_d_b": row(conv_d_b), "d_ln_g": row(d_ln_g), "d_ln_b": row(d_ln_b),
        "w_branch": w_branch.astype(BF16), "w_out": w_out.astype(BF16),
        "w_r": jnp.concatenate([w_rg[li], w_re[li], jnp.zeros((D_MODEL, pad), F32)], axis=1),
        "b_r": jnp.concatenate([b_rg[li], b_re[li], jnp.zeros((pad,), F32)]).reshape(1, LANES),
    }


def _slots(slot_rows):
    return slot_rows.reshape(-1, SUBLANES, slot_rows.shape[-1])[:, :2].reshape(-1)


def kernel(x_prompt, x_sample, state_conv_b, state_pool, state_conv_d, c_prompt, c_sample, w_mod, b_mod, norm1_g, norm2_g, w_in, a_ln_g, a_ln_b, a_ws, a_bs, conv_b_w, pool_w, pool_scale, conv_d_w, conv_d_b, d_ln_g, d_ln_b, w_branch, w_out, w_rg, b_rg, w_re, b_re, w_eg, w_eu, w_ed, final_g):
    depth = w_mod.shape[0]
    nb, seq, _ = x_prompt.shape
    ns = x_sample.shape[0]
    t = nb * seq
    cap = -(-(t + ns) // MOE_BM) * MOE_BM
    mod_all = _modulation(jnp.concatenate([c_prompt, c_sample], axis=0), w_mod, b_mod)
    fg = final_g.reshape(1, D_MODEL)

    xp = x_prompt
    xs_tok = x_sample.reshape(ns, D_MODEL)
    prev_p = prev_s = None
    outs = {k: [] for k in ("bp", "pp", "dp", "bs", "ps", "ds", "vs")}
    for li in range(depth):
        lw = _layer_weights(li, norm1_g, norm2_g, w_in, a_ln_g, a_ln_b, a_ws, a_bs, conv_b_w, pool_w, pool_scale,
                            conv_d_w, conv_d_b, d_ln_g, d_ln_b, w_branch, w_out, w_rg, b_rg, w_re, b_re)
        mod_p = mod_all[li, :nb].reshape(nb, 1, 6 * D_MODEL)
        mod_s = mod_all[li, nb:]

        pm = _prompt_mixer(xp, mod_p, lw, li, cap, prev_p)
        outs["bp"].append(pm["new_b"])
        outs["pp"].append(pm["new_p"])
        outs["dp"].append(pm["new_d"])

        st_b, st_p, st_d = state_conv_b[li], state_pool[li], state_conv_d[li]
        sm = _sample_mixer(xs_tok, mod_s, st_b, st_p, st_d, lw, li, cap,
                           pm["carry_out"], pm["xs"], prev_s)
        outs["bs"].append(jnp.concatenate([st_b[:, 1:], sm["cx"][:, None]], axis=1))
        outs["ps"].append(jnp.concatenate([st_p[:, 1:], sm["pin"][:, None]], axis=1))
        outs["ds"].append(jnp.concatenate([st_d[:, 1:], sm["gl"][:, None]], axis=1))
        outs["vs"].append(sm["v"][:, None])

        counts = sm["carry_out"][0, ROUTE_LANE0:ROUTE_LANE0 + N_EXPERTS].astype(I32)
        ys = _experts(sm["xs"], counts, cap, li, w_eg, w_eu, w_ed)
        route_p = pm["route"]
        prev_p = (route_p, mod_p, ys, _slots(pm["slots"]))
        prev_s = (sm["route"], mod_s, ys, _slots(sm["slots"]))
        xp, xs_tok = pm["x1"], sm["x1"]

    route_p, mod_p, ys, slots_p = prev_p
    route_s, mod_s, _, slots_s = prev_s
    y_p = _combine(ys, slots_p, MIX_TM, xp.reshape(t, D_MODEL), route_p.reshape(t, LANES), mod_p[:, :, 5 * D_MODEL:], fg)
    y_s = _combine(ys, slots_s, ns, xs_tok, route_s, mod_s[:, 5 * D_MODEL:], fg)

    stack = lambda k: jnp.stack(outs[k])
    return (y_p.reshape(nb, seq, D_MODEL), y_s.reshape(ns, 1, D_MODEL), stack("bp"), stack("pp"), stack("dp"),
            stack("bs"), stack("ps"), stack("ds"), stack("vs"))
```

```python
import functools

import jax
import jax.numpy as jnp
from jax import lax
from jax.experimental import pallas as pl
from jax.experimental.pallas import tpu as pltpu

F32 = jnp.float32
BF16 = jnp.bfloat16
I32 = jnp.int32
U32 = jnp.uint32

D_MODEL = 1024
MIX_W = D_MODEL // 2
CHUNK = 128
A_GROUPS = 4
CONV_B = 3
POOL_WINDOWS = (2, 4, 8, 16)
POOL_CTX = 15
CONV_D = 31
N_BRANCH = 4
N_GROUPS_MOE = 4
EXPERTS_PER_GROUP = 8
N_EXPERTS = N_GROUPS_MOE * EXPERTS_PER_GROUP
D_EXPERT = D_MODEL // 4
EPS = 1e-6
D_MIX_IN = 8 * MIX_W
D_IN = D_MIX_IN + N_BRANCH * D_MODEL
PAST_LEN = 16384

LANES = 128
SUBLANES = 8
ROUTE_LANE0 = N_GROUPS_MOE
ROUTE_E, ROUTE_SLOT, ROUTE_W = 0, 2, 4

MIX_TM = 512
CONV_ROWS = 64
TAIL_ROWS = 256
MXU_COLS = 256
CTX_B = 8
CTX_P = 32
CTX_D = 32
POOL_TMP_SLOTS = sum(w.bit_length() - 2 for w in POOL_WINDOWS)
MOE_BM = 512
DMA_UNROLL = 8
MIXER_VMEM_BYTES = 56 * 1024 * 1024


def _rms(x, g):
    return x * lax.rsqrt(jnp.mean(x * x, axis=-1, keepdims=True) + EPS) * g


def _ln(x, g, b):
    mu = jnp.mean(x, axis=-1, keepdims=True)
    xc = x - mu
    return xc * lax.rsqrt(jnp.mean(xc * xc, axis=-1, keepdims=True) + EPS) * g + b


def _conv_in(x):
    return x.astype(BF16).astype(F32)


def _pack_rows(x):
    c = x.shape[1] // 2
    bits = lambda a: lax.bitcast_convert_type(a.astype(BF16).astype(F32), U32)
    return (bits(x[:, :c]) >> 16) | (bits(x[:, c:]) & jnp.uint32(0xFFFF0000))


def _unpack_rows(p):
    lo = lax.bitcast_convert_type(p << 16, F32)
    hi = lax.bitcast_convert_type(p & jnp.uint32(0xFFFF0000), F32)
    return jnp.concatenate([lo, hi], axis=1).astype(BF16)


def _bdot(a, b):
    return jnp.dot(a.astype(BF16), b.astype(BF16), preferred_element_type=F32)


def _mod_kernel(c_ref, w_ref, b_ref, o_ref):
    o_ref[0] = _bdot(c_ref[...], w_ref[0]) + b_ref[0]


def _modulation(c_all, w_mod, b_mod):
    depth = w_mod.shape[0]
    rows = c_all.shape[0]
    tn = 1536
    return pl.pallas_call(
        _mod_kernel,
        grid=(depth, 6 * D_MODEL // tn),
        in_specs=[pl.BlockSpec((rows, D_MODEL), lambda l, n: (0, 0)),
                  pl.BlockSpec((1, D_MODEL, tn), lambda l, n: (l, 0, n)),
                  pl.BlockSpec((1, 1, tn), lambda l, n: (l, 0, n))],
        out_specs=pl.BlockSpec((1, rows, tn), lambda l, n: (l, 0, n)),
        out_shape=jax.ShapeDtypeStruct((depth, rows, 6 * D_MODEL), F32),
        name="modulation",
    )(c_all, w_mod, b_mod.reshape(depth, 1, 6 * D_MODEL))


def _project(h, win_ref, hb_ref, proj_ref):
    hb_ref[...] = h.astype(BF16)
    for c in range(0, D_MIX_IN, D_MODEL):
        proj_ref[:, c:c + D_MODEL] = jnp.dot(hb_ref[...], win_ref[:, c:c + D_MODEL], preferred_element_type=F32)


def _merge_branch(i, hb_ref, win_ref, ys_ref, wbr_ref, m_ref):
    lo = D_MIX_IN + i * D_MODEL
    rows = hb_ref.shape[0]
    for r0 in range(0, rows, TAIL_ROWS):
        rs = slice(r0, min(r0 + TAIL_ROWS, rows))
        gates = jnp.dot(hb_ref[rs, :], win_ref[:, lo:lo + D_MODEL], preferred_element_type=F32)
        br = jnp.dot(ys_ref[i, rs, :], wbr_ref[i], preferred_element_type=F32)
        t = jax.nn.sigmoid(gates) * br
        m_ref[rs, :] = t if i == 0 else m_ref[rs, :] + t


def _merge_tail(x_ref, g1, sh2, sc2, n2g, m_rows, wout_ref, wr_ref, br_ref, x1_ref, h2_ref, lg_ref):
    rows = x1_ref.shape[0]
    for r0 in range(0, rows, TAIL_ROWS):
        rs = slice(r0, min(r0 + TAIL_ROWS, rows))
        per_row = lambda a: a if a.shape[0] == 1 else a[rs]
        out = jnp.dot(m_rows(rs).astype(BF16), wout_ref[...], preferred_element_type=F32)
        x1 = x_ref[rs, :] + per_row(g1) * out
        h2 = _rms(x1, n2g) * (1.0 + per_row(sc2)) + per_row(sh2)
        x1_ref[rs, :] = x1
        h2_ref[rs, :] = _pack_rows(h2)
        lg_ref[rs, :] = _bdot(h2, wr_ref[...]) + br_ref[...]


def _route_rows(lg, carry_ref, cap):
    tr = lg.shape[0]
    lane = lax.broadcasted_iota(I32, (tr, LANES), 1)
    lanef = lane.astype(F32)

    def first_argmax(vals):
        mx = jnp.max(vals, axis=-1, keepdims=True)
        return mx, jnp.min(jnp.where(vals == mx, lanef, float(LANES)), axis=-1, keepdims=True)

    neg = -jnp.inf
    gl = jnp.where(lane < N_GROUPS_MOE, lg, neg)
    gmax, gsel = first_argmax(gl)
    p_g = 1.0 / jnp.sum(jnp.exp(gl - gmax), axis=-1, keepdims=True)
    e_lo = ROUTE_LANE0 + EXPERTS_PER_GROUP * gsel
    emask = jnp.logical_and(lanef >= e_lo, lanef < e_lo + EXPERTS_PER_GROUP)
    el = jnp.where(emask, lg, neg)
    emax = jnp.max(el, axis=-1, keepdims=True)
    ex = jnp.exp(el - emax)
    prob = jnp.where(emask, ex / jnp.sum(ex, axis=-1, keepdims=True), -1.0)
    p1, i1 = first_argmax(prob)
    p2, i2 = first_argmax(jnp.where(lanef == i1, -1.0, prob))
    w1 = p_g * p1 / (p1 + p2)
    w2 = p_g * p2 / (p1 + p2)

    hot1 = lanef == i1
    hot2 = lanef == i2
    hot = jnp.logical_or(hot1, hot2).astype(F32)
    rr = lax.broadcasted_iota(I32, (tr, tr), 0)
    cc = lax.broadcasted_iota(I32, (tr, tr), 1)
    before = jnp.dot((rr > cc).astype(BF16), hot.astype(BF16), preferred_element_type=F32) + carry_ref[...]
    rank1 = jnp.sum(jnp.where(hot1, before, 0.0), axis=-1, keepdims=True)
    rank2 = jnp.sum(jnp.where(hot2, before, 0.0), axis=-1, keepdims=True)
    carry_ref[...] = carry_ref[...] + jnp.sum(hot, axis=0, keepdims=True)

    e1 = i1 - ROUTE_LANE0
    e2 = i2 - ROUTE_LANE0
    fields = (e1, e2, e1 * float(cap) + rank1, e2 * float(cap) + rank2, w1, w2)
    out = jnp.zeros((tr, LANES), F32)
    for k, f in enumerate(fields):
        out = jnp.where(lane == k, f, out)
    return out


def _slots_to_smem(route, slot_v, slot_s, sem):
    slot_v[0:2, :] = jnp.transpose(route)[ROUTE_SLOT:ROUTE_SLOT + 2, :].astype(I32)
    pltpu.make_async_copy(slot_v.at[pl.ds(0, 2)], slot_s, sem).start()


def _combine_rows(x1_prev, route_prev, g2, gbuf):
    return x1_prev + g2 * (route_prev[:, ROUTE_W:ROUTE_W + 1] * gbuf[0]
                           + route_prev[:, ROUTE_W + 1:ROUTE_W + 2] * gbuf[1])


MIXER_COL_GROUPS = tuple(range(lo * MIX_W // MXU_COLS, hi * MIX_W // MXU_COLS) for lo, hi in ((0, 2), (2, 5), (5, 6), (6, 8)))
PROJ_SLABS = max(len(g) for g in MIXER_COL_GROUPS)
PROJ_SLAB_OF = {g: g - groups[0] for groups in MIXER_COL_GROUPS for g in groups}


class _ProjCols:
    def __init__(self, ref):
        self.ref = ref

    def slab(self, col):
        return self.ref.at[PROJ_SLAB_OF[col // MXU_COLS]]

    def __getitem__(self, idx):
        rows, cols = idx
        lo, hi = cols.start, cols.stop
        parts = []
        while lo < hi:
            g = lo // MXU_COLS
            end = min(hi, (g + 1) * MXU_COLS)
            parts.append(self.slab(lo)[rows, lo - g * MXU_COLS:end - g * MXU_COLS])
            lo = end
        return parts[0] if len(parts) == 1 else jnp.concatenate(parts, axis=1)


def _tap(w8_ref, k, x):
    rows, c = x.shape
    return (x.reshape(rows // SUBLANES, SUBLANES, c) * w8_ref[k][None]).reshape(rows, c)


PROMPT_CONSTS = ("n1g", "n2g", "w_in", "a_ln_g", "a_ln_b", "a_ws", "a_bs_full", "conv_b_w8", "pool_w", "pool_scale",
                 "conv_d_w8", "conv_d_b", "d_ln_g", "d_ln_b", "w_branch", "w_out", "w_r", "b_r")


def _prompt_mixer_kernel(slots_prev, *refs, names, first, cap):
    r = dict(zip(names, refs))
    tm = r["x1"].shape[1]
    hb_ref, proj_ref, ys_ref = r["hb"], _ProjCols(r["proj"]), r["ys"]
    m_ref = r["x1"].at[0]
    cx_ext, p_ext, gl_ext, pool_tmp, conv_tmp = r["cx_ext"], r["p_ext"], r["gl_ext"], r["pool_tmp"], r["conv_tmp"]
    h2buf, slot_v, slot_s, carry, sem = r["h2buf"], r["slot_v"], r["slot_s"], r["carry"], r["sem"]
    xs_hbm = r["xs"]
    win_ref, wbr_ref, wout_ref = r["w_in"].at[0], r["w_branch"].at[0], r["w_out"].at[0]
    j = pl.program_id(1)
    nj = pl.num_programs(1)
    step = pl.program_id(0) * nj + j
    nsteps = pl.num_programs(0) * nj
    trash = N_EXPERTS * cap
    sem_g, sem_d, sem_s = sem.at[0], sem.at[1], sem.at[2]

    def gather_row(tile, row):
        for k in range(2):
            p = slots_prev[(tile * 2 + k) * tm + row]
            pltpu.make_async_copy(r["ys_prev"].at[pl.ds(p, 1)], r["gbuf"].at[k, pl.ds(row, 1)], sem_g).start()

    def dispatch_row(row):
        for k in range(2):
            pltpu.make_async_copy(h2buf.at[pl.ds(row, 1)], xs_hbm.at[pl.ds(slot_s[k, row], 1)], sem_d).start()

    def wait_rows(src, dst, sem_, times):
        for _ in range(times):
            pltpu.make_async_copy(src, dst, sem_).wait()

    @pl.when(step == 0)
    def _():
        carry[...] = r["carry_in"][...]
        h2buf[...] = jnp.zeros_like(h2buf)
        slot_v[...] = (trash + tm * lax.broadcasted_iota(I32, slot_v.shape, 0)
                       + lax.broadcasted_iota(I32, slot_v.shape, 1))
        pltpu.make_async_copy(slot_v.at[pl.ds(0, 2)], slot_s, sem_s).start()
        if not first:
            lax.fori_loop(0, tm, lambda i, c: (gather_row(0, i), c)[1], 0, unroll=DMA_UNROLL)

    @pl.when(j == 0)
    def _():
        cx_ext[0:CTX_B, :] = jnp.zeros((CTX_B, MIX_W), F32)
        p_ext[0:CTX_P, :] = jnp.zeros((CTX_P, MIX_W), F32)
        gl_ext[0:CTX_D, :] = jnp.zeros((CTX_D, MIX_W), F32)

    mod = r["mod"][0]
    sh1, sc1, g1, sh2, sc2 = (mod[:, i * D_MODEL:(i + 1) * D_MODEL] for i in range(5))
    if first:
        x_src = r["x"].at[0]
    else:
        wait_rows(r["ys_prev"].at[pl.ds(0, tm)], r["gbuf"].at[0], sem_g, 2)
        x_src = r["xin"]
        x_src[...] = _combine_rows(r["x"][0], r["route_prev"][0], r["mod_prev"][0][:, 5 * D_MODEL:], r["gbuf"])
    h = _rms(x_src[...], r["n1g"][...]) * (1.0 + sc1) + sh1

    pltpu.make_async_copy(slot_v.at[pl.ds(0, 2)], slot_s, sem_s).wait()

    row_blocks = [slice(r0, r0 + TAIL_ROWS) for r0 in range(0, tm, TAIL_ROWS)]

    def proj_piece(mixer, rs):
        groups = MIXER_COL_GROUPS[mixer]
        lo = groups[0] * MXU_COLS

        def f():
            res = jnp.dot(hb_ref[rs, :], win_ref[:, lo:lo + len(groups) * MXU_COLS], preferred_element_type=F32)
            for n, g in enumerate(groups):
                proj_ref.slab(g * MXU_COLS)[rs, :] = res[:, n * MXU_COLS:(n + 1) * MXU_COLS]
        return f

    def proj_pieces(mixer):
        return [proj_piece(mixer, rs) for rs in row_blocks]

    def merge_piece(i, rs):
        def f():
            lo = D_MIX_IN + i * D_MODEL
            gates = jnp.dot(hb_ref[rs, :], win_ref[:, lo:lo + D_MODEL], preferred_element_type=F32)
            br = jnp.dot(ys_ref[i, rs, :], wbr_ref[i], preferred_element_type=F32)
            t = jax.nn.sigmoid(gates) * br
            m_ref[rs, :] = t if i == 0 else m_ref[rs, :] + t
        return f

    def merge_pieces(i):
        return [merge_piece(i, rs) for rs in row_blocks]

    def emit(vector_pieces, matrix_pieces):
        nv, nm = len(vector_pieces), len(matrix_pieces)
        for i, vp in enumerate(vector_pieces):
            vp()
            for mp in matrix_pieces[i * nm // nv:(i + 1) * nm // nv]:
                mp()

    def dispatch_chunk(i, n):
        def f():
            for row in range(i * tm // n, (i + 1) * tm // n):
                dispatch_row(row)
        return f

    hb_ref[...] = h.astype(BF16)
    n_a = len(MIXER_COL_GROUPS[0])
    emit(proj_pieces(0), [dispatch_chunk(i, n_a) for i in range(n_a)])

    rr = lax.broadcasted_iota(I32, (CHUNK, CHUNK), 0)
    cc = lax.broadcasted_iota(I32, (CHUNK, CHUNK), 1)
    wtril = [jnp.where(rr >= cc, r["a_ws"][g], 0.0).astype(BF16) for g in range(A_GROUPS)]
    cg = MIX_W // A_GROUPS

    def a_chunk(c0):
        def f():
            u = jax.nn.gelu(proj_ref[c0:c0 + CHUNK, 0:MIX_W])
            v = _ln(jax.nn.gelu(proj_ref[c0:c0 + CHUNK, MIX_W:2 * MIX_W]), r["a_ln_g"][...], r["a_ln_b"][...])
            vb = v.astype(BF16)
            mixed = jnp.concatenate(
                [jnp.dot(wtril[g], vb[:, g * cg:(g + 1) * cg], preferred_element_type=F32)
                 for g in range(A_GROUPS)], axis=1) + r["a_bs_full"][...]
            ys_ref[0, c0:c0 + CHUNK, :] = (u * mixed).astype(BF16)
        return f

    emit([a_chunk(c0) for c0 in range(0, tm, CHUNK)], [])
    emit(proj_pieces(1), [])

    cbw_ref = r["conv_b_w8"]

    def b_prep(c0):
        def f():
            cx_ext[CTX_B + c0:CTX_B + c0 + CHUNK, :] = _conv_in(
                proj_ref[c0:c0 + CHUNK, 3 * MIX_W:4 * MIX_W] * proj_ref[c0:c0 + CHUNK, 4 * MIX_W:5 * MIX_W])
        return f

    def b_block(r0):
        def f():
            acc = None
            for k in range(CONV_B):
                t = _tap(cbw_ref, k, cx_ext[pl.ds(r0 + CTX_B - (CONV_B - 1) + k, CONV_ROWS), :])
                acc = t if acc is None else acc + t
            ys_ref[1, r0:r0 + CONV_ROWS, :] = (proj_ref[r0:r0 + CONV_ROWS, 2 * MIX_W:3 * MIX_W] * acc).astype(BF16)
        return f

    def b_end():
        tail = pl.ds(tm - (CONV_B - 1), CONV_B - 1)
        r["new_b"][0] = proj_ref[tail, 3 * MIX_W:4 * MIX_W] * proj_ref[tail, 4 * MIX_W:5 * MIX_W]
        cx_ext[0:CTX_B, :] = cx_ext[tm:tm + CTX_B, :]

    emit([b_prep(c0) for c0 in range(0, tm, CHUNK)] + [b_block(r0) for r0 in range(0, tm, CONV_ROWS)] + [b_end],
         merge_pieces(0))
    emit(proj_pieces(2), [])

    pos = j * tm + lax.broadcasted_iota(I32, (tm, 1), 0)
    pg = MIX_W // len(POOL_WINDOWS)

    def c_prep():
        p_ext[CTX_P:CTX_P + tm, :] = proj_ref[:, 5 * MIX_W:6 * MIX_W]

    def c_group(gi, w):
        def f():
            lo = gi * pg
            levels = w.bit_length() - 1
            for lv in range(levels):
                span = 1 << lv
                start = CTX_P if lv == levels - 1 else SUBLANES * (lv + 1)
                rows = CTX_P + tm - start
                if lv == 0:
                    a = p_ext[start:start + rows, lo:lo + pg] + p_ext[pl.ds(start - span, rows), lo:lo + pg]
                else:
                    prev = pool_tmp.at[gi * (gi - 1) // 2 + lv - 1]
                    a = prev[start:start + rows, :] + prev[pl.ds(start - span, rows), :]
                if lv == levels - 1:
                    s = a
                else:
                    pool_tmp[gi * (gi - 1) // 2 + lv, start:start + rows, :] = a
            cnt = jnp.minimum(pos + 1, w).astype(F32)
            pooled = s / cnt - p_ext[CTX_P:CTX_P + tm, lo:lo + pg]
            ys_ref[2, :, lo:lo + pg] = (_bdot(pooled, r["pool_w"][gi]) * r["pool_scale"][:, lo:lo + pg]).astype(BF16)
        return f

    def c_end():
        r["new_p"][0] = p_ext[pl.ds(CTX_P + tm - POOL_CTX, POOL_CTX), :]
        p_ext[0:CTX_P, :] = p_ext[tm:tm + CTX_P, :]

    emit([c_prep] + proj_pieces(3), [])
    emit([c_group(gi, w) for gi, w in enumerate(POOL_WINDOWS)] + [c_end], merge_pieces(1))

    cdw_ref = r["conv_d_w8"]
    off0 = CTX_D - (CONV_D - 1)
    next_tile = jnp.minimum(step + 1, nsteps - 1)

    def d_prep(c0):
        def f():
            gl_ext[CTX_D + c0:CTX_D + c0 + CHUNK, :] = _conv_in(
                proj_ref[c0:c0 + CHUNK, 6 * MIX_W:7 * MIX_W]
                * jax.nn.sigmoid(proj_ref[c0:c0 + CHUNK, 7 * MIX_W:8 * MIX_W]))
        return f

    def d_block(r0):
        def f():
            acc = None
            for rr_ in range(SUBLANES):
                rows = CONV_ROWS if rr_ == 0 else CONV_ROWS + SUBLANES
                y = None
                for q in range(-(-(off0 + CONV_D) // SUBLANES)):
                    k = SUBLANES * q + rr_ - off0
                    if 0 <= k < CONV_D:
                        t = _tap(cdw_ref, k, gl_ext[r0 + SUBLANES * q:r0 + SUBLANES * q + rows, :])
                        y = t if y is None else y + t
                if rr_ > 0:
                    conv_tmp[rr_ - 1] = y
                    y = conv_tmp[rr_ - 1, pl.ds(rr_, CONV_ROWS), :]
                acc = y if acc is None else acc + y
            dc = acc + r["conv_d_b"][...]
            ys_ref[3, r0:r0 + CONV_ROWS, :] = jax.nn.silu(_ln(dc, r["d_ln_g"][...], r["d_ln_b"][...])).astype(BF16)
            if not first:
                for row in range(r0, r0 + CONV_ROWS):
                    gather_row(next_tile, row)
        return f

    def d_end():
        tail = pl.ds(tm - (CONV_D - 1), CONV_D - 1)
        r["new_d"][0] = proj_ref[tail, 6 * MIX_W:7 * MIX_W] * jax.nn.sigmoid(proj_ref[tail, 7 * MIX_W:8 * MIX_W])
        gl_ext[0:CTX_D, :] = gl_ext[tm:tm + CTX_D, :]

    emit([d_prep(c0) for c0 in range(0, tm, CHUNK)] + [d_block(r0) for r0 in range(0, tm, CONV_ROWS)] + [d_end],
         merge_pieces(2))
    for mp in merge_pieces(3):
        mp()

    wait_rows(h2buf, xs_hbm.at[pl.ds(0, tm)], sem_d, 2)
    _merge_tail(x_src, g1, sh2, sc2, r["n2g"][...], lambda rs: m_ref[rs, :], wout_ref, r["w_r"], r["b_r"],
                r["x1"].at[0], h2buf, r["lg"])
    route = _route_rows(r["lg"][...], carry, cap)
    r["route"][0] = route
    r["carry_out"][...] = carry[...]
    _slots_to_smem(route, slot_v, slot_s, sem_s)
    r["slots"][...] = slot_v[...]

    @pl.when(step == nsteps - 1)
    def _():
        pltpu.make_async_copy(slot_v.at[pl.ds(0, 2)], slot_s, sem_s).wait()
        lax.fori_loop(0, tm, lambda i, c: (dispatch_row(i), c)[1], 0, unroll=DMA_UNROLL)
        wait_rows(h2buf, xs_hbm.at[pl.ds(0, tm)], sem_d, 2)
        if not first:
            wait_rows(r["ys_prev"].at[pl.ds(0, tm)], r["gbuf"].at[0], sem_g, 2)


def _const_spec(shape):
    nd = len(shape)
    return pl.BlockSpec(shape, lambda *_: (0,) * nd, pipeline_mode=pl.Buffered(1))


PER_LAYER = ("w_in", "w_branch", "w_out")


def _layer_spec(shape, li):
    nd = len(shape)
    return pl.BlockSpec((1,) + tuple(shape[1:]), lambda *_: (li,) + (0,) * (nd - 1), pipeline_mode=pl.Buffered(1))


def _prompt_mixer(x, mod, lw, li, cap, prev=None):
    n, l, _ = x.shape
    tm = MIX_TM
    first = prev is None
    tok = lambda last: pl.BlockSpec((1, tm, last), lambda b, j, sp: (b, j, 0))
    per_seq = lambda rows, last: pl.BlockSpec((1, rows, last), lambda b, j, sp: (b, 0, 0))
    anyspec = pl.BlockSpec(memory_space=pl.ANY)
    names, ins, in_specs = ["x", "mod", "carry_in"], [x, mod, jnp.zeros((1, LANES), F32)], \
        [tok(D_MODEL), per_seq(1, 6 * D_MODEL), _const_spec((1, LANES))]
    if first:
        slots_prev = jnp.zeros((SUBLANES,), I32)
    else:
        route_prev, mod_prev, ys_prev, slots_prev = prev
        names += ["route_prev", "mod_prev", "ys_prev"]
        ins += [route_prev, mod_prev, ys_prev]
        in_specs += [tok(LANES), per_seq(1, 6 * D_MODEL), anyspec]
    names += list(PROMPT_CONSTS)
    ins += [lw[k] for k in PROMPT_CONSTS]
    in_specs += [_layer_spec(lw[k].shape, li) if k in PER_LAYER else _const_spec(lw[k].shape) for k in PROMPT_CONSTS]

    xs_rows = N_EXPERTS * cap + 2 * tm
    out_names = ["x1", "route", "carry_out", "xs", "slots", "new_b", "new_p", "new_d"]
    out_specs = [tok(D_MODEL), tok(LANES), pl.BlockSpec((1, LANES), lambda b, j, sp: (0, 0)), anyspec,
                 pl.BlockSpec((SUBLANES, tm), lambda b, j, sp: (b * (l // tm) + j, 0)),
                 per_seq(CONV_B - 1, MIX_W), per_seq(POOL_CTX, MIX_W), per_seq(CONV_D - 1, MIX_W)]
    out_shape = [jax.ShapeDtypeStruct((n, l, D_MODEL), F32), jax.ShapeDtypeStruct((n, l, LANES), F32),
                 jax.ShapeDtypeStruct((1, LANES), F32), jax.ShapeDtypeStruct((xs_rows, D_MODEL // 2), U32),
                 jax.ShapeDtypeStruct((n * (l // tm) * SUBLANES, tm), I32),
                 jax.ShapeDtypeStruct((n, CONV_B - 1, MIX_W), F32), jax.ShapeDtypeStruct((n, POOL_CTX, MIX_W), F32),
                 jax.ShapeDtypeStruct((n, CONV_D - 1, MIX_W), F32)]
    scratch = {"hb": pltpu.VMEM((tm, D_MODEL), BF16), "proj": pltpu.VMEM((PROJ_SLABS, tm, MXU_COLS), F32),
               "ys": pltpu.VMEM((N_BRANCH, tm, MIX_W), BF16),
               "cx_ext": pltpu.VMEM((CTX_B + tm, MIX_W), F32), "p_ext": pltpu.VMEM((CTX_P + tm, MIX_W), F32),
               "gl_ext": pltpu.VMEM((CTX_D + tm, MIX_W), F32),
               "pool_tmp": pltpu.VMEM((POOL_TMP_SLOTS, CTX_P + tm, MIX_W // len(POOL_WINDOWS)), F32),
               "conv_tmp": pltpu.VMEM((SUBLANES - 1, CONV_ROWS + SUBLANES, MIX_W), F32),
               "h2buf": pltpu.VMEM((tm, D_MODEL // 2), U32), "lg": pltpu.VMEM((tm, LANES), F32),
               "slot_v": pltpu.VMEM((SUBLANES, tm), I32), "slot_s": pltpu.SMEM((2, tm), I32),
               "carry": pltpu.VMEM((1, LANES), F32), "sem": pltpu.SemaphoreType.DMA((3,))}
    if not first:
        scratch["xin"] = pltpu.VMEM((tm, D_MODEL), F32)
        scratch["gbuf"] = pltpu.VMEM((2, tm, D_MODEL), F32)
    names = tuple(names + out_names + list(scratch))
    outs = pl.pallas_call(
        functools.partial(_prompt_mixer_kernel, names=names, first=first, cap=cap),
        grid_spec=pltpu.PrefetchScalarGridSpec(
            num_scalar_prefetch=1, grid=(n, l // tm), in_specs=in_specs, out_specs=out_specs,
            scratch_shapes=list(scratch.values())),
        out_shape=out_shape,
        compiler_params=pltpu.CompilerParams(dimension_semantics=("arbitrary", "arbitrary"),
                                             vmem_limit_bytes=MIXER_VMEM_BYTES, disable_bounds_checks=True),
        name="prompt_mixer",
    )(slots_prev, *ins)
    return dict(zip(out_names, outs))


SAMPLE_CONSTS = ("n1g", "n2g", "w_in", "a_ln_g", "a_ln_b", "a_w0", "a_b0", "conv_b_w", "pool_w", "pool_scale",
                 "conv_d_w", "conv_d_b", "d_ln_g", "d_ln_b", "w_branch", "w_out", "w_r", "b_r")


def _sample_mixer_kernel(slots_prev, *refs, names, first, cap):
    r = dict(zip(names, refs))
    rows = r["x1"].shape[0]
    hb_ref, proj_ref, ys_ref, m_ref = r["hb"], r["proj"], r["ys"], r["m"]
    h2buf, slot_v, slot_s, carry, sem = r["h2buf"], r["slot_v"], r["slot_s"], r["carry"], r["sem"]
    xs_hbm = r["xs"]
    win_ref, wbr_ref, wout_ref = r["w_in"].at[0], r["w_branch"].at[0], r["w_out"].at[0]
    cbw_ref, cdw_ref, stb_ref, stp_ref, std_ref = r["conv_b_w"], r["conv_d_w"], r["st_b"], r["st_p"], r["st_d"]
    sem_g, sem_d, sem_s = sem.at[0], sem.at[1], sem.at[2]

    mod = r["mod"][...]
    sh1, sc1, g1, sh2, sc2 = (mod[:, i * D_MODEL:(i + 1) * D_MODEL] for i in range(5))
    if first:
        x_src = r["x"]
    else:
        def gather_row(i, c):
            for k in range(2):
                p = slots_prev[k * rows + i]
                pltpu.make_async_copy(r["ys_prev"].at[pl.ds(p, 1)], r["gbuf"].at[k, pl.ds(i, 1)], sem_g).start()
            return c
        lax.fori_loop(0, rows, gather_row, 0, unroll=DMA_UNROLL)
        for _ in range(2):
            pltpu.make_async_copy(r["ys_prev"].at[pl.ds(0, rows)], r["gbuf"].at[0], sem_g).wait()
        x_src = r["xin"]
        x_src[...] = _combine_rows(r["x"][...], r["route_prev"][...], r["mod_prev"][...][:, 5 * D_MODEL:], r["gbuf"])
    h = _rms(x_src[...], r["n1g"][...]) * (1.0 + sc1) + sh1
    _project(h, win_ref, hb_ref, proj_ref)

    u = jax.nn.gelu(proj_ref[:, 0:MIX_W])
    v = _ln(jax.nn.gelu(proj_ref[:, MIX_W:2 * MIX_W]), r["a_ln_g"][...], r["a_ln_b"][...])
    r["v"][...] = v
    ys_ref[0] = (u * (r["a_w0"][...] * v + r["a_b0"][...])).astype(BF16)

    cx = proj_ref[:, 3 * MIX_W:4 * MIX_W] * proj_ref[:, 4 * MIX_W:5 * MIX_W]
    r["cx"][...] = cx
    acc = cbw_ref[CONV_B - 1:CONV_B, :] * _conv_in(cx)
    for k in range(CONV_B - 1):
        acc = acc + cbw_ref[k:k + 1, :] * _conv_in(stb_ref[:, k, :])
    ys_ref[1] = (proj_ref[:, 2 * MIX_W:3 * MIX_W] * acc).astype(BF16)

    pin = proj_ref[:, 5 * MIX_W:6 * MIX_W]
    r["pin"][...] = pin
    pg = MIX_W // len(POOL_WINDOWS)
    mixed = []
    for gi, w in enumerate(POOL_WINDOWS):
        lo = gi * pg
        s = pin[:, lo:lo + pg]
        for i in range(1, w):
            row = POOL_CTX - i
            s = s + stp_ref[:, row, lo:lo + pg]
        pooled = s / float(min(PAST_LEN + 1, w)) - pin[:, lo:lo + pg]
        mixed.append(_bdot(pooled, r["pool_w"][gi]))
    ys_ref[2] = (jnp.concatenate(mixed, axis=1) * r["pool_scale"][...]).astype(BF16)

    gl = proj_ref[:, 6 * MIX_W:7 * MIX_W] * jax.nn.sigmoid(proj_ref[:, 7 * MIX_W:8 * MIX_W])
    r["gl"][...] = gl
    acc = cdw_ref[CONV_D - 1:CONV_D, :] * _conv_in(gl)
    for k in range(CONV_D - 1):
        acc = acc + cdw_ref[k:k + 1, :] * _conv_in(std_ref[:, k, :])
    dc = acc + r["conv_d_b"][...]
    ys_ref[3] = jax.nn.silu(_ln(dc, r["d_ln_g"][...], r["d_ln_b"][...])).astype(BF16)

    for i in range(N_BRANCH):
        _merge_branch(i, hb_ref, win_ref, ys_ref, wbr_ref, m_ref)
    _merge_tail(x_src, g1, sh2, sc2, r["n2g"][...], lambda rs: m_ref[rs, :], wout_ref, r["w_r"], r["b_r"],
                r["x1"], h2buf, r["lg"])

    carry[...] = r["carry_in"][...]
    route = _route_rows(r["lg"][...], carry, cap)
    r["route"][...] = route
    r["carry_out"][...] = carry[...]
    slot_v[...] = jnp.zeros_like(slot_v)
    _slots_to_smem(route, slot_v, slot_s, sem_s)
    r["slots"][...] = slot_v[...]
    pltpu.make_async_copy(slot_v.at[pl.ds(0, 2)], slot_s, sem_s).wait()

    def dispatch_row(i, c):
        for k in range(2):
            pltpu.make_async_copy(h2buf.at[pl.ds(i, 1)], xs_hbm.at[pl.ds(slot_s[k, i], 1)], sem_d).start()
        return c
    lax.fori_loop(0, rows, dispatch_row, 0, unroll=DMA_UNROLL)
    for _ in range(2):
        pltpu.make_async_copy(h2buf, xs_hbm.at[pl.ds(0, rows)], sem_d).wait()


def _sample_mixer(x, mod, st_b, st_p, st_d, lw, li, cap, carry_in, xs, prev=None):
    rows = x.shape[0]
    first = prev is None
    full = lambda a: pl.BlockSpec(a.shape, lambda i, sp: (0,) * a.ndim)
    anyspec = pl.BlockSpec(memory_space=pl.ANY)
    names, ins = ["x", "mod", "carry_in", "xs_in", "st_b", "st_p", "st_d"], [x, mod, carry_in, xs, st_b, st_p, st_d]
    in_specs = [full(x), full(mod), full(carry_in), anyspec, full(st_b), full(st_p), full(st_d)]
    if first:
        slots_prev = jnp.zeros((SUBLANES,), I32)
    else:
        route_prev, mod_prev, ys_prev, slots_prev = prev
        names += ["route_prev", "mod_prev", "ys_prev"]
        ins += [route_prev, mod_prev, ys_prev]
        in_specs += [full(route_prev), full(mod_prev), anyspec]
    names += list(SAMPLE_CONSTS)
    ins += [lw[k] for k in SAMPLE_CONSTS]
    in_specs += [_layer_spec(lw[k].shape, li) if k in PER_LAYER else full(lw[k]) for k in SAMPLE_CONSTS]

    mk = lambda last: jax.ShapeDtypeStruct((rows, last), F32)
    out_names = ["x1", "route", "carry_out", "xs", "slots", "cx", "pin", "gl", "v"]
    out_shape = [mk(D_MODEL), mk(LANES), jax.ShapeDtypeStruct((1, LANES), F32),
                 jax.ShapeDtypeStruct(xs.shape, xs.dtype), jax.ShapeDtypeStruct((SUBLANES, rows), I32),
                 mk(MIX_W), mk(MIX_W), mk(MIX_W), mk(MIX_W)]
    out_specs = [pl.BlockSpec(s.shape, lambda i, sp, nd=len(s.shape): (0,) * nd) for s in out_shape]
    out_specs[3] = anyspec
    scratch = {"hb": pltpu.VMEM((rows, D_MODEL), BF16), "proj": pltpu.VMEM((rows, D_MIX_IN), F32),
               "ys": pltpu.VMEM((N_BRANCH, rows, MIX_W), BF16), "m": pltpu.VMEM((rows, D_MODEL), F32),
               "h2buf": pltpu.VMEM((rows, D_MODEL // 2), U32), "lg": pltpu.VMEM((rows, LANES), F32),
               "slot_v": pltpu.VMEM((SUBLANES, rows), I32), "slot_s": pltpu.SMEM((2, rows), I32),
               "carry": pltpu.VMEM((1, LANES), F32), "sem": pltpu.SemaphoreType.DMA((3,))}
    if not first:
        scratch["xin"] = pltpu.VMEM((rows, D_MODEL), F32)
        scratch["gbuf"] = pltpu.VMEM((2, rows, D_MODEL), F32)
    names = tuple(names + out_names + list(scratch))
    outs = pl.pallas_call(
        functools.partial(_sample_mixer_kernel, names=names, first=first, cap=cap),
        grid_spec=pltpu.PrefetchScalarGridSpec(
            num_scalar_prefetch=1, grid=(1,), in_specs=in_specs, out_specs=out_specs,
            scratch_shapes=list(scratch.values())),
        out_shape=out_shape,
        input_output_aliases={1 + names.index("xs_in"): out_names.index("xs")},
        compiler_params=pltpu.CompilerParams(dimension_semantics=("arbitrary",), vmem_limit_bytes=MIXER_VMEM_BYTES,
                                             disable_bounds_checks=True),
        name="sample_mixer",
    )(slots_prev, *ins)
    return dict(zip(out_names, outs))


def _expert_kernel(vb_ref, be_ref, nv_ref, nu_ref, xs_ref, wg_ref, wu_ref, wd_ref, ys_ref, wgb, wub, wdb):
    b = pl.program_id(0)

    @pl.when(b < nu_ref[0])
    def _():
        @pl.when(jnp.logical_or(b == 0, be_ref[b] != be_ref[jnp.maximum(b - 1, 0)]))
        def _():
            wgb[...] = wg_ref[0, 0].astype(BF16)
            wub[...] = wu_ref[0, 0].astype(BF16)
            wdb[...] = wd_ref[0, 0].astype(BF16)

        valid = lax.broadcasted_iota(I32, (xs_ref.shape[0], 1), 0) < nv_ref[b]
        xb = _unpack_rows(jnp.where(valid, xs_ref[...], jnp.uint32(0)))
        g = jnp.dot(xb, wgb[...], preferred_element_type=F32)
        u = jnp.dot(xb, wub[...], preferred_element_type=F32)
        ys_ref[...] = jnp.dot((jax.nn.silu(g) * u).astype(BF16), wdb[...], preferred_element_type=F32)


def _experts(xs, counts, cap, li, w_eg, w_eu, w_ed):
    bm = MOE_BM
    n_blocks = -(-xs.shape[0] // bm)
    max_used = min(N_EXPERTS * (cap // bm), -(-(cap * 2) // bm) + N_EXPERTS)
    nblk = (counts + bm - 1) // bm
    ends = jnp.cumsum(nblk)
    n_used = ends[-1:]
    b = jnp.arange(max_used, dtype=I32)
    bc = jnp.minimum(b, n_used[0] - 1)
    be = jnp.minimum(jnp.sum(ends[None, :] <= bc[:, None], axis=1), N_EXPERTS - 1).astype(I32)
    mine = be[:, None] == jnp.arange(N_EXPERTS, dtype=I32)[None, :]
    pick = lambda per_expert: jnp.sum(jnp.where(mine, per_expert[None, :], 0), axis=1)
    local = bc - pick(ends - nblk)
    vblock = (be * (cap // bm) + local).astype(I32)
    nvalid = jnp.clip(pick(counts) - local * bm, 0, bm).astype(I32)
    wspec = lambda shape: pl.BlockSpec((1, 1) + shape, lambda i, vb, be_, nv, nu: (li, be_[i], 0, 0))
    return pl.pallas_call(
        _expert_kernel,
        grid_spec=pltpu.PrefetchScalarGridSpec(
            num_scalar_prefetch=4,
            grid=(max_used,),
            in_specs=[pl.BlockSpec((bm, D_MODEL // 2), lambda i, vb, be_, nv, nu: (vb[i], 0)),
                      wspec((D_MODEL, D_EXPERT)), wspec((D_MODEL, D_EXPERT)), wspec((D_EXPERT, D_MODEL))],
            out_specs=pl.BlockSpec((bm, D_MODEL), lambda i, vb, be_, nv, nu: (vb[i], 0)),
            scratch_shapes=[pltpu.VMEM((D_MODEL, D_EXPERT), BF16), pltpu.VMEM((D_MODEL, D_EXPERT), BF16),
                            pltpu.VMEM((D_EXPERT, D_MODEL), BF16)]),
        out_shape=jax.ShapeDtypeStruct((n_blocks * bm, D_MODEL), F32),
        compiler_params=pltpu.CompilerParams(dimension_semantics=("arbitrary",)),
        name="experts",
    )(vblock, be, nvalid, n_used.astype(I32), xs, w_eg, w_eu, w_ed)


def _combine_kernel(pos_ref, ys_hbm, x1_ref, route_ref, g2_ref, fg_ref, out_ref, buf, sem):
    tc = x1_ref.shape[0]
    hc = tc // 2
    i = pl.program_id(0)
    n = pl.num_programs(0)
    g2 = g2_ref[...].reshape(-1, D_MODEL)

    def issue(tile, half):
        for row in range(hc):
            for k in range(2):
                p = pos_ref[(tile * 2 + k) * tc + half * hc + row]
                pltpu.make_async_copy(ys_hbm.at[pl.ds(p, 1)], buf.at[half, k, pl.ds(row, 1)], sem.at[half]).start()

    def finish(half):
        for k in range(2):
            pltpu.make_async_copy(ys_hbm.at[pl.ds(0, hc)], buf.at[half, k], sem.at[half]).wait()
        rs = slice(half * hc, (half + 1) * hc)
        x2 = _combine_rows(x1_ref[rs, :], route_ref[rs, :], g2 if g2.shape[0] == 1 else g2[rs], buf.at[half])
        out_ref[rs, :] = _rms(x2, fg_ref[...])

    @pl.when(i == 0)
    def _():
        issue(0, 0)

    issue(i, 1)
    finish(0)

    @pl.when(i + 1 < n)
    def _():
        issue(i + 1, 0)

    finish(1)


def _combine(ys, pos, tc, x1, route, g2, fg):
    t = x1.shape[0]
    if g2.ndim == 3:
        seq = t // g2.shape[0]
        g2spec = pl.BlockSpec((1, 1, D_MODEL), lambda i, pos: (i * tc // seq, 0, 0))
    else:
        g2spec = pl.BlockSpec((tc, D_MODEL), lambda i, pos: (i, 0))
    tokspec = lambda last: pl.BlockSpec((tc, last), lambda i, pos: (i, 0))
    return pl.pallas_call(
        _combine_kernel,
        grid_spec=pltpu.PrefetchScalarGridSpec(
            num_scalar_prefetch=1,
            grid=(t // tc,),
            in_specs=[pl.BlockSpec(memory_space=pl.ANY), tokspec(D_MODEL), tokspec(LANES), g2spec,
                      pl.BlockSpec((1, D_MODEL), lambda i, pos: (0, 0))],
            out_specs=tokspec(D_MODEL),
            scratch_shapes=[pltpu.VMEM((2, 2, tc // 2, D_MODEL), F32), pltpu.SemaphoreType.DMA((2,))]),
        out_shape=jax.ShapeDtypeStruct((t, D_MODEL), F32),
        compiler_params=pltpu.CompilerParams(dimension_semantics=("arbitrary",), disable_bounds_checks=True,
                                             vmem_limit_bytes=MIXER_VMEM_BYTES),
        name="combine",
    )(pos, ys, x1, route, g2, fg)


def _layer_weights(li, norm1_g, norm2_g, w_in, a_ln_g, a_ln_b, a_ws, a_bs, conv_b_w, pool_w, pool_scale,
                   conv_d_w, conv_d_b, d_ln_g, d_ln_b, w_branch, w_out, w_rg, b_rg, w_re, b_re):
    row = lambda a: a[li].reshape(1, -1)
    cg = MIX_W // A_GROUPS
    pad = LANES - N_GROUPS_MOE - N_EXPERTS
    return {
        "n1g": row(norm1_g), "n2g": row(norm2_g), "w_in": w_in.astype(BF16),
        "a_ln_g": row(a_ln_g), "a_ln_b": row(a_ln_b), "a_ws": a_ws[li],
        "a_bs_full": jnp.repeat(a_bs[li].T, cg, axis=1),
        "a_w0": jnp.repeat(a_ws[li, :, 0, 0], cg).reshape(1, MIX_W),
        "a_b0": jnp.repeat(a_bs[li, :, 0], cg).reshape(1, MIX_W),
        "conv_b_w": conv_b_w[li], "pool_w": pool_w[li], "pool_scale": row(pool_scale),
        "conv_b_w8": jnp.broadcast_to(conv_b_w[li][:, None, :], (CONV_B, SUBLANES, MIX_W)),
        "conv_d_w8": jnp.broadcast_to(conv_d_w[li][:, None, :], (CONV_D, SUBLANES, MIX_W)),
        "conv_d_w": conv_d_w[li], "conv_d_b": row(conv_d_b), "d_ln_g": row(d_ln_g), "d_ln_b": row(d_ln_b),
        "w_branch": w_branch.astype(BF16), "w_out": w_out.astype(BF16),
        "w_r": jnp.concatenate([w_rg[li], w_re[li], jnp.zeros((D_MODEL, pad), F32)], axis=1),
        "b_r": jnp.concatenate([b_rg[li], b_re[li], jnp.zeros((pad,), F32)]).reshape(1, LANES),
    }


def _slots(slot_rows):
    return slot_rows.reshape(-1, SUBLANES, slot_rows.shape[-1])[:, :2].reshape(-1)


def kernel(x_prompt, x_sample, state_conv_b, state_pool, state_conv_d, c_prompt, c_sample, w_mod, b_mod, norm1_g, norm2_g, w_in, a_ln_g, a_ln_b, a_ws, a_bs, conv_b_w, pool_w, pool_scale, conv_d_w, conv_d_b, d_ln_g, d_ln_b, w_branch, w_out, w_rg, b_rg, w_re, b_re, w_eg, w_eu, w_ed, final_g):
    depth = w_mod.shape[0]
    nb, seq, _ = x_prompt.shape
    ns = x_sample.shape[0]
    t = nb * seq
    cap = -(-(t + ns) // MOE_BM) * MOE_BM
    mod_all = _modulation(jnp.concatenate([c_prompt, c_sample], axis=0), w_mod, b_mod)
    fg = final_g.reshape(1, D_MODEL)

    xp = x_prompt
    xs_tok = x_sample.reshape(ns, D_MODEL)
    prev_p = prev_s = None
    outs = {k: [] for k in ("bp", "pp", "dp", "bs", "ps", "ds", "vs")}
    for li in range(depth):
        lw = _layer_weights(li, norm1_g, norm2_g, w_in, a_ln_g, a_ln_b, a_ws, a_bs, conv_b_w, pool_w, pool_scale,
                            conv_d_w, conv_d_b, d_ln_g, d_ln_b, w_branch, w_out, w_rg, b_rg, w_re, b_re)
        mod_p = mod_all[li, :nb].reshape(nb, 1, 6 * D_MODEL)
        mod_s = mod_all[li, nb:]

        pm = _prompt_mixer(xp, mod_p, lw, li, cap, prev_p)
        outs["bp"].append(pm["new_b"])
        outs["pp"].append(pm["new_p"])
        outs["dp"].append(pm["new_d"])

        st_b, st_p, st_d = state_conv_b[li], state_pool[li], state_conv_d[li]
        sm = _sample_mixer(xs_tok, mod_s, st_b, st_p, st_d, lw, li, cap,
                           pm["carry_out"], pm["xs"], prev_s)
        outs["bs"].append(jnp.concatenate([st_b[:, 1:], sm["cx"][:, None]], axis=1))
        outs["ps"].append(jnp.concatenate([st_p[:, 1:], sm["pin"][:, None]], axis=1))
        outs["ds"].append(jnp.concatenate([st_d[:, 1:], sm["gl"][:, None]], axis=1))
        outs["vs"].append(sm["v"][:, None])

        counts = sm["carry_out"][0, ROUTE_LANE0:ROUTE_LANE0 + N_EXPERTS].astype(I32)
        ys = _experts(sm["xs"], counts, cap, li, w_eg, w_eu, w_ed)
        route_p = pm["route"]
        prev_p = (route_p, mod_p, ys, _slots(pm["slots"]))
        prev_s = (sm["route"], mod_s, ys, _slots(sm["slots"]))
        xp, xs_tok = pm["x1"], sm["x1"]

    route_p, mod_p, ys, slots_p = prev_p
    route_s, mod_s, _, slots_s = prev_s
    y_p = _combine(ys, slots_p, MIX_TM, xp.reshape(t, D_MODEL), route_p.reshape(t, LANES), mod_p[:, :, 5 * D_MODEL:], fg)
    y_s = _combine(ys, slots_s, ns, xs_tok, route_s, mod_s[:, 5 * D_MODEL:], fg)

    stack = lambda k: jnp.stack(outs[k])
    return (y_p.reshape(nb, seq, D_MODEL), y_s.reshape(ns, 1, D_MODEL), stack("bp"), stack("pp"), stack("dp"),
            stack("bs"), stack("ps"), stack("ds"), stack("vs"))
```

```python
import functools

import jax
import jax.numpy as jnp
from jax import lax
from jax.experimental import pallas as pl
from jax.experimental.pallas import tpu as pltpu

F32 = jnp.float32
BF16 = jnp.bfloat16
I32 = jnp.int32
U32 = jnp.uint32

D_MODEL = 1024
MIX_W = D_MODEL // 2
CHUNK = 128
A_GROUPS = 4
CONV_B = 3
POOL_WINDOWS = (2, 4, 8, 16)
POOL_CTX = 15
CONV_D = 31
N_BRANCH = 4
N_GROUPS_MOE = 4
EXPERTS_PER_GROUP = 8
N_EXPERTS = N_GROUPS_MOE * EXPERTS_PER_GROUP
D_EXPERT = D_MODEL // 4
EPS = 1e-6
D_MIX_IN = 8 * MIX_W
D_IN = D_MIX_IN + N_BRANCH * D_MODEL
PAST_LEN = 16384

LANES = 128
SUBLANES = 8
ROUTE_LANE0 = N_GROUPS_MOE
ROUTE_E, ROUTE_SLOT, ROUTE_W = 0, 2, 4

MIX_TM = 512
CONV_ROWS = 64
TAIL_ROWS = 256
MXU_COLS = 256
CTX_B = 8
CTX_P = 32
CTX_D = 32
POOL_TMP_SLOTS = sum(w.bit_length() - 2 for w in POOL_WINDOWS)
MOE_BM = 512
EXPERT_RING = 3
DMA_UNROLL = 8
MIXER_VMEM_BYTES = 56 * 1024 * 1024


def _rms(x, g):
    return x * lax.rsqrt(jnp.mean(x * x, axis=-1, keepdims=True) + EPS) * g


def _ln(x, g, b):
    mu = jnp.mean(x, axis=-1, keepdims=True)
    xc = x - mu
    return xc * lax.rsqrt(jnp.mean(xc * xc, axis=-1, keepdims=True) + EPS) * g + b


def _conv_in(x):
    return x.astype(BF16).astype(F32)


def _pack_rows(x):
    c = x.shape[1] // 2
    bits = lambda a: lax.bitcast_convert_type(a.astype(BF16).astype(F32), U32)
    return (bits(x[:, :c]) >> 16) | (bits(x[:, c:]) & jnp.uint32(0xFFFF0000))


def _unpack_rows(p):
    lo = lax.bitcast_convert_type(p << 16, F32)
    hi = lax.bitcast_convert_type(p & jnp.uint32(0xFFFF0000), F32)
    return jnp.concatenate([lo, hi], axis=1).astype(BF16)


def _bdot(a, b):
    return jnp.dot(a.astype(BF16), b.astype(BF16), preferred_element_type=F32)


def _mod_kernel(c_ref, w_ref, b_ref, o_ref):
    o_ref[0] = _bdot(c_ref[...], w_ref[0]) + b_ref[0]


def _modulation(c_all, w_mod, b_mod):
    depth = w_mod.shape[0]
    rows = c_all.shape[0]
    tn = 1536
    return pl.pallas_call(
        _mod_kernel,
        grid=(depth, 6 * D_MODEL // tn),
        in_specs=[pl.BlockSpec((rows, D_MODEL), lambda l, n: (0, 0)),
                  pl.BlockSpec((1, D_MODEL, tn), lambda l, n: (l, 0, n)),
                  pl.BlockSpec((1, 1, tn), lambda l, n: (l, 0, n))],
        out_specs=pl.BlockSpec((1, rows, tn), lambda l, n: (l, 0, n)),
        out_shape=jax.ShapeDtypeStruct((depth, rows, 6 * D_MODEL), F32),
        name="modulation",
    )(c_all, w_mod, b_mod.reshape(depth, 1, 6 * D_MODEL))


def _project(h, win_ref, hb_ref, proj_ref):
    hb_ref[...] = h.astype(BF16)
    for c in range(0, D_MIX_IN, D_MODEL):
        proj_ref[:, c:c + D_MODEL] = jnp.dot(hb_ref[...], win_ref[:, c:c + D_MODEL], preferred_element_type=F32)


def _merge_branch(i, hb_ref, win_ref, ys_ref, wbr_ref, m_ref):
    lo = D_MIX_IN + i * D_MODEL
    rows = hb_ref.shape[0]
    for r0 in range(0, rows, TAIL_ROWS):
        rs = slice(r0, min(r0 + TAIL_ROWS, rows))
        gates = jnp.dot(hb_ref[rs, :], win_ref[:, lo:lo + D_MODEL], preferred_element_type=F32)
        br = jnp.dot(ys_ref[i, rs, :], wbr_ref[i], preferred_element_type=F32)
        t = jax.nn.sigmoid(gates) * br
        m_ref[rs, :] = t if i == 0 else m_ref[rs, :] + t


def _merge_tail(x_ref, g1, sh2, sc2, n2g, m_rows, wout_ref, wr_ref, br_ref, x1_ref, h2_ref, lg_ref):
    rows = x1_ref.shape[0]
    for r0 in range(0, rows, TAIL_ROWS):
        rs = slice(r0, min(r0 + TAIL_ROWS, rows))
        per_row = lambda a: a if a.shape[0] == 1 else a[rs]
        out = jnp.dot(m_rows(rs).astype(BF16), wout_ref[...], preferred_element_type=F32)
        x1 = x_ref[rs, :] + per_row(g1) * out
        h2 = _rms(x1, n2g) * (1.0 + per_row(sc2)) + per_row(sh2)
        x1_ref[rs, :] = x1
        h2_ref[rs, :] = _pack_rows(h2)
        lg_ref[rs, :] = _bdot(h2, wr_ref[...]) + br_ref[...]


def _route_rows(lg, carry_ref, cap):
    tr = lg.shape[0]
    lane = lax.broadcasted_iota(I32, (tr, LANES), 1)
    lanef = lane.astype(F32)

    def first_argmax(vals):
        mx = jnp.max(vals, axis=-1, keepdims=True)
        return mx, jnp.min(jnp.where(vals == mx, lanef, float(LANES)), axis=-1, keepdims=True)

    neg = -jnp.inf
    gl = jnp.where(lane < N_GROUPS_MOE, lg, neg)
    gmax, gsel = first_argmax(gl)
    p_g = 1.0 / jnp.sum(jnp.exp(gl - gmax), axis=-1, keepdims=True)
    e_lo = ROUTE_LANE0 + EXPERTS_PER_GROUP * gsel
    emask = jnp.logical_and(lanef >= e_lo, lanef < e_lo + EXPERTS_PER_GROUP)
    el = jnp.where(emask, lg, neg)
    emax = jnp.max(el, axis=-1, keepdims=True)
    ex = jnp.exp(el - emax)
    prob = jnp.where(emask, ex / jnp.sum(ex, axis=-1, keepdims=True), -1.0)
    p1, i1 = first_argmax(prob)
    p2, i2 = first_argmax(jnp.where(lanef == i1, -1.0, prob))
    w1 = p_g * p1 / (p1 + p2)
    w2 = p_g * p2 / (p1 + p2)

    hot1 = lanef == i1
    hot2 = lanef == i2
    hot = jnp.logical_or(hot1, hot2).astype(F32)
    rr = lax.broadcasted_iota(I32, (tr, tr), 0)
    cc = lax.broadcasted_iota(I32, (tr, tr), 1)
    before = jnp.dot((rr > cc).astype(BF16), hot.astype(BF16), preferred_element_type=F32) + carry_ref[...]
    rank1 = jnp.sum(jnp.where(hot1, before, 0.0), axis=-1, keepdims=True)
    rank2 = jnp.sum(jnp.where(hot2, before, 0.0), axis=-1, keepdims=True)
    carry_ref[...] = carry_ref[...] + jnp.sum(hot, axis=0, keepdims=True)

    e1 = i1 - ROUTE_LANE0
    e2 = i2 - ROUTE_LANE0
    fields = (e1, e2, e1 * float(cap) + rank1, e2 * float(cap) + rank2, w1, w2)
    out = jnp.zeros((tr, LANES), F32)
    for k, f in enumerate(fields):
        out = jnp.where(lane == k, f, out)
    return out


def _slots_to_smem(route, slot_v, slot_s, sem):
    slot_v[0:2, :] = jnp.transpose(route)[ROUTE_SLOT:ROUTE_SLOT + 2, :].astype(I32)
    pltpu.make_async_copy(slot_v.at[pl.ds(0, 2)], slot_s, sem).start()


def _combine_rows(x1_prev, route_prev, g2, gbuf):
    return x1_prev + g2 * (route_prev[:, ROUTE_W:ROUTE_W + 1] * gbuf[0]
                           + route_prev[:, ROUTE_W + 1:ROUTE_W + 2] * gbuf[1])


MIXER_COL_GROUPS = tuple(range(lo * MIX_W // MXU_COLS, hi * MIX_W // MXU_COLS) for lo, hi in ((0, 2), (2, 5), (5, 6), (6, 8)))
PROJ_SLABS = max(len(g) for g in MIXER_COL_GROUPS)
PROJ_SLAB_OF = {g: g - groups[0] for groups in MIXER_COL_GROUPS for g in groups}


class _ProjCols:
    def __init__(self, ref):
        self.ref = ref

    def slab(self, col):
        return self.ref.at[PROJ_SLAB_OF[col // MXU_COLS]]

    def __getitem__(self, idx):
        rows, cols = idx
        lo, hi = cols.start, cols.stop
        parts = []
        while lo < hi:
            g = lo // MXU_COLS
            end = min(hi, (g + 1) * MXU_COLS)
            parts.append(self.slab(lo)[rows, lo - g * MXU_COLS:end - g * MXU_COLS])
            lo = end
        return parts[0] if len(parts) == 1 else jnp.concatenate(parts, axis=1)


def _tap(w8_ref, k, x):
    rows, c = x.shape
    return (x.reshape(rows // SUBLANES, SUBLANES, c) * w8_ref[k][None]).reshape(rows, c)


PROMPT_CONSTS = ("n1g", "n2g", "w_in", "a_ln_g", "a_ln_b", "a_ws", "a_bs_full", "conv_b_w8", "pool_w", "pool_scale",
                 "conv_d_w8", "conv_d_b", "d_ln_g", "d_ln_b", "w_branch", "w_out", "w_r", "b_r")


def _prompt_mixer_kernel(slots_prev, *refs, names, first, cap):
    r = dict(zip(names, refs))
    tm = r["x1"].shape[1]
    hb_ref, proj_ref, ys_ref = r["hb"], _ProjCols(r["proj"]), r["ys"]
    m_ref = r["x1"].at[0]
    cx_ext, p_ext, gl_ext, pool_tmp, conv_tmp = r["cx_ext"], r["p_ext"], r["gl_ext"], r["pool_tmp"], r["conv_tmp"]
    h2buf, slot_v, slot_s, carry, sem = r["h2buf"], r["slot_v"], r["slot_s"], r["carry"], r["sem"]
    xs_hbm = r["xs"]
    win_ref, wbr_ref, wout_ref = r["w_in"].at[0], r["w_branch"].at[0], r["w_out"].at[0]
    j = pl.program_id(1)
    nj = pl.num_programs(1)
    step = pl.program_id(0) * nj + j
    nsteps = pl.num_programs(0) * nj
    trash = N_EXPERTS * cap
    sem_g, sem_d, sem_s = sem.at[0], sem.at[1], sem.at[2]

    def gather_row(tile, row):
        for k in range(2):
            p = slots_prev[(tile * 2 + k) * tm + row]
            pltpu.make_async_copy(r["ys_prev"].at[pl.ds(p, 1)], r["gbuf"].at[k, pl.ds(row, 1)], sem_g).start()

    def dispatch_row(row):
        for k in range(2):
            pltpu.make_async_copy(h2buf.at[pl.ds(row, 1)], xs_hbm.at[pl.ds(slot_s[k, row], 1)], sem_d).start()

    def wait_rows(src, dst, sem_, times):
        for _ in range(times):
            pltpu.make_async_copy(src, dst, sem_).wait()

    @pl.when(step == 0)
    def _():
        carry[...] = r["carry_in"][...]
        h2buf[...] = jnp.zeros_like(h2buf)
        slot_v[...] = (trash + tm * lax.broadcasted_iota(I32, slot_v.shape, 0)
                       + lax.broadcasted_iota(I32, slot_v.shape, 1))
        pltpu.make_async_copy(slot_v.at[pl.ds(0, 2)], slot_s, sem_s).start()
        if not first:
            lax.fori_loop(0, tm, lambda i, c: (gather_row(0, i), c)[1], 0, unroll=DMA_UNROLL)

    @pl.when(j == 0)
    def _():
        cx_ext[0:CTX_B, :] = jnp.zeros((CTX_B, MIX_W), F32)
        p_ext[0:CTX_P, :] = jnp.zeros((CTX_P, MIX_W), F32)
        gl_ext[0:CTX_D, :] = jnp.zeros((CTX_D, MIX_W), F32)

    mod = r["mod"][0]
    sh1, sc1, g1, sh2, sc2 = (mod[:, i * D_MODEL:(i + 1) * D_MODEL] for i in range(5))
    if first:
        x_src = r["x"].at[0]
    else:
        wait_rows(r["ys_prev"].at[pl.ds(0, tm)], r["gbuf"].at[0], sem_g, 2)
        x_src = r["xin"]
        x_src[...] = _combine_rows(r["x"][0], r["route_prev"][0], r["mod_prev"][0][:, 5 * D_MODEL:], r["gbuf"])
    h = _rms(x_src[...], r["n1g"][...]) * (1.0 + sc1) + sh1

    pltpu.make_async_copy(slot_v.at[pl.ds(0, 2)], slot_s, sem_s).wait()

    row_blocks = [slice(r0, r0 + TAIL_ROWS) for r0 in range(0, tm, TAIL_ROWS)]

    def proj_piece(mixer, rs):
        groups = MIXER_COL_GROUPS[mixer]
        lo = groups[0] * MXU_COLS

        def f():
            res = jnp.dot(hb_ref[rs, :], win_ref[:, lo:lo + len(groups) * MXU_COLS], preferred_element_type=F32)
            for n, g in enumerate(groups):
                proj_ref.slab(g * MXU_COLS)[rs, :] = res[:, n * MXU_COLS:(n + 1) * MXU_COLS]
        return f

    def proj_pieces(mixer):
        return [proj_piece(mixer, rs) for rs in row_blocks]

    def merge_piece(i, rs):
        def f():
            lo = D_MIX_IN + i * D_MODEL
            gates = jnp.dot(hb_ref[rs, :], win_ref[:, lo:lo + D_MODEL], preferred_element_type=F32)
            br = jnp.dot(ys_ref[i, rs, :], wbr_ref[i], preferred_element_type=F32)
            t = jax.nn.sigmoid(gates) * br
            m_ref[rs, :] = t if i == 0 else m_ref[rs, :] + t
        return f

    def merge_pieces(i):
        return [merge_piece(i, rs) for rs in row_blocks]

    def emit(vector_pieces, matrix_pieces):
        nv, nm = len(vector_pieces), len(matrix_pieces)
        for i, vp in enumerate(vector_pieces):
            vp()
            for mp in matrix_pieces[i * nm // nv:(i + 1) * nm // nv]:
                mp()

    def dispatch_chunk(i, n):
        def f():
            for row in range(i * tm // n, (i + 1) * tm // n):
                dispatch_row(row)
        return f

    hb_ref[...] = h.astype(BF16)
    n_a = len(MIXER_COL_GROUPS[0])
    emit(proj_pieces(0), [dispatch_chunk(i, n_a) for i in range(n_a)])

    rr = lax.broadcasted_iota(I32, (CHUNK, CHUNK), 0)
    cc = lax.broadcasted_iota(I32, (CHUNK, CHUNK), 1)
    wtril = [jnp.where(rr >= cc, r["a_ws"][g], 0.0).astype(BF16) for g in range(A_GROUPS)]
    cg = MIX_W // A_GROUPS

    def a_chunk(c0):
        def f():
            u = jax.nn.gelu(proj_ref[c0:c0 + CHUNK, 0:MIX_W])
            v = _ln(jax.nn.gelu(proj_ref[c0:c0 + CHUNK, MIX_W:2 * MIX_W]), r["a_ln_g"][...], r["a_ln_b"][...])
            vb = v.astype(BF16)
            mixed = jnp.concatenate(
                [jnp.dot(wtril[g], vb[:, g * cg:(g + 1) * cg], preferred_element_type=F32)
                 for g in range(A_GROUPS)], axis=1) + r["a_bs_full"][...]
            ys_ref[0, c0:c0 + CHUNK, :] = (u * mixed).astype(BF16)
        return f

    emit([a_chunk(c0) for c0 in range(0, tm, CHUNK)], [])
    emit(proj_pieces(1), [])

    cbw_ref = r["conv_b_w8"]

    def b_prep(c0):
        def f():
            cx_ext[CTX_B + c0:CTX_B + c0 + CHUNK, :] = _conv_in(
                proj_ref[c0:c0 + CHUNK, 3 * MIX_W:4 * MIX_W] * proj_ref[c0:c0 + CHUNK, 4 * MIX_W:5 * MIX_W])
        return f

    def b_block(r0):
        def f():
            acc = None
            for k in range(CONV_B):
                t = _tap(cbw_ref, k, cx_ext[pl.ds(r0 + CTX_B - (CONV_B - 1) + k, CONV_ROWS), :])
                acc = t if acc is None else acc + t
            ys_ref[1, r0:r0 + CONV_ROWS, :] = (proj_ref[r0:r0 + CONV_ROWS, 2 * MIX_W:3 * MIX_W] * acc).astype(BF16)
        return f

    def b_end():
        tail = pl.ds(tm - (CONV_B - 1), CONV_B - 1)
        r["new_b"][0] = proj_ref[tail, 3 * MIX_W:4 * MIX_W] * proj_ref[tail, 4 * MIX_W:5 * MIX_W]
        cx_ext[0:CTX_B, :] = cx_ext[tm:tm + CTX_B, :]

    emit([b_prep(c0) for c0 in range(0, tm, CHUNK)] + [b_block(r0) for r0 in range(0, tm, CONV_ROWS)] + [b_end],
         merge_pieces(0))
    emit(proj_pieces(2), [])

    pos = j * tm + lax.broadcasted_iota(I32, (tm, 1), 0)
    pg = MIX_W // len(POOL_WINDOWS)

    def c_prep():
        p_ext[CTX_P:CTX_P + tm, :] = proj_ref[:, 5 * MIX_W:6 * MIX_W]

    def c_group(gi, w):
        def f():
            lo = gi * pg
            levels = w.bit_length() - 1
            for lv in range(levels):
                span = 1 << lv
                start = CTX_P if lv == levels - 1 else SUBLANES * (lv + 1)
                rows = CTX_P + tm - start
                if lv == 0:
                    a = p_ext[start:start + rows, lo:lo + pg] + p_ext[pl.ds(start - span, rows), lo:lo + pg]
                else:
                    prev = pool_tmp.at[gi * (gi - 1) // 2 + lv - 1]
                    a = prev[start:start + rows, :] + prev[pl.ds(start - span, rows), :]
                if lv == levels - 1:
                    s = a
                else:
                    pool_tmp[gi * (gi - 1) // 2 + lv, start:start + rows, :] = a
            cnt = jnp.minimum(pos + 1, w).astype(F32)
            pooled = s / cnt - p_ext[CTX_P:CTX_P + tm, lo:lo + pg]
            ys_ref[2, :, lo:lo + pg] = (_bdot(pooled, r["pool_w"][gi]) * r["pool_scale"][:, lo:lo + pg]).astype(BF16)
        return f

    def c_end():
        r["new_p"][0] = p_ext[pl.ds(CTX_P + tm - POOL_CTX, POOL_CTX), :]
        p_ext[0:CTX_P, :] = p_ext[tm:tm + CTX_P, :]

    emit([c_prep] + proj_pieces(3), [])
    emit([c_group(gi, w) for gi, w in enumerate(POOL_WINDOWS)] + [c_end], merge_pieces(1))

    cdw_ref = r["conv_d_w8"]
    off0 = CTX_D - (CONV_D - 1)
    next_tile = jnp.minimum(step + 1, nsteps - 1)

    def d_prep(c0):
        def f():
            gl_ext[CTX_D + c0:CTX_D + c0 + CHUNK, :] = _conv_in(
                proj_ref[c0:c0 + CHUNK, 6 * MIX_W:7 * MIX_W]
                * jax.nn.sigmoid(proj_ref[c0:c0 + CHUNK, 7 * MIX_W:8 * MIX_W]))
        return f

    def d_block(r0):
        def f():
            acc = None
            for rr_ in range(SUBLANES):
                rows = CONV_ROWS if rr_ == 0 else CONV_ROWS + SUBLANES
                y = None
                for q in range(-(-(off0 + CONV_D) // SUBLANES)):
                    k = SUBLANES * q + rr_ - off0
                    if 0 <= k < CONV_D:
                        t = _tap(cdw_ref, k, gl_ext[r0 + SUBLANES * q:r0 + SUBLANES * q + rows, :])
                        y = t if y is None else y + t
                if rr_ > 0:
                    conv_tmp[rr_ - 1] = y
                    y = conv_tmp[rr_ - 1, pl.ds(rr_, CONV_ROWS), :]
                acc = y if acc is None else acc + y
            dc = acc + r["conv_d_b"][...]
            ys_ref[3, r0:r0 + CONV_ROWS, :] = jax.nn.silu(_ln(dc, r["d_ln_g"][...], r["d_ln_b"][...])).astype(BF16)
            if not first:
                for row in range(r0, r0 + CONV_ROWS):
                    gather_row(next_tile, row)
        return f

    def d_end():
        tail = pl.ds(tm - (CONV_D - 1), CONV_D - 1)
        r["new_d"][0] = proj_ref[tail, 6 * MIX_W:7 * MIX_W] * jax.nn.sigmoid(proj_ref[tail, 7 * MIX_W:8 * MIX_W])
        gl_ext[0:CTX_D, :] = gl_ext[tm:tm + CTX_D, :]

    emit([d_prep(c0) for c0 in range(0, tm, CHUNK)] + [d_block(r0) for r0 in range(0, tm, CONV_ROWS)] + [d_end],
         merge_pieces(2))
    for mp in merge_pieces(3):
        mp()

    wait_rows(h2buf, xs_hbm.at[pl.ds(0, tm)], sem_d, 2)
    _merge_tail(x_src, g1, sh2, sc2, r["n2g"][...], lambda rs: m_ref[rs, :], wout_ref, r["w_r"], r["b_r"],
                r["x1"].at[0], h2buf, r["lg"])
    route = _route_rows(r["lg"][...], carry, cap)
    r["route"][0] = route
    r["carry_out"][...] = carry[...]
    _slots_to_smem(route, slot_v, slot_s, sem_s)
    r["slots"][...] = slot_v[...]

    @pl.when(step == nsteps - 1)
    def _():
        pltpu.make_async_copy(slot_v.at[pl.ds(0, 2)], slot_s, sem_s).wait()
        lax.fori_loop(0, tm, lambda i, c: (dispatch_row(i), c)[1], 0, unroll=DMA_UNROLL)
        wait_rows(h2buf, xs_hbm.at[pl.ds(0, tm)], sem_d, 2)
        if not first:
            wait_rows(r["ys_prev"].at[pl.ds(0, tm)], r["gbuf"].at[0], sem_g, 2)


def _const_spec(shape):
    nd = len(shape)
    return pl.BlockSpec(shape, lambda *_: (0,) * nd, pipeline_mode=pl.Buffered(1))


PER_LAYER = ("w_in", "w_branch", "w_out")


def _layer_spec(shape, li):
    nd = len(shape)
    return pl.BlockSpec((1,) + tuple(shape[1:]), lambda *_: (li,) + (0,) * (nd - 1), pipeline_mode=pl.Buffered(1))


def _prompt_mixer(x, mod, lw, li, cap, prev=None):
    n, l, _ = x.shape
    tm = MIX_TM
    first = prev is None
    tok = lambda last: pl.BlockSpec((1, tm, last), lambda b, j, sp: (b, j, 0))
    per_seq = lambda rows, last: pl.BlockSpec((1, rows, last), lambda b, j, sp: (b, 0, 0))
    anyspec = pl.BlockSpec(memory_space=pl.ANY)
    names, ins, in_specs = ["x", "mod", "carry_in"], [x, mod, jnp.zeros((1, LANES), F32)], \
        [tok(D_MODEL), per_seq(1, 6 * D_MODEL), _const_spec((1, LANES))]
    if first:
        slots_prev = jnp.zeros((SUBLANES,), I32)
    else:
        route_prev, mod_prev, ys_prev, slots_prev = prev
        names += ["route_prev", "mod_prev", "ys_prev"]
        ins += [route_prev, mod_prev, ys_prev]
        in_specs += [tok(LANES), per_seq(1, 6 * D_MODEL), anyspec]
    names += list(PROMPT_CONSTS)
    ins += [lw[k] for k in PROMPT_CONSTS]
    in_specs += [_layer_spec(lw[k].shape, li) if k in PER_LAYER else _const_spec(lw[k].shape) for k in PROMPT_CONSTS]

    xs_rows = N_EXPERTS * cap + 2 * tm
    out_names = ["x1", "route", "carry_out", "xs", "slots", "new_b", "new_p", "new_d"]
    out_specs = [tok(D_MODEL), tok(LANES), pl.BlockSpec((1, LANES), lambda b, j, sp: (0, 0)), anyspec,
                 pl.BlockSpec((SUBLANES, tm), lambda b, j, sp: (b * (l // tm) + j, 0)),
                 per_seq(CONV_B - 1, MIX_W), per_seq(POOL_CTX, MIX_W), per_seq(CONV_D - 1, MIX_W)]
    out_shape = [jax.ShapeDtypeStruct((n, l, D_MODEL), F32), jax.ShapeDtypeStruct((n, l, LANES), F32),
                 jax.ShapeDtypeStruct((1, LANES), F32), jax.ShapeDtypeStruct((xs_rows, D_MODEL // 2), U32),
                 jax.ShapeDtypeStruct((n * (l // tm) * SUBLANES, tm), I32),
                 jax.ShapeDtypeStruct((n, CONV_B - 1, MIX_W), F32), jax.ShapeDtypeStruct((n, POOL_CTX, MIX_W), F32),
                 jax.ShapeDtypeStruct((n, CONV_D - 1, MIX_W), F32)]
    scratch = {"hb": pltpu.VMEM((tm, D_MODEL), BF16), "proj": pltpu.VMEM((PROJ_SLABS, tm, MXU_COLS), F32),
               "ys": pltpu.VMEM((N_BRANCH, tm, MIX_W), BF16),
               "cx_ext": pltpu.VMEM((CTX_B + tm, MIX_W), F32), "p_ext": pltpu.VMEM((CTX_P + tm, MIX_W), F32),
               "gl_ext": pltpu.VMEM((CTX_D + tm, MIX_W), F32),
               "pool_tmp": pltpu.VMEM((POOL_TMP_SLOTS, CTX_P + tm, MIX_W // len(POOL_WINDOWS)), F32),
               "conv_tmp": pltpu.VMEM((SUBLANES - 1, CONV_ROWS + SUBLANES, MIX_W), F32),
               "h2buf": pltpu.VMEM((tm, D_MODEL // 2), U32), "lg": pltpu.VMEM((tm, LANES), F32),
               "slot_v": pltpu.VMEM((SUBLANES, tm), I32), "slot_s": pltpu.SMEM((2, tm), I32),
               "carry": pltpu.VMEM((1, LANES), F32), "sem": pltpu.SemaphoreType.DMA((3,))}
    if not first:
        scratch["xin"] = pltpu.VMEM((tm, D_MODEL), F32)
        scratch["gbuf"] = pltpu.VMEM((2, tm, D_MODEL), F32)
    names = tuple(names + out_names + list(scratch))
    outs = pl.pallas_call(
        functools.partial(_prompt_mixer_kernel, names=names, first=first, cap=cap),
        grid_spec=pltpu.PrefetchScalarGridSpec(
            num_scalar_prefetch=1, grid=(n, l // tm), in_specs=in_specs, out_specs=out_specs,
            scratch_shapes=list(scratch.values())),
        out_shape=out_shape,
        compiler_params=pltpu.CompilerParams(dimension_semantics=("arbitrary", "arbitrary"),
                                             vmem_limit_bytes=MIXER_VMEM_BYTES, disable_bounds_checks=True),
        name="prompt_mixer",
    )(slots_prev, *ins)
    return dict(zip(out_names, outs))


SAMPLE_CONSTS = ("n1g", "n2g", "w_in", "a_ln_g", "a_ln_b", "a_w0", "a_b0", "conv_b_w", "pool_w", "pool_scale",
                 "conv_d_w", "conv_d_b", "d_ln_g", "d_ln_b", "w_branch", "w_out", "w_r", "b_r")


def _sample_mixer_kernel(slots_prev, *refs, names, first, cap):
    r = dict(zip(names, refs))
    rows = r["x1"].shape[0]
    hb_ref, proj_ref, ys_ref, m_ref = r["hb"], r["proj"], r["ys"], r["m"]
    h2buf, slot_v, slot_s, carry, sem = r["h2buf"], r["slot_v"], r["slot_s"], r["carry"], r["sem"]
    xs_hbm = r["xs"]
    win_ref, wbr_ref, wout_ref = r["w_in"].at[0], r["w_branch"].at[0], r["w_out"].at[0]
    cbw_ref, cdw_ref, stb_ref, stp_ref, std_ref = r["conv_b_w"], r["conv_d_w"], r["st_b"], r["st_p"], r["st_d"]
    sem_g, sem_d, sem_s = sem.at[0], sem.at[1], sem.at[2]

    mod = r["mod"][...]
    sh1, sc1, g1, sh2, sc2 = (mod[:, i * D_MODEL:(i + 1) * D_MODEL] for i in range(5))
    if first:
        x_src = r["x"]
    else:
        def gather_row(i, c):
            for k in range(2):
                p = slots_prev[k * rows + i]
                pltpu.make_async_copy(r["ys_prev"].at[pl.ds(p, 1)], r["gbuf"].at[k, pl.ds(i, 1)], sem_g).start()
            return c
        lax.fori_loop(0, rows, gather_row, 0, unroll=DMA_UNROLL)
        for _ in range(2):
            pltpu.make_async_copy(r["ys_prev"].at[pl.ds(0, rows)], r["gbuf"].at[0], sem_g).wait()
        x_src = r["xin"]
        x_src[...] = _combine_rows(r["x"][...], r["route_prev"][...], r["mod_prev"][...][:, 5 * D_MODEL:], r["gbuf"])
    h = _rms(x_src[...], r["n1g"][...]) * (1.0 + sc1) + sh1
    _project(h, win_ref, hb_ref, proj_ref)

    u = jax.nn.gelu(proj_ref[:, 0:MIX_W])
    v = _ln(jax.nn.gelu(proj_ref[:, MIX_W:2 * MIX_W]), r["a_ln_g"][...], r["a_ln_b"][...])
    r["v"][...] = v
    ys_ref[0] = (u * (r["a_w0"][...] * v + r["a_b0"][...])).astype(BF16)

    cx = proj_ref[:, 3 * MIX_W:4 * MIX_W] * proj_ref[:, 4 * MIX_W:5 * MIX_W]
    r["cx"][...] = cx
    acc = cbw_ref[CONV_B - 1:CONV_B, :] * _conv_in(cx)
    for k in range(CONV_B - 1):
        acc = acc + cbw_ref[k:k + 1, :] * _conv_in(stb_ref[:, k, :])
    ys_ref[1] = (proj_ref[:, 2 * MIX_W:3 * MIX_W] * acc).astype(BF16)

    pin = proj_ref[:, 5 * MIX_W:6 * MIX_W]
    r["pin"][...] = pin
    pg = MIX_W // len(POOL_WINDOWS)
    mixed = []
    for gi, w in enumerate(POOL_WINDOWS):
        lo = gi * pg
        s = pin[:, lo:lo + pg]
        for i in range(1, w):
            row = POOL_CTX - i
            s = s + stp_ref[:, row, lo:lo + pg]
        pooled = s / float(min(PAST_LEN + 1, w)) - pin[:, lo:lo + pg]
        mixed.append(_bdot(pooled, r["pool_w"][gi]))
    ys_ref[2] = (jnp.concatenate(mixed, axis=1) * r["pool_scale"][...]).astype(BF16)

    gl = proj_ref[:, 6 * MIX_W:7 * MIX_W] * jax.nn.sigmoid(proj_ref[:, 7 * MIX_W:8 * MIX_W])
    r["gl"][...] = gl
    acc = cdw_ref[CONV_D - 1:CONV_D, :] * _conv_in(gl)
    for k in range(CONV_D - 1):
        acc = acc + cdw_ref[k:k + 1, :] * _conv_in(std_ref[:, k, :])
    dc = acc + r["conv_d_b"][...]
    ys_ref[3] = jax.nn.silu(_ln(dc, r["d_ln_g"][...], r["d_ln_b"][...])).astype(BF16)

    for i in range(N_BRANCH):
        _merge_branch(i, hb_ref, win_ref, ys_ref, wbr_ref, m_ref)
    _merge_tail(x_src, g1, sh2, sc2, r["n2g"][...], lambda rs: m_ref[rs, :], wout_ref, r["w_r"], r["b_r"],
                r["x1"], h2buf, r["lg"])

    carry[...] = r["carry_in"][...]
    route = _route_rows(r["lg"][...], carry, cap)
    r["route"][...] = route
    r["carry_out"][...] = carry[...]
    slot_v[...] = jnp.zeros_like(slot_v)
    _slots_to_smem(route, slot_v, slot_s, sem_s)
    r["slots"][...] = slot_v[...]
    pltpu.make_async_copy(slot_v.at[pl.ds(0, 2)], slot_s, sem_s).wait()

    def dispatch_row(i, c):
        for k in range(2):
            pltpu.make_async_copy(h2buf.at[pl.ds(i, 1)], xs_hbm.at[pl.ds(slot_s[k, i], 1)], sem_d).start()
        return c
    lax.fori_loop(0, rows, dispatch_row, 0, unroll=DMA_UNROLL)
    for _ in range(2):
        pltpu.make_async_copy(h2buf, xs_hbm.at[pl.ds(0, rows)], sem_d).wait()


def _sample_mixer(x, mod, st_b, st_p, st_d, lw, li, cap, carry_in, xs, prev=None):
    rows = x.shape[0]
    first = prev is None
    full = lambda a: pl.BlockSpec(a.shape, lambda i, sp: (0,) * a.ndim)
    anyspec = pl.BlockSpec(memory_space=pl.ANY)
    names, ins = ["x", "mod", "carry_in", "xs_in", "st_b", "st_p", "st_d"], [x, mod, carry_in, xs, st_b, st_p, st_d]
    in_specs = [full(x), full(mod), full(carry_in), anyspec, full(st_b), full(st_p), full(st_d)]
    if first:
        slots_prev = jnp.zeros((SUBLANES,), I32)
    else:
        route_prev, mod_prev, ys_prev, slots_prev = prev
        names += ["route_prev", "mod_prev", "ys_prev"]
        ins += [route_prev, mod_prev, ys_prev]
        in_specs += [full(route_prev), full(mod_prev), anyspec]
    names += list(SAMPLE_CONSTS)
    ins += [lw[k] for k in SAMPLE_CONSTS]
    in_specs += [_layer_spec(lw[k].shape, li) if k in PER_LAYER else full(lw[k]) for k in SAMPLE_CONSTS]

    mk = lambda last: jax.ShapeDtypeStruct((rows, last), F32)
    out_names = ["x1", "route", "carry_out", "xs", "slots", "cx", "pin", "gl", "v"]
    out_shape = [mk(D_MODEL), mk(LANES), jax.ShapeDtypeStruct((1, LANES), F32),
                 jax.ShapeDtypeStruct(xs.shape, xs.dtype), jax.ShapeDtypeStruct((SUBLANES, rows), I32),
                 mk(MIX_W), mk(MIX_W), mk(MIX_W), mk(MIX_W)]
    out_specs = [pl.BlockSpec(s.shape, lambda i, sp, nd=len(s.shape): (0,) * nd) for s in out_shape]
    out_specs[3] = anyspec
    scratch = {"hb": pltpu.VMEM((rows, D_MODEL), BF16), "proj": pltpu.VMEM((rows, D_MIX_IN), F32),
               "ys": pltpu.VMEM((N_BRANCH, rows, MIX_W), BF16), "m": pltpu.VMEM((rows, D_MODEL), F32),
               "h2buf": pltpu.VMEM((rows, D_MODEL // 2), U32), "lg": pltpu.VMEM((rows, LANES), F32),
               "slot_v": pltpu.VMEM((SUBLANES, rows), I32), "slot_s": pltpu.SMEM((2, rows), I32),
               "carry": pltpu.VMEM((1, LANES), F32), "sem": pltpu.SemaphoreType.DMA((3,))}
    if not first:
        scratch["xin"] = pltpu.VMEM((rows, D_MODEL), F32)
        scratch["gbuf"] = pltpu.VMEM((2, rows, D_MODEL), F32)
    names = tuple(names + out_names + list(scratch))
    outs = pl.pallas_call(
        functools.partial(_sample_mixer_kernel, names=names, first=first, cap=cap),
        grid_spec=pltpu.PrefetchScalarGridSpec(
            num_scalar_prefetch=1, grid=(1,), in_specs=in_specs, out_specs=out_specs,
            scratch_shapes=list(scratch.values())),
        out_shape=out_shape,
        input_output_aliases={1 + names.index("xs_in"): out_names.index("xs")},
        compiler_params=pltpu.CompilerParams(dimension_semantics=("arbitrary",), vmem_limit_bytes=MIXER_VMEM_BYTES,
                                             disable_bounds_checks=True),
        name="sample_mixer",
    )(slots_prev, *ins)
    return dict(zip(out_names, outs))


def _expert_kernel(vb_ref, be_ref, nv_ref, nu_ref, xs_hbm, wg_ref, wu_ref, wd_ref, ys_ref, wgb, wub, wdb, xbuf, xsem):
    b = pl.program_id(0)
    nu = nu_ref[0]
    bm = xbuf.shape[1]

    def fetch(k):
        slot = k % EXPERT_RING
        return pltpu.make_async_copy(xs_hbm.at[pl.ds(pl.multiple_of(vb_ref[k] * bm, bm), bm)], xbuf.at[slot],
                                     xsem.at[slot])

    @pl.when(b == 0)
    def _():
        for k in range(EXPERT_RING - 1):
            @pl.when(k < nu)
            def _():
                fetch(k).start()

    @pl.when(b + EXPERT_RING - 1 < nu)
    def _():
        fetch(b + EXPERT_RING - 1).start()

    @pl.when(b < nu)
    def _():
        fetch(b).wait()
        xs_ref = xbuf.at[b % EXPERT_RING]

        @pl.when(jnp.logical_or(b == 0, be_ref[b] != be_ref[jnp.maximum(b - 1, 0)]))
        def _():
            wgb[...] = wg_ref[0, 0].astype(BF16)
            wub[...] = wu_ref[0, 0].astype(BF16)
            wdb[...] = wd_ref[0, 0].astype(BF16)

        valid = lax.broadcasted_iota(I32, (xs_ref.shape[0], 1), 0) < nv_ref[b]
        xb = _unpack_rows(jnp.where(valid, xs_ref[...], jnp.uint32(0)))
        g = jnp.dot(xb, wgb[...], preferred_element_type=F32)
        u = jnp.dot(xb, wub[...], preferred_element_type=F32)
        ys_ref[...] = jnp.dot((jax.nn.silu(g) * u).astype(BF16), wdb[...], preferred_element_type=F32)


def _experts(xs, counts, cap, li, w_eg, w_eu, w_ed):
    bm = MOE_BM
    n_blocks = -(-xs.shape[0] // bm)
    max_used = min(N_EXPERTS * (cap // bm), -(-(cap * 2) // bm) + N_EXPERTS)
    nblk = (counts + bm - 1) // bm
    ends = jnp.cumsum(nblk)
    n_used = ends[-1:]
    b = jnp.arange(max_used, dtype=I32)
    bc = jnp.minimum(b, n_used[0] - 1)
    be = jnp.minimum(jnp.sum(ends[None, :] <= bc[:, None], axis=1), N_EXPERTS - 1).astype(I32)
    mine = be[:, None] == jnp.arange(N_EXPERTS, dtype=I32)[None, :]
    pick = lambda per_expert: jnp.sum(jnp.where(mine, per_expert[None, :], 0), axis=1)
    local = bc - pick(ends - nblk)
    vblock = (be * (cap // bm) + local).astype(I32)
    nvalid = jnp.clip(pick(counts) - local * bm, 0, bm).astype(I32)
    wspec = lambda shape: pl.BlockSpec((1, 1) + shape, lambda i, vb, be_, nv, nu: (li, be_[i], 0, 0))
    return pl.pallas_call(
        _expert_kernel,
        grid_spec=pltpu.PrefetchScalarGridSpec(
            num_scalar_prefetch=4,
            grid=(max_used,),
            in_specs=[pl.BlockSpec(memory_space=pl.ANY),
                      wspec((D_MODEL, D_EXPERT)), wspec((D_MODEL, D_EXPERT)), wspec((D_EXPERT, D_MODEL))],
            out_specs=pl.BlockSpec((bm, D_MODEL), lambda i, vb, be_, nv, nu: (vb[i], 0)),
            scratch_shapes=[pltpu.VMEM((D_MODEL, D_EXPERT), BF16), pltpu.VMEM((D_MODEL, D_EXPERT), BF16),
                            pltpu.VMEM((D_EXPERT, D_MODEL), BF16),
                            pltpu.VMEM((EXPERT_RING, bm, D_MODEL // 2), U32), pltpu.SemaphoreType.DMA((EXPERT_RING,))]),
        out_shape=jax.ShapeDtypeStruct((n_blocks * bm, D_MODEL), F32),
        compiler_params=pltpu.CompilerParams(dimension_semantics=("arbitrary",)),
        name="experts",
    )(vblock, be, nvalid, n_used.astype(I32), xs, w_eg, w_eu, w_ed)


def _combine_kernel(pos_ref, ys_hbm, x1_ref, route_ref, g2_ref, fg_ref, out_ref, buf, sem):
    tc = x1_ref.shape[0]
    hc = tc // 2
    i = pl.program_id(0)
    n = pl.num_programs(0)
    g2 = g2_ref[...].reshape(-1, D_MODEL)

    def issue(tile, half):
        for row in range(hc):
            for k in range(2):
                p = pos_ref[(tile * 2 + k) * tc + half * hc + row]
                pltpu.make_async_copy(ys_hbm.at[pl.ds(p, 1)], buf.at[half, k, pl.ds(row, 1)], sem.at[half]).start()

    def finish(half):
        for k in range(2):
            pltpu.make_async_copy(ys_hbm.at[pl.ds(0, hc)], buf.at[half, k], sem.at[half]).wait()
        rs = slice(half * hc, (half + 1) * hc)
        x2 = _combine_rows(x1_ref[rs, :], route_ref[rs, :], g2 if g2.shape[0] == 1 else g2[rs], buf.at[half])
        out_ref[rs, :] = _rms(x2, fg_ref[...])

    @pl.when(i == 0)
    def _():
        issue(0, 0)

    issue(i, 1)
    finish(0)

    @pl.when(i + 1 < n)
    def _():
        issue(i + 1, 0)

    finish(1)


def _combine(ys, pos, tc, x1, route, g2, fg):
    t = x1.shape[0]
    if g2.ndim == 3:
        seq = t // g2.shape[0]
        g2spec = pl.BlockSpec((1, 1, D_MODEL), lambda i, pos: (i * tc // seq, 0, 0))
    else:
        g2spec = pl.BlockSpec((tc, D_MODEL), lambda i, pos: (i, 0))
    tokspec = lambda last: pl.BlockSpec((tc, last), lambda i, pos: (i, 0))
    return pl.pallas_call(
        _combine_kernel,
        grid_spec=pltpu.PrefetchScalarGridSpec(
            num_scalar_prefetch=1,
            grid=(t // tc,),
            in_specs=[pl.BlockSpec(memory_space=pl.ANY), tokspec(D_MODEL), tokspec(LANES), g2spec,
                      pl.BlockSpec((1, D_MODEL), lambda i, pos: (0, 0))],
            out_specs=tokspec(D_MODEL),
            scratch_shapes=[pltpu.VMEM((2, 2, tc // 2, D_MODEL), F32), pltpu.SemaphoreType.DMA((2,))]),
        out_shape=jax.ShapeDtypeStruct((t, D_MODEL), F32),
        compiler_params=pltpu.CompilerParams(dimension_semantics=("arbitrary",), disable_bounds_checks=True,
                                             vmem_limit_bytes=MIXER_VMEM_BYTES),
        name="combine",
    )(pos, ys, x1, route, g2, fg)


def _layer_weights(li, norm1_g, norm2_g, w_in, a_ln_g, a_ln_b, a_ws, a_bs, conv_b_w, pool_w, pool_scale,
                   conv_d_w, conv_d_b, d_ln_g, d_ln_b, w_branch, w_out, w_rg, b_rg, w_re, b_re):
    row = lambda a: a[li].reshape(1, -1)
    cg = MIX_W // A_GROUPS
    pad = LANES - N_GROUPS_MOE - N_EXPERTS
    return {
        "n1g": row(norm1_g), "n2g": row(norm2_g), "w_in": w_in.astype(BF16),
        "a_ln_g": row(a_ln_g), "a_ln_b": row(a_ln_b), "a_ws": a_ws[li],
        "a_bs_full": jnp.repeat(a_bs[li].T, cg, axis=1),
        "a_w0": jnp.repeat(a_ws[li, :, 0, 0], cg).reshape(1, MIX_W),
        "a_b0": jnp.repeat(a_bs[li, :, 0], cg).reshape(1, MIX_W),
        "conv_b_w": conv_b_w[li], "pool_w": pool_w[li], "pool_scale": row(pool_scale),
        "conv_b_w8": jnp.broadcast_to(conv_b_w[li][:, None, :], (CONV_B, SUBLANES, MIX_W)),
        "conv_d_w8": jnp.broadcast_to(conv_d_w[li][:, None, :], (CONV_D, SUBLANES, MIX_W)),
        "conv_d_w": conv_d_w[li], "conv_d_b": row(conv_d_b), "d_ln_g": row(d_ln_g), "d_ln_b": row(d_ln_b),
        "w_branch": w_branch.astype(BF16), "w_out": w_out.astype(BF16),
        "w_r": jnp.concatenate([w_rg[li], w_re[li], jnp.zeros((D_MODEL, pad), F32)], axis=1),
        "b_r": jnp.concatenate([b_rg[li], b_re[li], jnp.zeros((pad,), F32)]).reshape(1, LANES),
    }


def _slots(slot_rows):
    return slot_rows.reshape(-1, SUBLANES, slot_rows.shape[-1])[:, :2].reshape(-1)


def kernel(x_prompt, x_sample, state_conv_b, state_pool, state_conv_d, c_prompt, c_sample, w_mod, b_mod, norm1_g, norm2_g, w_in, a_ln_g, a_ln_b, a_ws, a_bs, conv_b_w, pool_w, pool_scale, conv_d_w, conv_d_b, d_ln_g, d_ln_b, w_branch, w_out, w_rg, b_rg, w_re, b_re, w_eg, w_eu, w_ed, final_g):
    depth = w_mod.shape[0]
    nb, seq, _ = x_prompt.shape
    ns = x_sample.shape[0]
    t = nb * seq
    cap = -(-(t + ns) // MOE_BM) * MOE_BM
    mod_all = _modulation(jnp.concatenate([c_prompt, c_sample], axis=0), w_mod, b_mod)
    fg = final_g.reshape(1, D_MODEL)

    xp = x_prompt
    xs_tok = x_sample.reshape(ns, D_MODEL)
    prev_p = prev_s = None
    outs = {k: [] for k in ("bp", "pp", "dp", "bs", "ps", "ds", "vs")}
    for li in range(depth):
        lw = _layer_weights(li, norm1_g, norm2_g, w_in, a_ln_g, a_ln_b, a_ws, a_bs, conv_b_w, pool_w, pool_scale,
                            conv_d_w, conv_d_b, d_ln_g, d_ln_b, w_branch, w_out, w_rg, b_rg, w_re, b_re)
        mod_p = mod_all[li, :nb].reshape(nb, 1, 6 * D_MODEL)
        mod_s = mod_all[li, nb:]

        pm = _prompt_mixer(xp, mod_p, lw, li, cap, prev_p)
        outs["bp"].append(pm["new_b"])
        outs["pp"].append(pm["new_p"])
        outs["dp"].append(pm["new_d"])

        st_b, st_p, st_d = state_conv_b[li], state_pool[li], state_conv_d[li]
        sm = _sample_mixer(xs_tok, mod_s, st_b, st_p, st_d, lw, li, cap,
                           pm["carry_out"], pm["xs"], prev_s)
        outs["bs"].append(jnp.concatenate([st_b[:, 1:], sm["cx"][:, None]], axis=1))
        outs["ps"].append(jnp.concatenate([st_p[:, 1:], sm["pin"][:, None]], axis=1))
        outs["ds"].append(jnp.concatenate([st_d[:, 1:], sm["gl"][:, None]], axis=1))
        outs["vs"].append(sm["v"][:, None])

        counts = sm["carry_out"][0, ROUTE_LANE0:ROUTE_LANE0 + N_EXPERTS].astype(I32)
        ys = _experts(sm["xs"], counts, cap, li, w_eg, w_eu, w_ed)
        route_p = pm["route"]
        prev_p = (route_p, mod_p, ys, _slots(pm["slots"]))
        prev_s = (sm["route"], mod_s, ys, _slots(sm["slots"]))
        xp, xs_tok = pm["x1"], sm["x1"]

    route_p, mod_p, ys, slots_p = prev_p
    route_s, mod_s, _, slots_s = prev_s
    y_p = _combine(ys, slots_p, MIX_TM, xp.reshape(t, D_MODEL), route_p.reshape(t, LANES), mod_p[:, :, 5 * D_MODEL:], fg)
    y_s = _combine(ys, slots_s, ns, xs_tok, route_s, mod_s[:, 5 * D_MODEL:], fg)

    stack = lambda k: jnp.stack(outs[k])
    return (y_p.reshape(nb, seq, D_MODEL), y_s.reshape(ns, 1, D_MODEL), stack("bp"), stack("pp"), stack("dp"),
            stack("bs"), stack("ps"), stack("ds"), stack("vs"))
```
